```python
import jax, jax.numpy as jnp
from jax import lax
import numpy as np

D_MODEL = 4096
BATCH = 1
SEQ = 16384
DEPTH = 2

GLA_HEADS = 4
GLA_DK = D_MODEL // 2
GLA_DV = D_MODEL
GLA_DK_HEAD = GLA_DK // GLA_HEADS
GLA_DV_HEAD = GLA_DV // GLA_HEADS
GLA_GATE_RANK = 16
GLA_GATE_TAU = 16.0
GLA_CHUNK = 64
ATTN_HEAD_DIM = 128
ATTN_HEADS_PER_GROUP = 16
DILATED_GROUPS = ((128, 1), (512, 4), (2048, 16))
N_GROUPS = 3
ATTN_HEADS = N_GROUPS * ATTN_HEADS_PER_GROUP
ATTN_WIDTH = ATTN_HEADS * ATTN_HEAD_DIM
ATTN_OUT = ATTN_HEADS_PER_GROUP * ATTN_HEAD_DIM
ROPE_THETA = 10000.0
D_FF = 4 * D_MODEL
N_BRANCHES = 2
N_MOD = 6
NORM_EPS = 1e-6
IN_SIZES = (GLA_DK, GLA_DK, GLA_DV, GLA_DV, GLA_GATE_RANK, ATTN_WIDTH, ATTN_WIDTH, ATTN_WIDTH)
N_IN = 2 * GLA_DK + 2 * GLA_DV + GLA_GATE_RANK + 3 * ATTN_WIDTH

kernel_name = "hybrid_gla_dilated_attn_adaln_block"


def rms_norm(x, g):
    xf = x.astype(jnp.float32)
    y = xf * lax.rsqrt(jnp.mean(jnp.square(xf), axis=-1, keepdims=True) + NORM_EPS)
    return (y * g.astype(jnp.float32)).astype(x.dtype)


def rope(x, positions):
    half = x.shape[-1] // 2
    inv = ROPE_THETA ** (-jnp.arange(half, dtype=jnp.float32) / half)
    ang = positions.astype(jnp.float32)[..., None] * inv
    cos = jnp.cos(ang)[:, :, None, :]
    sin = jnp.sin(ang)[:, :, None, :]
    x1 = x[..., :half].astype(jnp.float32)
    x2 = x[..., half:].astype(jnp.float32)
    return jnp.concatenate([x1 * cos - x2 * sin, x2 * cos + x1 * sin], axis=-1).astype(x.dtype)


def gla_chunked(q, k, v, log_a):
    B, S, H, dk = q.shape
    dv = v.shape[-1]
    n = S // GLA_CHUNK

    def chunks(t):
        return t.astype(jnp.float32).reshape(B, n, GLA_CHUNK, H, t.shape[-1]).transpose(1, 0, 3, 2, 4)

    causal = jnp.tril(jnp.ones((GLA_CHUNK, GLA_CHUNK), dtype=bool))

    def step(state, inp):
        qc, kc, vc, gc = inp
        b = jnp.cumsum(gc, axis=2)
        inter = jnp.einsum('bhcd,bhde->bhce', qc * jnp.exp(b), state)
        diff = b[:, :, :, None, :] - b[:, :, None, :, :]
        decay = jnp.exp(jnp.where(causal[:, :, None], diff, -jnp.inf))
        scores = jnp.einsum('bhid,bhjd,bhijd->bhij', qc, kc, decay)
        intra = jnp.einsum('bhij,bhje->bhie', scores, vc)
        b_last = b[:, :, -1, :]
        new_state = jnp.exp(b_last)[..., None] * state + jnp.einsum(
            'bhcd,bhce->bhde', kc * jnp.exp(b_last[:, :, None, :] - b), vc)
        return new_state, inter + intra

    state0 = jnp.zeros((B, H, dk, dv), jnp.float32)
    _, o = lax.scan(step, state0, (chunks(q), chunks(k), chunks(v), chunks(log_a)))
    return o.transpose(1, 0, 3, 2, 4).reshape(B, S, H, dv)


def dilated_window_attention(q, k, v, window, dilation):
    B, S, H, Dh = q.shape
    L = window // dilation
    span = L * dilation
    S_pad = -(-S // span) * span
    U = S_pad // dilation
    nb = U // L
    pad = ((0, 0), (0, S_pad - S), (0, 0), (0, 0))

    def to_blocks(t):
        t = jnp.pad(t, pad).reshape(B, U, dilation, H, Dh)
        return t.transpose(0, 2, 3, 1, 4).reshape(B, dilation, H, nb, L, Dh)

    def with_prev(t):
        prev = jnp.pad(t[:, :, :, :-1], ((0, 0), (0, 0), (0, 0), (1, 0), (0, 0), (0, 0)))
        return jnp.concatenate([prev, t], axis=-2)

    qb = to_blocks(q)
    kw = with_prev(to_blocks(k))
    vw = with_prev(to_blocks(v))
    s = jnp.einsum('brhnid,brhnjd->brhnij', qb, kw, preferred_element_type=jnp.float32)
    i = jnp.arange(L)[:, None]
    j = jnp.arange(2 * L)[None, :]
    dist = L + i - j
    band = (dist >= 0) & (dist <= L)
    valid_first = (jnp.arange(nb)[:, None, None] > 0) | (j >= L)[None]
    mask = band[None] & valid_first
    s = jnp.where(mask, s, -jnp.inf)
    m = jnp.max(s, axis=-1, keepdims=True)
    p = jnp.exp(s - m)
    den = jnp.sum(p, axis=-1, keepdims=True)
    o = jnp.einsum('brhnij,brhnjd->brhnid', p, vw.astype(jnp.float32)) / den
    lse = m + jnp.log(den)

    def from_blocks(t):
        e = t.shape[-1]
        t = t.reshape(B, dilation, H, U, e).transpose(0, 3, 1, 2, 4).reshape(B, S_pad, H, e)
        return t[:, :S]

    return from_blocks(o), from_blocks(lse)[..., 0]


def mixer(h, positions, w_in, w_alpha_up, b_alpha, gla_norm, w_branch_gla, w_branch_attn,
          w_gate, b_gate, w_out):
    B, S, _ = h.shape
    proj = h @ w_in
    offsets = []
    acc = 0
    for sz in IN_SIZES[:-1]:
        acc += sz
        offsets.append(acc)
    gq, gk, gv, gr, glr, aq, ak, av = jnp.split(proj, offsets, axis=-1)

    gq = gq.reshape(B, S, GLA_HEADS, GLA_DK_HEAD) * (GLA_DK_HEAD ** -0.5)
    gk = gk.reshape(B, S, GLA_HEADS, GLA_DK_HEAD)
    gv = gv.reshape(B, S, GLA_HEADS, GLA_DV_HEAD)
    alpha_logits = (glr @ w_alpha_up + b_alpha).astype(jnp.float32)
    log_a = (jax.nn.log_sigmoid(alpha_logits) / GLA_GATE_TAU).reshape(B, S, GLA_HEADS, GLA_DK_HEAD)
    o_gla = gla_chunked(gq, gk, gv, log_a)
    o_gla = rms_norm(o_gla, gla_norm).reshape(B, S, GLA_DV).astype(h.dtype) * jax.nn.silu(gr)
    branch_a = o_gla @ w_branch_gla

    aq = rope(aq.reshape(B, S, ATTN_HEADS, ATTN_HEAD_DIM), positions) * (ATTN_HEAD_DIM ** -0.5)
    ak = rope(ak.reshape(B, S, ATTN_HEADS, ATTN_HEAD_DIM), positions)
    av = av.reshape(B, S, ATTN_HEADS, ATTN_HEAD_DIM)
    outs, lses = [], []
    for g, (window, dilation) in enumerate(DILATED_GROUPS):
        sl = slice(g * ATTN_HEADS_PER_GROUP, (g + 1) * ATTN_HEADS_PER_GROUP)
        o_g, lse_g = dilated_window_attention(aq[:, :, sl], ak[:, :, sl], av[:, :, sl], window, dilation)
        outs.append(o_g)
        lses.append(lse_g)
    weights = jax.nn.softmax(jnp.stack(lses, axis=0), axis=0)
    o_attn = jnp.einsum('gbsh,gbshd->bshd', weights, jnp.stack(outs, axis=0))
    branch_b = o_attn.reshape(B, S, ATTN_OUT).astype(h.dtype) @ w_branch_attn

    gates = jax.nn.sigmoid(h @ w_gate + b_gate)
    g_a, g_b = jnp.split(gates, N_BRANCHES, axis=-1)
    return (g_a * branch_a + g_b * branch_b) @ w_out


def setup_inputs(seed: int = 0) -> dict:
    key = jax.random.key(seed)
    ks = jax.random.split(key, 24)
    nrm = jax.random.normal
    f32 = jnp.float32
    offset = jax.random.randint(ks[2], (BATCH, 1), 0, 4096, dtype=jnp.int32)
    positions = offset + jnp.arange(SEQ, dtype=jnp.int32)[None, :]
    return {
        "x": nrm(ks[0], (BATCH, SEQ, D_MODEL), f32),
        "c": nrm(ks[1], (BATCH, D_MODEL), f32),
        "positions": positions,
        "ada_w": nrm(ks[3], (DEPTH, D_MODEL, N_MOD * D_MODEL), f32) * (0.5 * D_MODEL ** -0.5),
        "ada_b": nrm(ks[4], (DEPTH, N_MOD * D_MODEL), f32) * 0.02,
        "norm_mix": 1.0 + 0.02 * nrm(ks[5], (DEPTH, D_MODEL), f32),
        "norm_mlp": 1.0 + 0.02 * nrm(ks[6], (DEPTH, D_MODEL), f32),
        "w_in": nrm(ks[7], (DEPTH, D_MODEL, N_IN), f32) * (D_MODEL ** -0.5),
        "w_alpha_up": nrm(ks[8], (DEPTH, GLA_GATE_RANK, GLA_DK), f32) * (GLA_GATE_RANK ** -0.5),
        "b_alpha": nrm(ks[9], (DEPTH, GLA_DK), f32) * 0.1,
        "gla_norm": 1.0 + 0.02 * nrm(ks[10], (DEPTH, GLA_DV_HEAD), f32),
        "w_branch_gla": nrm(ks[11], (DEPTH, GLA_DV, D_MODEL), f32) * (GLA_DV ** -0.5),
        "w_branch_attn": nrm(ks[12], (DEPTH, ATTN_OUT, D_MODEL), f32) * (ATTN_OUT ** -0.5),
        "w_gate": nrm(ks[13], (DEPTH, D_MODEL, N_BRANCHES * D_MODEL), f32) * (D_MODEL ** -0.5),
        "b_gate": nrm(ks[14], (DEPTH, N_BRANCHES * D_MODEL), f32) * 0.02,
        "w_out": nrm(ks[15], (DEPTH, D_MODEL, D_MODEL), f32) * (D_MODEL ** -0.5),
        "w_mlp_in": nrm(ks[16], (DEPTH, D_MODEL, D_FF), f32) * (D_MODEL ** -0.5),
        "w_mlp_out": nrm(ks[17], (DEPTH, D_FF, D_MODEL), f32) * (D_FF ** -0.5),
        "norm_final": 1.0 + 0.02 * nrm(ks[18], (D_MODEL,), f32),
    }


def reference(x, c, positions, ada_w, ada_b, norm_mix, norm_mlp, w_in, w_alpha_up, b_alpha,
              gla_norm, w_branch_gla, w_branch_attn, w_gate, b_gate, w_out, w_mlp_in, w_mlp_out,
              norm_final):
    cond = jax.nn.silu(c)
    for l in range(DEPTH):
        mod = (cond @ ada_w[l] + ada_b[l])[:, None, :]
        sh_a, sc_a, gt_a, sh_m, sc_m, gt_m = jnp.split(mod, N_MOD, axis=-1)
        h = rms_norm(x, norm_mix[l]) * (1.0 + sc_a) + sh_a
        x = x + gt_a * mixer(h, positions, w_in[l], w_alpha_up[l], b_alpha[l], gla_norm[l],
                             w_branch_gla[l], w_branch_attn[l], w_gate[l], b_gate[l], w_out[l])
        h = rms_norm(x, norm_mlp[l]) * (1.0 + sc_m) + sh_m
        x = x + gt_m * (jnp.square(jax.nn.relu(h @ w_mlp_in[l])) @ w_mlp_out[l])
    return rms_norm(x, norm_final)
```

```python
import functools
from typing import NamedTuple

import jax
import jax.numpy as jnp
from jax import lax
from jax.experimental import pallas as pl
from jax.experimental.pallas import tpu as pltpu

F32 = jnp.float32
BF16 = jnp.bfloat16
LANES = 128
SUBLANES = 8
VMEM_LIMIT = 56 * 1024 * 1024


class Config(NamedTuple):
    d_model: int = 4096
    depth: int = 2
    gla_heads: int = 4
    gate_rank: int = 16
    gate_tau: float = 16.0
    gla_chunk: int = 64
    attn_head_dim: int = 128
    attn_heads_per_group: int = 16
    dilated_groups: tuple = ((128, 1), (512, 4), (2048, 16))
    rope_theta: float = 10000.0
    ff_mult: int = 4
    n_mod: int = 6
    norm_eps: float = 1e-6

    @property
    def gla_dk(self):
        return self.d_model // 2

    @property
    def gla_dv(self):
        return self.d_model

    @property
    def dk_head(self):
        return self.gla_dk // self.gla_heads

    @property
    def dv_head(self):
        return self.gla_dv // self.gla_heads

    @property
    def group_width(self):
        return self.attn_heads_per_group * self.attn_head_dim

    @property
    def attn_width(self):
        return len(self.dilated_groups) * self.group_width

    @property
    def gla_width(self):
        return 2 * self.gla_dk + 2 * self.gla_dv


def _params(sem):
    return pltpu.CompilerParams(dimension_semantics=sem, vmem_limit_bytes=VMEM_LIMIT)


def _sigmoid(x):
    return 1.0 / (1.0 + jnp.exp(-x))


def _dot(a, b):
    return jnp.dot(a, b, preferred_element_type=F32)


def _dot_nt(a, b):
    return lax.dot_general(a, b, (((1,), (1,)), ((), ())), preferred_element_type=F32)


def _dot_tn(a, b):
    return lax.dot_general(a, b, (((0,), (0,)), ((), ())), preferred_element_type=F32)


def _dot_f32(a, b):
    return jnp.dot(a, b, preferred_element_type=F32, precision=lax.Precision.HIGHEST)


def _tile(n, pref):
    t = min(n, pref)
    assert n % t == 0, (n, pref)
    return t


def _ada_kernel(c_ref, w_ref, b_ref, o_ref, *, rows):
    d, tn = w_ref.shape

    def body(i, acc):
        r = pl.multiple_of(i * rows, rows)
        cc = c_ref[pl.ds(r, rows), :]
        cond = cc * _sigmoid(cc)
        p = w_ref[pl.ds(r, rows), :] * cond
        return acc + jnp.sum(p.reshape(rows // SUBLANES, SUBLANES, tn), axis=0)

    acc = lax.fori_loop(0, d // rows, body, jnp.zeros((SUBLANES, tn), F32))
    o_ref[...] = jnp.sum(acc, axis=0, keepdims=True) + b_ref[...]


def _ada_mod(c_col, ada_w, ada_b3):
    depth, d, n = ada_w.shape
    tn = _tile(n, 512)
    rows = _tile(d, 256)
    return pl.pallas_call(
        functools.partial(_ada_kernel, rows=rows),
        grid=(depth, n // tn),
        in_specs=[
            pl.BlockSpec((d, 1), lambda l, j: (0, 0)),
            pl.BlockSpec((None, d, tn), lambda l, j: (l, 0, j)),
            pl.BlockSpec((None, 1, tn), lambda l, j: (l, 0, j)),
        ],
        out_specs=pl.BlockSpec((None, 1, tn), lambda l, j: (l, 0, j)),
        out_shape=jax.ShapeDtypeStruct((depth, 1, n), F32),
        compiler_params=_params(("parallel", "parallel")),
        name="ada_mod",
    )(c_col, ada_w, ada_b3)


def _rope_kernel(p_ref, inv_ref, sgn_ref, c_ref, s_ref):
    ang = p_ref[...].astype(F32) * inv_ref[...]
    c_ref[...] = jnp.cos(ang)
    s_ref[...] = jnp.sin(ang) * sgn_ref[...]


def _rope_tables(pos_col, cfg):
    s = pos_col.shape[0]
    hd = cfg.attn_head_dim
    half = hd // 2
    inv = cfg.rope_theta ** (-jnp.arange(half, dtype=F32) / half)
    inv2 = jnp.concatenate([inv, inv]).reshape(1, hd)
    sgn = jnp.concatenate([-jnp.ones((half,), F32), jnp.ones((half,), F32)]).reshape(1, hd)
    ts = _tile(s, 1024)
    spec_t = pl.BlockSpec((ts, hd), lambda i: (i, 0))
    spec_c = pl.BlockSpec((1, hd), lambda i: (0, 0))
    return pl.pallas_call(
        _rope_kernel,
        grid=(s // ts,),
        in_specs=[pl.BlockSpec((ts, 1), lambda i: (i, 0)), spec_c, spec_c],
        out_specs=[spec_t, spec_t],
        out_shape=[jax.ShapeDtypeStruct((s, hd), F32)] * 2,
        compiler_params=_params(("parallel",)),
        name="rope_tables",
    )(pos_col, inv2, sgn)


def _norm_mod_kernel(x_ref, g_ref, sc_ref, sh_ref, o_ref, *, eps):
    x = x_ref[...]
    ms = jnp.mean(x * x, axis=-1, keepdims=True)
    y = x * lax.rsqrt(ms + eps) * g_ref[...]
    o_ref[...] = (y * (1.0 + sc_ref[...]) + sh_ref[...]).astype(o_ref.dtype)


def _norm_mod(x, g3, mod, layer, i_shift, i_scale, cfg):
    s, d = x.shape
    ts = _tile(s, 256)
    return pl.pallas_call(
        functools.partial(_norm_mod_kernel, eps=cfg.norm_eps),
        grid=(s // ts,),
        in_specs=[
            pl.BlockSpec((ts, d), lambda i: (i, 0)),
            pl.BlockSpec((None, 1, d), lambda i: (layer, 0, 0)),
            pl.BlockSpec((None, 1, d), lambda i: (layer, 0, i_scale)),
            pl.BlockSpec((None, 1, d), lambda i: (layer, 0, i_shift)),
        ],
        out_specs=pl.BlockSpec((ts, d), lambda i: (i, 0)),
        out_shape=jax.ShapeDtypeStruct((s, d), BF16),
        compiler_params=_params(("parallel",)),
        name="norm_mod",
    )(x, g3, mod, mod)


def _final_norm_kernel(x_ref, g_ref, o_ref, *, eps):
    x = x_ref[...]
    ms = jnp.mean(x * x, axis=-1, keepdims=True)
    o_ref[...] = x * lax.rsqrt(ms + eps) * g_ref[...]


def _final_norm(x, g2, cfg):
    s, d = x.shape
    ts = _tile(s, 256)
    return pl.pallas_call(
        functools.partial(_final_norm_kernel, eps=cfg.norm_eps),
        grid=(s // ts,),
        in_specs=[pl.BlockSpec((ts, d), lambda i: (i, 0)), pl.BlockSpec((1, d), lambda i: (0, 0))],
        out_specs=pl.BlockSpec((ts, d), lambda i: (i, 0)),
        out_shape=jax.ShapeDtypeStruct((s, d), F32),
        compiler_params=_params(("parallel",)),
        name="final_norm",
    )(x, g2)


def _proj_kernel(x_ref, w_ref, cos_ref, sin_ref, o_ref, *, j_rope0, j_rope_k, j_rope1, q_scale, hd):
    acc = _dot(x_ref[...], w_ref[...])
    j = pl.program_id(1)
    is_rope = jnp.logical_and(j >= j_rope0, j < j_rope1)

    @pl.when(is_rope)
    def _():
        scale = jnp.where(j < j_rope_k, q_scale, 1.0).astype(F32)
        cos = cos_ref[...] * scale
        sin = sin_ref[...] * scale
        for c in range(acc.shape[1] // hd):
            a = acc[:, c * hd:(c + 1) * hd]
            o_ref[:, c * hd:(c + 1) * hd] = (a * cos + pltpu.roll(a, hd // 2, 1) * sin).astype(o_ref.dtype)

    @pl.when(jnp.logical_not(is_rope))
    def _():
        o_ref[...] = acc.astype(o_ref.dtype)


def _proj(h, w, cos_t, sin_t, cfg):
    m, k = h.shape
    n = w.shape[1]
    hd = cfg.attn_head_dim
    tm = _tile(m, 1024)
    tn = _tile(cfg.group_width, 1024)
    assert cfg.gla_width % tn == 0 and n % tn == 0
    j0 = cfg.gla_width // tn
    jk = j0 + cfg.attn_width // tn
    j1 = jk + cfg.attn_width // tn
    return pl.pallas_call(
        functools.partial(_proj_kernel, j_rope0=j0, j_rope_k=jk, j_rope1=j1,
                          q_scale=float(hd) ** -0.5, hd=hd),
        grid=(m // tm, n // tn),
        in_specs=[
            pl.BlockSpec((tm, k), lambda i, j: (i, 0)),
            pl.BlockSpec((k, tn), lambda i, j: (0, j)),
            pl.BlockSpec((tm, hd), lambda i, j: (i, 0)),
            pl.BlockSpec((tm, hd), lambda i, j: (i, 0)),
        ],
        out_specs=pl.BlockSpec((tm, tn), lambda i, j: (i, j)),
        out_shape=jax.ShapeDtypeStruct((m, n), BF16),
        compiler_params=_params(("parallel", "parallel")),
        name="in_proj",
    )(h, w, cos_t, sin_t)


def _mm_plain_kernel(x_ref, w_ref, o_ref):
    o_ref[...] = _dot(x_ref[...], w_ref[...]).astype(o_ref.dtype)


def _mm_relu2_kernel(x_ref, w_ref, o_ref):
    a = jnp.maximum(_dot(x_ref[...], w_ref[...]), 0.0)
    o_ref[...] = (a * a).astype(o_ref.dtype)


def _mm_resid_kernel(x_ref, w_ref, r_ref, g_ref, o_ref):
    o_ref[...] = r_ref[...] + g_ref[...] * _dot(x_ref[...], w_ref[...])


def _mm_resid_acc_kernel(x_ref, w_ref, r_ref, g_ref, o_ref, acc_ref):
    kk = pl.program_id(2)

    @pl.when(kk == 0)
    def _():
        acc_ref[...] = jnp.zeros_like(acc_ref)

    acc_ref[...] += _dot(x_ref[...], w_ref[...])

    @pl.when(kk == pl.num_programs(2) - 1)
    def _():
        o_ref[...] = r_ref[...] + g_ref[...] * acc_ref[...]


def _mm(kernel_fn, x, w, out_dtype, *, tm=1024, tn=1024, name):
    m, k = x.shape
    n = w.shape[1]
    tm, tn = _tile(m, tm), _tile(n, tn)
    return pl.pallas_call(
        kernel_fn,
        grid=(m // tm, n // tn),
        in_specs=[pl.BlockSpec((tm, k), lambda i, j: (i, 0)), pl.BlockSpec((k, tn), lambda i, j: (0, j))],
        out_specs=pl.BlockSpec((tm, tn), lambda i, j: (i, j)),
        out_shape=jax.ShapeDtypeStruct((m, n), out_dtype),
        compiler_params=_params(("parallel", "parallel")),
        name=name,
    )(x, w)


def _mm_resid(x, w, resid, mod, layer, i_gate, *, name, tk=2048):
    m, k = x.shape
    n = w.shape[1]
    tm, tn = _tile(m, 1024), _tile(n, 1024)
    gate_blocks = n // tn
    if k <= 4096:
        return pl.pallas_call(
            _mm_resid_kernel,
            grid=(m // tm, n // tn),
            in_specs=[
                pl.BlockSpec((tm, k), lambda i, j: (i, 0)),
                pl.BlockSpec((k, tn), lambda i, j: (0, j)),
                pl.BlockSpec((tm, tn), lambda i, j: (i, j)),
                pl.BlockSpec((None, 1, tn), lambda i, j: (layer, 0, i_gate * gate_blocks + j)),
            ],
            out_specs=pl.BlockSpec((tm, tn), lambda i, j: (i, j)),
            out_shape=jax.ShapeDtypeStruct((m, n), F32),
            compiler_params=_params(("parallel", "parallel")),
            name=name,
        )(x, w, resid, mod)
    tk = _tile(k, tk)
    return pl.pallas_call(
        _mm_resid_acc_kernel,
        grid=(m // tm, n // tn, k // tk),
        in_specs=[
            pl.BlockSpec((tm, tk), lambda i, j, kk: (i, kk)),
            pl.BlockSpec((tk, tn), lambda i, j, kk: (kk, j)),
            pl.BlockSpec((tm, tn), lambda i, j, kk: (i, j)),
            pl.BlockSpec((None, 1, tn), lambda i, j, kk: (layer, 0, i_gate * gate_blocks + j)),
        ],
        out_specs=pl.BlockSpec((tm, tn), lambda i, j, kk: (i, j)),
        out_shape=jax.ShapeDtypeStruct((m, n), F32),
        scratch_shapes=[pltpu.VMEM((tm, tn), F32)],
        compiler_params=_params(("parallel", "parallel", "arbitrary")),
        name=name,
    )(x, w, resid, mod)


def _merge_kernel(h_ref, og_ref, oa_ref, wga_ref, wgb_ref, bga_ref, bgb_ref, wa_ref, wb_ref, o_ref):
    h = h_ref[...]
    ga = _sigmoid(_dot(h, wga_ref[...]) + bga_ref[...])
    gb = _sigmoid(_dot(h, wgb_ref[...]) + bgb_ref[...])
    a = _dot(og_ref[...], wa_ref[...])
    b = _dot(oa_ref[...], wb_ref[...])
    o_ref[...] = (ga * a + gb * b).astype(o_ref.dtype)


def _merge(h, o_gla, o_attn, w_gate, b_gate2, w_a, w_b):
    m, d = h.shape
    tm, tn = _tile(m, 512), _tile(d, 512)
    nb = d // tn
    x_spec = lambda kdim: pl.BlockSpec((tm, kdim), lambda j, i: (i, 0))
    w_spec = lambda kdim, off: pl.BlockSpec((kdim, tn), lambda j, i: (0, j + off))
    b_spec = lambda off: pl.BlockSpec((1, tn), lambda j, i: (0, j + off))
    return pl.pallas_call(
        _merge_kernel,
        grid=(nb, m // tm),
        in_specs=[
            x_spec(d), x_spec(o_gla.shape[1]), x_spec(o_attn.shape[1]),
            w_spec(d, 0), w_spec(d, nb), b_spec(0), b_spec(nb),
            w_spec(w_a.shape[0], 0), w_spec(w_b.shape[0], 0),
        ],
        out_specs=pl.BlockSpec((tm, tn), lambda j, i: (i, j)),
        out_shape=jax.ShapeDtypeStruct((m, d), BF16),
        compiler_params=_params(("parallel", "parallel")),
        name="gated_merge",
    )(h, o_gla, o_attn, w_gate, w_gate, b_gate2, b_gate2, w_a, w_b)


def _gla_kernel(q_ref, k_ref, v_ref, gr_ref, glr_ref, wa_ref, ba_ref, gn_ref, o_ref, st_ref,
                *, chunk, tau, q_scale, eps):
    rows, dk = q_ref.shape
    sub = SUBLANES

    @pl.when(pl.program_id(1) == 0)
    def _():
        st_ref[...] = jnp.zeros_like(st_ref)

    ri = lax.broadcasted_iota(jnp.int32, (chunk, chunk), 0)
    ci = lax.broadcasted_iota(jnp.int32, (chunk, chunk), 1)
    tril = (ri >= ci).astype(F32)
    row_dk = lax.broadcasted_iota(jnp.int32, (chunk, dk), 0)
    sub_i = lax.broadcasted_iota(jnp.int32, (sub, chunk), 0)
    lane_i = lax.broadcasted_iota(jnp.int32, (sub, chunk), 1)

    def chunk_body(c, carry):
        r0 = pl.multiple_of(c * chunk, chunk)
        rs = pl.ds(r0, chunk)
        q = q_ref[rs, :].astype(F32) * q_scale
        k = k_ref[rs, :].astype(F32)
        vb = v_ref[rs, :]
        logits = _dot_f32(glr_ref[rs, :], wa_ref[...]) + ba_ref[...]
        g = (jnp.minimum(logits, 0.0) - jnp.log(1.0 + jnp.exp(-jnp.abs(logits)))) * (1.0 / tau)
        b = _dot_f32(tril, g)
        b_last = b[chunk - 1:chunk, :]

        st = st_ref[...]
        qe = (q * jnp.exp(b)).astype(BF16)
        o = _dot_nt(qe, st.astype(BF16))

        s = jnp.zeros((chunk, chunk), F32)
        g_blk = chunk // 2
        while g_blk >= sub:
            span = 2 * g_blk
            ref = jnp.concatenate(
                [jnp.broadcast_to(b[s0 + g_blk - 1:s0 + g_blk, :], (span, dk)) for s0 in range(0, chunk, span)],
                axis=0)
            upper = (row_dk & g_blk) != 0
            qg = jnp.where(upper, q * jnp.exp(jnp.minimum(b - ref, 0.0)), 0.0).astype(BF16)
            kg = jnp.where(upper, 0.0, k * jnp.exp(jnp.minimum(ref - b, 0.0))).astype(BF16)
            same = (ri ^ ci) < span
            s = s + jnp.where(same, _dot_nt(qg, kg), 0.0)
            g_blk //= 2
        diag_blocks = []
        for blk in range(chunk // sub):
            lo = blk * sub
            qi, ki, bi = q[lo:lo + sub, :], k[lo:lo + sub, :], b[lo:lo + sub, :]
            acc = jnp.zeros((sub, chunk), F32)
            for jj in range(sub):
                t = qi * ki[jj:jj + 1, :] * jnp.exp(jnp.minimum(bi - bi[jj:jj + 1, :], 0.0))
                col = jnp.sum(t, axis=1, keepdims=True)
                keep = jnp.logical_and(lane_i == lo + jj, sub_i >= jj)
                acc = jnp.where(keep, col, acc)
            diag_blocks.append(acc)
        s = s + jnp.concatenate(diag_blocks, axis=0)
        o = o + _dot(s.astype(BF16), vb)

        kd = (k * jnp.exp(b_last - b)).astype(BF16)
        st_ref[...] = st * jnp.exp(b_last) + _dot_tn(vb, kd)

        ms = jnp.mean(o * o, axis=-1, keepdims=True)
        y = o * lax.rsqrt(ms + eps) * gn_ref[...]
        gr = gr_ref[rs, :].astype(F32)
        o_ref[rs, :] = (y * (gr * _sigmoid(gr))).astype(o_ref.dtype)
        return carry

    lax.fori_loop(0, rows // chunk, chunk_body, 0)


def _gla(proj, glr, w_alpha_pad, b_alpha2, gla_norm2, cfg):
    s = proj.shape[0]
    nh, dk, dv = cfg.gla_heads, cfg.dk_head, cfg.dv_head
    rows = _tile(s, 256)
    assert rows % cfg.gla_chunk == 0 and cfg.gla_chunk % (2 * SUBLANES) == 0
    assert cfg.gla_chunk & (cfg.gla_chunk - 1) == 0, "chunk must be a power of two"
    k_off = cfg.gla_dk // dk
    v_off = (2 * cfg.gla_dk) // dv
    r_off = (2 * cfg.gla_dk + cfg.gla_dv) // dv
    assert (2 * cfg.gla_dk) % dv == 0
    return pl.pallas_call(
        functools.partial(_gla_kernel, chunk=cfg.gla_chunk, tau=cfg.gate_tau,
                          q_scale=float(dk) ** -0.5, eps=cfg.norm_eps),
        grid=(nh, s // rows),
        in_specs=[
            pl.BlockSpec((rows, dk), lambda h, t: (t, h)),
            pl.BlockSpec((rows, dk), lambda h, t: (t, k_off + h)),
            pl.BlockSpec((rows, dv), lambda h, t: (t, v_off + h)),
            pl.BlockSpec((rows, dv), lambda h, t: (t, r_off + h)),
            pl.BlockSpec((rows, LANES), lambda h, t: (t, 0)),
            pl.BlockSpec((LANES, dk), lambda h, t: (0, h)),
            pl.BlockSpec((1, dk), lambda h, t: (0, h)),
            pl.BlockSpec((1, dv), lambda h, t: (0, 0)),
        ],
        out_specs=pl.BlockSpec((rows, dv), lambda h, t: (t, h)),
        out_shape=jax.ShapeDtypeStruct((s, cfg.gla_dv), BF16),
        scratch_shapes=[pltpu.VMEM((dv, dk), F32)],
        compiler_params=_params(("parallel", "arbitrary")),
        name="gla",
    )(proj, proj, proj, proj, glr, w_alpha_pad, b_alpha2, gla_norm2)


def _attn_kernel(q_ref, kp_ref, kc_ref, vp_ref, vc_ref, o_ref, lse_ref, *, heads, hd):
    blk = q_ref.shape[0]
    n = pl.program_id(1)
    qi = lax.broadcasted_iota(jnp.int32, (blk, 2 * blk), 0)
    kj = lax.broadcasted_iota(jnp.int32, (blk, 2 * blk), 1)
    dist = blk + qi - kj
    band = jnp.logical_and(dist >= 0, dist <= blk)
    mask = jnp.logical_and(band, jnp.logical_or(kj >= blk, n > 0))
    lane = lax.broadcasted_iota(jnp.int32, (blk, LANES), 1)
    lse_tile = jnp.zeros((blk, LANES), F32)
    for hh in range(heads):
        sl = slice(hh * hd, (hh + 1) * hd)
        q = q_ref[:, sl]
        s = jnp.concatenate([_dot_nt(q, kp_ref[:, sl]), _dot_nt(q, kc_ref[:, sl])], axis=1)
        s = jnp.where(mask, s, -jnp.inf)
        m = jnp.max(s, axis=1, keepdims=True)
        p = jnp.exp(s - m)
        den = jnp.sum(p, axis=1, keepdims=True)
        pb = p.astype(BF16)
        o = _dot(pb[:, :blk], vp_ref[:, sl]) + _dot(pb[:, blk:], vc_ref[:, sl])
        o_ref[:, sl] = (o / den).astype(o_ref.dtype)
        lse_tile = jnp.where(lane == hh, m + jnp.log(den), lse_tile)
    lse_ref[...] = lse_tile


def _attn_group(proj, g, window, dilation, cfg):
    s, w = proj.shape
    blk = window // dilation
    assert s % (blk * dilation) == 0
    u = s // dilation
    nb = u // blk
    gw = cfg.group_width
    assert w % gw == 0 and cfg.gla_width % gw == 0 and cfg.attn_heads_per_group <= LANES
    nblk = w // gw
    q_off = cfg.gla_width // gw + g
    k_off = q_off + cfg.attn_width // gw
    v_off = k_off + cfg.attn_width // gw
    pv = proj.reshape(u, dilation * w)

    def cur(off):
        return pl.BlockSpec((blk, gw), lambda r, n: (n, r * nblk + off))

    def prev(off):
        return pl.BlockSpec((blk, gw), lambda r, n: (jnp.maximum(n - 1, 0), r * nblk + off))

    o, lse = pl.pallas_call(
        functools.partial(_attn_kernel, heads=cfg.attn_heads_per_group, hd=cfg.attn_head_dim),
        grid=(dilation, nb),
        in_specs=[cur(q_off), prev(k_off), cur(k_off), prev(v_off), cur(v_off)],
        out_specs=[pl.BlockSpec((blk, gw), lambda r, n: (n, r)),
                   pl.BlockSpec((blk, LANES), lambda r, n: (n, r))],
        out_shape=[jax.ShapeDtypeStruct((u, dilation * gw), BF16),
                   jax.ShapeDtypeStruct((u, dilation * LANES), F32)],
        compiler_params=_params(("parallel", "arbitrary")),
        name=f"dilated_attn_{g}",
    )(pv, pv, pv, pv, pv)
    return o.reshape(s, gw), lse.reshape(s, LANES)


def _combine_kernel(*refs, groups, heads, hd):
    o_refs, l_refs, out_ref = refs[:groups], refs[groups:2 * groups], refs[2 * groups]
    ls = [r[...] for r in l_refs]
    m = functools.reduce(jnp.maximum, ls)
    es = [jnp.exp(l - m) for l in ls]
    inv = 1.0 / functools.reduce(lambda a, b: a + b, es)
    ws = [e * inv for e in es]
    for hh in range(heads):
        sl = slice(hh * hd, (hh + 1) * hd)
        acc = ws[0][:, hh:hh + 1] * o_refs[0][:, sl].astype(F32)
        for gi in range(1, groups):
            acc = acc + ws[gi][:, hh:hh + 1] * o_refs[gi][:, sl].astype(F32)
        out_ref[:, sl] = acc.astype(out_ref.dtype)


def _combine(outs, lses, cfg):
    s, gw = outs[0].shape
    ts = _tile(s, 512)
    ng = len(outs)
    return pl.pallas_call(
        functools.partial(_combine_kernel, groups=ng, heads=cfg.attn_heads_per_group, hd=cfg.attn_head_dim),
        grid=(s // ts,),
        in_specs=[pl.BlockSpec((ts, gw), lambda i: (i, 0))] * ng + [pl.BlockSpec((ts, LANES), lambda i: (i, 0))] * ng,
        out_specs=pl.BlockSpec((ts, gw), lambda i: (i, 0)),
        out_shape=jax.ShapeDtypeStruct((s, gw), BF16),
        compiler_params=_params(("parallel",)),
        name="attn_combine",
    )(*outs, *lses)


def _forward(cfg, x, c, positions, ada_w, ada_b, norm_mix, norm_mlp, w_in, w_alpha_up, b_alpha,
             gla_norm, w_branch_gla, w_branch_attn, w_gate, b_gate, w_out, w_mlp_in, w_mlp_out,
             norm_final):
    bsz, s, d = x.shape
    assert bsz == 1 and d == cfg.d_model
    depth = cfg.depth
    x2 = x.reshape(s, d)
    mod = _ada_mod(c.reshape(d, 1), ada_w, ada_b.reshape(depth, 1, -1))
    cos_t, sin_t = _rope_tables(positions.reshape(s, 1), cfg)
    g_mix = norm_mix.reshape(depth, 1, d)
    g_mlp = norm_mlp.reshape(depth, 1, d)
    gw, rank = cfg.gla_width, cfg.gate_rank

    for l in range(depth):
        w_main = jnp.concatenate([w_in[l, :, :gw], w_in[l, :, gw + rank:]], axis=1).astype(BF16)
        w_glr = jnp.pad(w_in[l, :, gw:gw + rank], ((0, 0), (0, LANES - rank))).astype(BF16)
        w_alpha_pad = jnp.pad(w_alpha_up[l], ((0, LANES - rank), (0, 0)))

        h = _norm_mod(x2, g_mix, mod, l, 0, 1, cfg)
        proj = _proj(h, w_main, cos_t, sin_t, cfg)
        glr = _mm(_mm_plain_kernel, h, w_glr, F32, tn=LANES, name="gate_lowrank")
        o_gla = _gla(proj, glr, w_alpha_pad, b_alpha[l].reshape(1, -1), gla_norm[l].reshape(1, -1), cfg)
        outs, lses = [], []
        for g, (window, dilation) in enumerate(cfg.dilated_groups):
            o_g, lse_g = _attn_group(proj, g, window, dilation, cfg)
            outs.append(o_g)
            lses.append(lse_g)
        o_attn = _combine(outs, lses, cfg)
        merged = _merge(h, o_gla, o_attn, w_gate[l].astype(BF16), b_gate[l].reshape(1, -1),
                        w_branch_gla[l].astype(BF16), w_branch_attn[l].astype(BF16))
        x2 = _mm_resid(merged, w_out[l].astype(BF16), x2, mod, l, 2, name="out_proj")

        h = _norm_mod(x2, g_mlp, mod, l, 3, 4, cfg)
        a = _mm(_mm_relu2_kernel, h, w_mlp_in[l].astype(BF16), BF16, name="mlp_in")
        x2 = _mm_resid(a, w_mlp_out[l].astype(BF16), x2, mod, l, 5, name="mlp_out")

    return _final_norm(x2, norm_final.reshape(1, d), cfg).reshape(bsz, s, d)


def kernel(x, c, positions, ada_w, ada_b, norm_mix, norm_mlp, w_in, w_alpha_up, b_alpha, gla_norm,
           w_branch_gla, w_branch_attn, w_gate, b_gate, w_out, w_mlp_in, w_mlp_out, norm_final):
    return _forward(Config(), x, c, positions, ada_w, ada_b, norm_mix, norm_mlp, w_in, w_alpha_up,
                    b_alpha, gla_norm, w_branch_gla, w_branch_attn, w_gate, b_gate, w_out, w_mlp_in,
                    w_mlp_out, norm_final)
```

```python
import functools
from typing import NamedTuple

import jax
import jax.numpy as jnp
from jax import lax
from jax.experimental import pallas as pl
from jax.experimental.pallas import tpu as pltpu

F32 = jnp.float32
BF16 = jnp.bfloat16
LANES = 128
SUBLANES = 8
VMEM_LIMIT = 56 * 1024 * 1024


class Config(NamedTuple):
    d_model: int = 4096
    depth: int = 2
    gla_heads: int = 4
    gate_rank: int = 16
    gate_tau: float = 16.0
    gla_chunk: int = 64
    attn_head_dim: int = 128
    attn_heads_per_group: int = 16
    dilated_groups: tuple = ((128, 1), (512, 4), (2048, 16))
    rope_theta: float = 10000.0
    ff_mult: int = 4
    n_mod: int = 6
    norm_eps: float = 1e-6

    @property
    def gla_dk(self):
        return self.d_model // 2

    @property
    def gla_dv(self):
        return self.d_model

    @property
    def dk_head(self):
        return self.gla_dk // self.gla_heads

    @property
    def dv_head(self):
        return self.gla_dv // self.gla_heads

    @property
    def group_width(self):
        return self.attn_heads_per_group * self.attn_head_dim

    @property
    def attn_width(self):
        return len(self.dilated_groups) * self.group_width

    @property
    def gla_width(self):
        return 2 * self.gla_dk + 2 * self.gla_dv


def _params(sem):
    return pltpu.CompilerParams(dimension_semantics=sem, vmem_limit_bytes=VMEM_LIMIT)


def _sigmoid(x):
    return 1.0 / (1.0 + jnp.exp(-x))


def _dot(a, b):
    return jnp.dot(a, b, preferred_element_type=F32)


def _dot_nt(a, b):
    return lax.dot_general(a, b, (((1,), (1,)), ((), ())), preferred_element_type=F32)


def _dot_tn(a, b):
    return lax.dot_general(a, b, (((0,), (0,)), ((), ())), preferred_element_type=F32)


def _dot_f32(a, b):
    return jnp.dot(a, b, preferred_element_type=F32, precision=lax.Precision.HIGHEST)


def _tile(n, pref):
    t = min(n, pref)
    assert n % t == 0, (n, pref)
    return t


def _perm_matrix(size, inner, dtype):
    outer = size // inner
    assert inner * outer == size and inner & (inner - 1) == 0 and outer & (outer - 1) == 0
    i = lax.broadcasted_iota(jnp.int32, (size, size), 0)
    j = lax.broadcasted_iota(jnp.int32, (size, size), 1)
    src = (i & (inner - 1)) * outer + (i >> (inner.bit_length() - 1))
    return (j == src).astype(dtype)


def _ada_kernel(c_ref, w_ref, b_ref, o_ref, *, rows):
    d, tn = w_ref.shape

    def body(i, acc):
        r = pl.multiple_of(i * rows, rows)
        cc = c_ref[pl.ds(r, rows), :]
        cond = cc * _sigmoid(cc)
        p = w_ref[pl.ds(r, rows), :] * cond
        return acc + jnp.sum(p.reshape(rows // SUBLANES, SUBLANES, tn), axis=0)

    acc = lax.fori_loop(0, d // rows, body, jnp.zeros((SUBLANES, tn), F32))
    o_ref[...] = jnp.sum(acc, axis=0, keepdims=True) + b_ref[...]


def _ada_mod(c_col, ada_w, ada_b3):
    depth, d, n = ada_w.shape
    tn = _tile(n, 512)
    rows = _tile(d, 256)
    return pl.pallas_call(
        functools.partial(_ada_kernel, rows=rows),
        grid=(depth, n // tn),
        in_specs=[
            pl.BlockSpec((d, 1), lambda l, j: (0, 0)),
            pl.BlockSpec((None, d, tn), lambda l, j: (l, 0, j)),
            pl.BlockSpec((None, 1, tn), lambda l, j: (l, 0, j)),
        ],
        out_specs=pl.BlockSpec((None, 1, tn), lambda l, j: (l, 0, j)),
        out_shape=jax.ShapeDtypeStruct((depth, 1, n), F32),
        compiler_params=_params(("parallel", "parallel")),
        name="ada_mod",
    )(c_col, ada_w, ada_b3)


def _rope_kernel(p_ref, inv_ref, sgn_ref, c_ref, s_ref):
    ang = p_ref[...].astype(F32) * inv_ref[...]
    c_ref[...] = jnp.cos(ang)
    s_ref[...] = jnp.sin(ang) * sgn_ref[...]


def _rope_tables(pos_col, cfg):
    s = pos_col.shape[0]
    hd = cfg.attn_head_dim
    half = hd // 2
    inv = cfg.rope_theta ** (-jnp.arange(half, dtype=F32) / half)
    inv2 = jnp.concatenate([inv, inv]).reshape(1, hd)
    sgn = jnp.concatenate([-jnp.ones((half,), F32), jnp.ones((half,), F32)]).reshape(1, hd)
    ts = _tile(s, 1024)
    spec_t = pl.BlockSpec((ts, hd), lambda i: (i, 0))
    spec_c = pl.BlockSpec((1, hd), lambda i: (0, 0))
    return pl.pallas_call(
        _rope_kernel,
        grid=(s // ts,),
        in_specs=[pl.BlockSpec((ts, 1), lambda i: (i, 0)), spec_c, spec_c],
        out_specs=[spec_t, spec_t],
        out_shape=[jax.ShapeDtypeStruct((s, hd), F32)] * 2,
        compiler_params=_params(("parallel",)),
        name="rope_tables",
    )(pos_col, inv2, sgn)


def _norm_mod_kernel(x_ref, g_ref, sc_ref, sh_ref, o_ref, *rest, eps, dilations):
    x = x_ref[...]
    ms = jnp.mean(x * x, axis=-1, keepdims=True)
    y = x * lax.rsqrt(ms + eps) * g_ref[...]
    hb = (y * (1.0 + sc_ref[...]) + sh_ref[...]).astype(o_ref.dtype)
    o_ref[...] = hb
    ts = hb.shape[0]
    for p_ref, dil in zip(rest, dilations):
        n = ts // dil
        hp = _dot(_perm_matrix(ts, n, hb.dtype), hb).astype(p_ref.dtype)
        for r in range(dil):
            p_ref[r] = hp[r * n:(r + 1) * n, :]


def _norm_mod(x, g3, mod, layer, i_shift, i_scale, cfg, dilations=()):
    s, d = x.shape
    ts = _tile(s, 256)
    out_specs = [pl.BlockSpec((ts, d), lambda i: (i, 0))]
    out_shape = [jax.ShapeDtypeStruct((s, d), BF16)]
    for dil in dilations:
        assert ts % (dil * 2 * SUBLANES) == 0
        out_specs.append(pl.BlockSpec((dil, ts // dil, d), lambda i: (0, i, 0)))
        out_shape.append(jax.ShapeDtypeStruct((dil, s // dil, d), BF16))
    return pl.pallas_call(
        functools.partial(_norm_mod_kernel, eps=cfg.norm_eps, dilations=tuple(dilations)),
        grid=(s // ts,),
        in_specs=[
            pl.BlockSpec((ts, d), lambda i: (i, 0)),
            pl.BlockSpec((None, 1, d), lambda i: (layer, 0, 0)),
            pl.BlockSpec((None, 1, d), lambda i: (layer, 0, i_scale)),
            pl.BlockSpec((None, 1, d), lambda i: (layer, 0, i_shift)),
        ],
        out_specs=out_specs,
        out_shape=out_shape,
        compiler_params=_params(("parallel",)),
        name="norm_mod",
    )(x, g3, mod, mod)


def _final_norm_kernel(x_ref, g_ref, o_ref, *, eps):
    x = x_ref[...]
    ms = jnp.mean(x * x, axis=-1, keepdims=True)
    o_ref[...] = x * lax.rsqrt(ms + eps) * g_ref[...]


def _final_norm(x, g2, cfg):
    s, d = x.shape
    ts = _tile(s, 256)
    return pl.pallas_call(
        functools.partial(_final_norm_kernel, eps=cfg.norm_eps),
        grid=(s // ts,),
        in_specs=[pl.BlockSpec((ts, d), lambda i: (i, 0)), pl.BlockSpec((1, d), lambda i: (0, 0))],
        out_specs=pl.BlockSpec((ts, d), lambda i: (i, 0)),
        out_shape=jax.ShapeDtypeStruct((s, d), F32),
        compiler_params=_params(("parallel",)),
        name="final_norm",
    )(x, g2)


def _proj_kernel(x_ref, w_ref, cos_ref, sin_ref, o_ref, *, j_rope0, j_rope_k, j_rope1, q_scale, hd):
    acc = _dot(x_ref[...], w_ref[...])
    j = pl.program_id(1)
    is_rope = jnp.logical_and(j >= j_rope0, j < j_rope1)

    @pl.when(is_rope)
    def _():
        scale = jnp.where(j < j_rope_k, q_scale, 1.0).astype(F32)
        cos = cos_ref[...] * scale
        sin = sin_ref[...] * scale
        for c in range(acc.shape[1] // hd):
            a = acc[:, c * hd:(c + 1) * hd]
            o_ref[:, c * hd:(c + 1) * hd] = (a * cos + pltpu.roll(a, hd // 2, 1) * sin).astype(o_ref.dtype)

    @pl.when(jnp.logical_not(is_rope))
    def _():
        o_ref[...] = acc.astype(o_ref.dtype)


def _attn_proj(h, w_attn, cos_t, sin_t, g, cfg):
    m, k = h.shape
    hd = cfg.attn_head_dim
    gw = cfg.group_width
    tm = _tile(m, 1024)
    tn = _tile(gw, 1024)
    per = gw // tn
    sect = cfg.attn_width // tn
    tbl0 = g * (m // tm)
    return pl.pallas_call(
        functools.partial(_proj_kernel, j_rope0=0, j_rope_k=per, j_rope1=2 * per,
                          q_scale=float(hd) ** -0.5, hd=hd),
        grid=(m // tm, 3 * per),
        in_specs=[
            pl.BlockSpec((tm, k), lambda i, j: (i, 0)),
            pl.BlockSpec((k, tn), lambda i, j: (0, (j // per) * sect + g * per + j % per)),
            pl.BlockSpec((tm, hd), lambda i, j: (tbl0 + i, 0)),
            pl.BlockSpec((tm, hd), lambda i, j: (tbl0 + i, 0)),
        ],
        out_specs=pl.BlockSpec((tm, tn), lambda i, j: (i, j)),
        out_shape=jax.ShapeDtypeStruct((m, 3 * gw), BF16),
        compiler_params=_params(("parallel", "parallel")),
        name=f"attn_proj_{g}",
    )(h, w_attn, cos_t, sin_t)


def _mm_plain_kernel(x_ref, w_ref, o_ref):
    o_ref[...] = _dot(x_ref[...], w_ref[...]).astype(o_ref.dtype)


def _mm_relu2_kernel(x_ref, w_ref, o_ref):
    a = jnp.maximum(_dot(x_ref[...], w_ref[...]), 0.0)
    o_ref[...] = (a * a).astype(o_ref.dtype)


def _mm_resid_kernel(x_ref, w_ref, r_ref, g_ref, o_ref):
    o_ref[...] = r_ref[...] + g_ref[...] * _dot(x_ref[...], w_ref[...])


def _mm_resid_acc_kernel(x_ref, w_ref, r_ref, g_ref, o_ref, acc_ref):
    kk = pl.program_id(2)

    @pl.when(kk == 0)
    def _():
        acc_ref[...] = jnp.zeros_like(acc_ref)

    acc_ref[...] += _dot(x_ref[...], w_ref[...])

    @pl.when(kk == pl.num_programs(2) - 1)
    def _():
        o_ref[...] = r_ref[...] + g_ref[...] * acc_ref[...]


def _mm(kernel_fn, x, w, out_dtype, *, tm=1024, tn=1024, name):
    m, k = x.shape
    n = w.shape[1]
    tm, tn = _tile(m, tm), _tile(n, tn)
    return pl.pallas_call(
        kernel_fn,
        grid=(m // tm, n // tn),
        in_specs=[pl.BlockSpec((tm, k), lambda i, j: (i, 0)), pl.BlockSpec((k, tn), lambda i, j: (0, j))],
        out_specs=pl.BlockSpec((tm, tn), lambda i, j: (i, j)),
        out_shape=jax.ShapeDtypeStruct((m, n), out_dtype),
        compiler_params=_params(("parallel", "parallel")),
        name=name,
    )(x, w)


def _mm_resid(x, w, resid, mod, layer, i_gate, *, name, tk=2048):
    m, k = x.shape
    n = w.shape[1]
    tm, tn = _tile(m, 1024), _tile(n, 1024)
    gate_blocks = n // tn
    if k <= 4096:
        return pl.pallas_call(
            _mm_resid_kernel,
            grid=(m // tm, n // tn),
            in_specs=[
                pl.BlockSpec((tm, k), lambda i, j: (i, 0)),
                pl.BlockSpec((k, tn), lambda i, j: (0, j)),
                pl.BlockSpec((tm, tn), lambda i, j: (i, j)),
                pl.BlockSpec((None, 1, tn), lambda i, j: (layer, 0, i_gate * gate_blocks + j)),
            ],
            out_specs=pl.BlockSpec((tm, tn), lambda i, j: (i, j)),
            out_shape=jax.ShapeDtypeStruct((m, n), F32),
            compiler_params=_params(("parallel", "parallel")),
            name=name,
        )(x, w, resid, mod)
    tk = _tile(k, tk)
    return pl.pallas_call(
        _mm_resid_acc_kernel,
        grid=(m // tm, n // tn, k // tk),
        in_specs=[
            pl.BlockSpec((tm, tk), lambda i, j, kk: (i, kk)),
            pl.BlockSpec((tk, tn), lambda i, j, kk: (kk, j)),
            pl.BlockSpec((tm, tn), lambda i, j, kk: (i, j)),
            pl.BlockSpec((None, 1, tn), lambda i, j, kk: (layer, 0, i_gate * gate_blocks + j)),
        ],
        out_specs=pl.BlockSpec((tm, tn), lambda i, j, kk: (i, j)),
        out_shape=jax.ShapeDtypeStruct((m, n), F32),
        scratch_shapes=[pltpu.VMEM((tm, tn), F32)],
        compiler_params=_params(("parallel", "parallel", "arbitrary")),
        name=name,
    )(x, w, resid, mod)


def _merge_kernel(h_ref, og_ref, oa_ref, wga_ref, wgb_ref, bga_ref, bgb_ref, wa_ref, wb_ref, o_ref):
    h = h_ref[...]
    ga = _sigmoid(_dot(h, wga_ref[...]) + bga_ref[...])
    gb = _sigmoid(_dot(h, wgb_ref[...]) + bgb_ref[...])
    a = _dot(og_ref[...], wa_ref[...])
    b = _dot(oa_ref[...], wb_ref[...])
    o_ref[...] = (ga * a + gb * b).astype(o_ref.dtype)


def _merge(h, o_gla, o_attn, w_gate, b_gate2, w_a, w_b):
    m, d = h.shape
    tm, tn = _tile(m, 512), _tile(d, 512)
    nb = d // tn
    x_spec = lambda kdim: pl.BlockSpec((tm, kdim), lambda j, i: (i, 0))
    w_spec = lambda kdim, off: pl.BlockSpec((kdim, tn), lambda j, i: (0, j + off))
    b_spec = lambda off: pl.BlockSpec((1, tn), lambda j, i: (0, j + off))
    return pl.pallas_call(
        _merge_kernel,
        grid=(nb, m // tm),
        in_specs=[
            x_spec(d), x_spec(o_gla.shape[1]), x_spec(o_attn.shape[1]),
            w_spec(d, 0), w_spec(d, nb), b_spec(0), b_spec(nb),
            w_spec(w_a.shape[0], 0), w_spec(w_b.shape[0], 0),
        ],
        out_specs=pl.BlockSpec((tm, tn), lambda j, i: (i, j)),
        out_shape=jax.ShapeDtypeStruct((m, d), BF16),
        compiler_params=_params(("parallel", "parallel")),
        name="gated_merge",
    )(h, o_gla, o_attn, w_gate, w_gate, b_gate2, b_gate2, w_a, w_b)


def _gla_kernel(q_ref, k_ref, v_ref, gr_ref, glr_ref, wa_ref, ba_ref, gn_ref, o_ref, st_ref,
                *, chunk, tau, q_scale, eps):
    rows, dk = q_ref.shape
    sub = SUBLANES

    @pl.when(pl.program_id(1) == 0)
    def _():
        st_ref[...] = jnp.zeros_like(st_ref)

    ri = lax.broadcasted_iota(jnp.int32, (chunk, chunk), 0)
    ci = lax.broadcasted_iota(jnp.int32, (chunk, chunk), 1)
    tril = (ri >= ci).astype(F32)
    row_dk = lax.broadcasted_iota(jnp.int32, (chunk, dk), 0)
    sub_i = lax.broadcasted_iota(jnp.int32, (sub, chunk), 0)
    lane_i = lax.broadcasted_iota(jnp.int32, (sub, chunk), 1)

    def chunk_body(c, carry):
        r0 = pl.multiple_of(c * chunk, chunk)
        rs = pl.ds(r0, chunk)
        q = q_ref[rs, :].astype(F32) * q_scale
        k = k_ref[rs, :].astype(F32)
        vb = v_ref[rs, :]
        logits = _dot_f32(glr_ref[rs, :], wa_ref[...]) + ba_ref[...]
        g = (jnp.minimum(logits, 0.0) - jnp.log(1.0 + jnp.exp(-jnp.abs(logits)))) * (1.0 / tau)
        b = _dot_f32(tril, g)
        b_last = b[chunk - 1:chunk, :]

        st = st_ref[...]
        qe = (q * jnp.exp(b)).astype(BF16)
        o = _dot_nt(qe, st.astype(BF16))

        s = jnp.zeros((chunk, chunk), F32)
        g_blk = chunk // 2
        while g_blk >= sub:
            span = 2 * g_blk
            ref = jnp.concatenate(
                [jnp.broadcast_to(b[s0 + g_blk - 1:s0 + g_blk, :], (span, dk)) for s0 in range(0, chunk, span)],
                axis=0)
            upper = (row_dk & g_blk) != 0
            qg = jnp.where(upper, q * jnp.exp(jnp.minimum(b - ref, 0.0)), 0.0).astype(BF16)
            kg = jnp.where(upper, 0.0, k * jnp.exp(jnp.minimum(ref - b, 0.0))).astype(BF16)
            same = (ri ^ ci) < span
            s = s + jnp.where(same, _dot_nt(qg, kg), 0.0)
            g_blk //= 2
        diag_blocks = []
        for blk in range(chunk // sub):
            lo = blk * sub
            qi, ki, bi = q[lo:lo + sub, :], k[lo:lo + sub, :], b[lo:lo + sub, :]
            acc = jnp.zeros((sub, chunk), F32)
            for jj in range(sub):
                t = qi * ki[jj:jj + 1, :] * jnp.exp(jnp.minimum(bi - bi[jj:jj + 1, :], 0.0))
                col = jnp.sum(t, axis=1, keepdims=True)
                keep = jnp.logical_and(lane_i == lo + jj, sub_i >= jj)
                acc = jnp.where(keep, col, acc)
            diag_blocks.append(acc)
        s = s + jnp.concatenate(diag_blocks, axis=0)
        o = o + _dot(s.astype(BF16), vb)

        kd = (k * jnp.exp(b_last - b)).astype(BF16)
        st_ref[...] = st * jnp.exp(b_last) + _dot_tn(vb, kd)

        ms = jnp.mean(o * o, axis=-1, keepdims=True)
        y = o * lax.rsqrt(ms + eps) * gn_ref[...]
        gr = gr_ref[rs, :].astype(F32)
        o_ref[rs, :] = (y * (gr * _sigmoid(gr))).astype(o_ref.dtype)
        return carry

    lax.fori_loop(0, rows // chunk, chunk_body, 0)


def _gla(proj, glr, w_alpha_pad, b_alpha2, gla_norm2, cfg):
    s = proj.shape[0]
    nh, dk, dv = cfg.gla_heads, cfg.dk_head, cfg.dv_head
    rows = _tile(s, 256)
    assert rows % cfg.gla_chunk == 0 and cfg.gla_chunk % (2 * SUBLANES) == 0
    assert cfg.gla_chunk & (cfg.gla_chunk - 1) == 0, "chunk must be a power of two"
    k_off = cfg.gla_dk // dk
    v_off = (2 * cfg.gla_dk) // dv
    r_off = (2 * cfg.gla_dk + cfg.gla_dv) // dv
    assert (2 * cfg.gla_dk) % dv == 0
    return pl.pallas_call(
        functools.partial(_gla_kernel, chunk=cfg.gla_chunk, tau=cfg.gate_tau,
                          q_scale=float(dk) ** -0.5, eps=cfg.norm_eps),
        grid=(nh, s // rows),
        in_specs=[
            pl.BlockSpec((rows, dk), lambda h, t: (t, h)),
            pl.BlockSpec((rows, dk), lambda h, t: (t, k_off + h)),
            pl.BlockSpec((rows, dv), lambda h, t: (t, v_off + h)),
            pl.BlockSpec((rows, dv), lambda h, t: (t, r_off + h)),
            pl.BlockSpec((rows, LANES), lambda h, t: (t, 0)),
            pl.BlockSpec((LANES, dk), lambda h, t: (0, h)),
            pl.BlockSpec((1, dk), lambda h, t: (0, h)),
            pl.BlockSpec((1, dv), lambda h, t: (0, 0)),
        ],
        out_specs=pl.BlockSpec((rows, dv), lambda h, t: (t, h)),
        out_shape=jax.ShapeDtypeStruct((s, cfg.gla_dv), BF16),
        scratch_shapes=[pltpu.VMEM((dv, dk), F32)],
        compiler_params=_params(("parallel", "arbitrary")),
        name="gla",
    )(proj, proj, proj, proj, glr, w_alpha_pad, b_alpha2, gla_norm2)


def _attn_kernel(q_ref, kp_ref, kc_ref, vp_ref, vc_ref, o_ref, lse_ref, s_scr, p_scr, *, heads, hd):
    blk = q_ref.shape[0]
    n = pl.program_id(1)
    qi = lax.broadcasted_iota(jnp.int32, (blk, 2 * blk), 0)
    kj = lax.broadcasted_iota(jnp.int32, (blk, 2 * blk), 1)
    dist = blk + qi - kj
    band = jnp.logical_and(dist >= 0, dist <= blk)
    mask = jnp.logical_and(band, jnp.logical_or(kj >= blk, n > 0))
    lane = lax.broadcasted_iota(jnp.int32, (blk, LANES), 1)
    for hh in range(heads):
        sl = slice(hh * hd, (hh + 1) * hd)
        q = q_ref[:, sl]
        s = jnp.concatenate([_dot_nt(q, kp_ref[:, sl]), _dot_nt(q, kc_ref[:, sl])], axis=1)
        s_scr[hh] = jnp.where(mask, s, -jnp.inf)
    lse_tile = jnp.zeros((blk, LANES), F32)
    for hh in range(heads):
        s = s_scr[hh]
        m = jnp.max(s, axis=1, keepdims=True)
        p = jnp.exp(s - m)
        den = jnp.sum(p, axis=1, keepdims=True)
        p_scr[hh] = (p * (1.0 / den)).astype(BF16)
        lse_tile = jnp.where(lane == hh, m + jnp.log(den), lse_tile)
    lse_ref[...] = lse_tile
    for hh in range(heads):
        sl = slice(hh * hd, (hh + 1) * hd)
        p = p_scr[hh]
        o = _dot(p[:, :blk], vp_ref[:, sl]) + _dot(p[:, blk:], vc_ref[:, sl])
        o_ref[:, sl] = o.astype(o_ref.dtype)


def _attn_group(qkv, window, dilation, cfg, name):
    dil, u, w3 = qkv.shape
    gw = cfg.group_width
    heads = cfg.attn_heads_per_group
    blk = window // dilation
    assert dil == dilation and w3 == 3 * gw and u % blk == 0 and heads <= LANES
    nb = u // blk

    def cur(part):
        return pl.BlockSpec((None, blk, gw), lambda r, n: (r, n, part))

    def prev(part):
        return pl.BlockSpec((None, blk, gw), lambda r, n: (r, jnp.maximum(n - 1, 0), part))

    return pl.pallas_call(
        functools.partial(_attn_kernel, heads=heads, hd=cfg.attn_head_dim),
        grid=(dilation, nb),
        in_specs=[cur(0), prev(1), cur(1), prev(2), cur(2)],
        out_specs=[pl.BlockSpec((None, blk, gw), lambda r, n: (r, n, 0)),
                   pl.BlockSpec((None, blk, LANES), lambda r, n: (r, n, 0))],
        out_shape=[jax.ShapeDtypeStruct((dilation, u, gw), BF16),
                   jax.ShapeDtypeStruct((dilation, u, LANES), F32)],
        scratch_shapes=[pltpu.VMEM((heads, blk, 2 * blk), F32), pltpu.VMEM((heads, blk, 2 * blk), BF16)],
        compiler_params=_params(("parallel", "arbitrary")),
        name=name,
    )(qkv, qkv, qkv, qkv, qkv)


def _combine_kernel(*refs, dilations, heads, hd):
    ng = len(dilations)
    o_refs, l_refs, out_ref, scr = refs[:ng], refs[ng:2 * ng], refs[2 * ng], refs[2 * ng + 1:]
    ts = out_ref.shape[0]
    o_tok, l_tok = [], []
    for gi, dil in enumerate(dilations):
        o_s, l_s = scr[2 * gi], scr[2 * gi + 1]
        o_g = o_refs[gi][...].reshape(ts, o_s.shape[1])
        l_g = l_refs[gi][...].reshape(ts, l_s.shape[1])
        if dil > 1:
            o_s[...] = _dot(_perm_matrix(ts, dil, o_g.dtype), o_g)
            l_s[...] = _dot_f32(_perm_matrix(ts, dil, F32), l_g)
        else:
            o_s[...] = o_g.astype(F32)
            l_s[...] = l_g
        o_tok.append(o_s)
        l_tok.append(l_s)
    ls = [r[...] for r in l_tok]
    m = functools.reduce(jnp.maximum, ls)
    es = [jnp.exp(l - m) for l in ls]
    inv = 1.0 / functools.reduce(lambda a, b: a + b, es)
    ws = [e * inv for e in es]
    for hh in range(heads):
        sl = slice(hh * hd, (hh + 1) * hd)
        acc = ws[0][:, hh:hh + 1] * o_tok[0][:, sl]
        for gi in range(1, ng):
            acc = acc + ws[gi][:, hh:hh + 1] * o_tok[gi][:, sl]
        out_ref[:, sl] = acc.astype(out_ref.dtype)


def _combine(outs, lses, dilations, cfg):
    gw = cfg.group_width
    s = outs[0].shape[0] * outs[0].shape[1]
    ts = _tile(s, 512)
    in_specs, scratch = [], []
    for dil in dilations:
        assert ts % (dil * 2 * SUBLANES) == 0
        in_specs.append(pl.BlockSpec((dil, ts // dil, gw), lambda i: (0, i, 0)))
    for dil in dilations:
        in_specs.append(pl.BlockSpec((dil, ts // dil, LANES), lambda i: (0, i, 0)))
        scratch += [pltpu.VMEM((ts, gw), F32), pltpu.VMEM((ts, LANES), F32)]
    return pl.pallas_call(
        functools.partial(_combine_kernel, dilations=tuple(dilations), heads=cfg.attn_heads_per_group,
                          hd=cfg.attn_head_dim),
        grid=(s // ts,),
        in_specs=in_specs,
        out_specs=pl.BlockSpec((ts, gw), lambda i: (i, 0)),
        out_shape=jax.ShapeDtypeStruct((s, gw), BF16),
        scratch_shapes=scratch,
        compiler_params=_params(("parallel",)),
        name="attn_combine",
    )(*outs, *lses)


def _forward(cfg, x, c, positions, ada_w, ada_b, norm_mix, norm_mlp, w_in, w_alpha_up, b_alpha,
             gla_norm, w_branch_gla, w_branch_attn, w_gate, b_gate, w_out, w_mlp_in, w_mlp_out,
             norm_final):
    bsz, s, d = x.shape
    assert bsz == 1 and d == cfg.d_model
    depth = cfg.depth
    x2 = x.reshape(s, d)
    mod = _ada_mod(c.reshape(d, 1), ada_w, ada_b.reshape(depth, 1, -1))
    dils = [dil for _, dil in cfg.dilated_groups]
    perm_dils = sorted({dil for dil in dils if dil > 1})
    pos = positions.reshape(s)
    pos_all = jnp.concatenate([pos.reshape(s // dil, dil).T.reshape(s) for dil in dils]).reshape(-1, 1)
    cos_t, sin_t = _rope_tables(pos_all, cfg)
    g_mix = norm_mix.reshape(depth, 1, d)
    g_mlp = norm_mlp.reshape(depth, 1, d)
    gw, rank = cfg.gla_width, cfg.gate_rank

    for l in range(depth):
        w_gla = w_in[l, :, :gw].astype(BF16)
        w_attn = w_in[l, :, gw + rank:].astype(BF16)
        w_glr = jnp.pad(w_in[l, :, gw:gw + rank], ((0, 0), (0, LANES - rank))).astype(BF16)
        w_alpha_pad = jnp.pad(w_alpha_up[l], ((0, LANES - rank), (0, 0)))

        hs = _norm_mod(x2, g_mix, mod, l, 0, 1, cfg, dilations=perm_dils)
        h = hs[0]
        h_by_dil = {1: h, **{dil: hp.reshape(s, d) for dil, hp in zip(perm_dils, hs[1:])}}
        proj = _mm(_mm_plain_kernel, h, w_gla, BF16, name="gla_proj")
        glr = _mm(_mm_plain_kernel, h, w_glr, F32, tn=LANES, name="gate_lowrank")
        o_gla = _gla(proj, glr, w_alpha_pad, b_alpha[l].reshape(1, -1), gla_norm[l].reshape(1, -1), cfg)
        outs, lses = [], []
        for g, (window, dil) in enumerate(cfg.dilated_groups):
            qkv = _attn_proj(h_by_dil[dil], w_attn, cos_t, sin_t, g, cfg)
            o_g, lse_g = _attn_group(qkv.reshape(dil, s // dil, -1), window, dil, cfg, f"dilated_attn_{g}")
            outs.append(o_g)
            lses.append(lse_g)
        o_attn = _combine(outs, lses, dils, cfg)
        merged = _merge(h, o_gla, o_attn, w_gate[l].astype(BF16), b_gate[l].reshape(1, -1),
                        w_branch_gla[l].astype(BF16), w_branch_attn[l].astype(BF16))
        x2 = _mm_resid(merged, w_out[l].astype(BF16), x2, mod, l, 2, name="out_proj")

        h = _norm_mod(x2, g_mlp, mod, l, 3, 4, cfg)[0]
        a = _mm(_mm_relu2_kernel, h, w_mlp_in[l].astype(BF16), BF16, name="mlp_in")
        x2 = _mm_resid(a, w_mlp_out[l].astype(BF16), x2, mod, l, 5, name="mlp_out")

    return _final_norm(x2, norm_final.reshape(1, d), cfg).reshape(bsz, s, d)


def kernel(x, c, positions, ada_w, ada_b, norm_mix, norm_mlp, w_in, w_alpha_up, b_alpha, gla_norm,
           w_branch_gla, w_branch_attn, w_gate, b_gate, w_out, w_mlp_in, w_mlp_out, norm_final):
    return _forward(Config(), x, c, positions, ada_w, ada_b, norm_mix, norm_mlp, w_in, w_alpha_up,
                    b_alpha, gla_norm, w_branch_gla, w_branch_attn, w_gate, b_gate, w_out, w_mlp_in,
                    w_mlp_out, norm_final)
```

```python
import functools
from typing import NamedTuple

import jax
import jax.numpy as jnp
from jax import lax
from jax.experimental import pallas as pl
from jax.experimental.pallas import tpu as pltpu

F32 = jnp.float32
BF16 = jnp.bfloat16
LANES = 128
SUBLANES = 8
VMEM_LIMIT = 56 * 1024 * 1024


class Config(NamedTuple):
    d_model: int = 4096
    depth: int = 2
    gla_heads: int = 4
    gate_rank: int = 16
    gate_tau: float = 16.0
    gla_chunk: int = 64
    attn_head_dim: int = 128
    attn_heads_per_group: int = 16
    dilated_groups: tuple = ((128, 1), (512, 4), (2048, 16))
    rope_theta: float = 10000.0
    ff_mult: int = 4
    n_mod: int = 6
    norm_eps: float = 1e-6

    @property
    def gla_dk(self):
        return self.d_model // 2

    @property
    def gla_dv(self):
        return self.d_model

    @property
    def dk_head(self):
        return self.gla_dk // self.gla_heads

    @property
    def dv_head(self):
        return self.gla_dv // self.gla_heads

    @property
    def group_width(self):
        return self.attn_heads_per_group * self.attn_head_dim

    @property
    def attn_width(self):
        return len(self.dilated_groups) * self.group_width

    @property
    def gla_width(self):
        return 2 * self.gla_dk + 2 * self.gla_dv


def _params(sem):
    return pltpu.CompilerParams(dimension_semantics=sem, vmem_limit_bytes=VMEM_LIMIT)


def _sigmoid(x):
    return 1.0 / (1.0 + jnp.exp(-x))


def _dot(a, b):
    return jnp.dot(a, b, preferred_element_type=F32)


def _dot_nt(a, b):
    return lax.dot_general(a, b, (((1,), (1,)), ((), ())), preferred_element_type=F32)


def _dot_tn(a, b):
    return lax.dot_general(a, b, (((0,), (0,)), ((), ())), preferred_element_type=F32)


def _dot_f32(a, b):
    return jnp.dot(a, b, preferred_element_type=F32, precision=lax.Precision.HIGHEST)


def _tile(n, pref):
    t = min(n, pref)
    assert n % t == 0, (n, pref)
    return t


def _perm_matrix(size, inner, dtype):
    outer = size // inner
    assert inner * outer == size and inner & (inner - 1) == 0 and outer & (outer - 1) == 0
    i = lax.broadcasted_iota(jnp.int32, (size, size), 0)
    j = lax.broadcasted_iota(jnp.int32, (size, size), 1)
    src = (i & (inner - 1)) * outer + (i >> (inner.bit_length() - 1))
    return (j == src).astype(dtype)


def _ada_kernel(c_ref, w_ref, b_ref, o_ref, *, rows):
    d, tn = w_ref.shape

    def body(i, acc):
        r = pl.multiple_of(i * rows, rows)
        cc = c_ref[pl.ds(r, rows), :]
        cond = cc * _sigmoid(cc)
        p = w_ref[pl.ds(r, rows), :] * cond
        return acc + jnp.sum(p.reshape(rows // SUBLANES, SUBLANES, tn), axis=0)

    acc = lax.fori_loop(0, d // rows, body, jnp.zeros((SUBLANES, tn), F32))
    o_ref[...] = jnp.sum(acc, axis=0, keepdims=True) + b_ref[...]


def _ada_mod(c_col, ada_w, ada_b3):
    depth, d, n = ada_w.shape
    tn = _tile(n, 512)
    rows = _tile(d, 256)
    return pl.pallas_call(
        functools.partial(_ada_kernel, rows=rows),
        grid=(depth, n // tn),
        in_specs=[
            pl.BlockSpec((d, 1), lambda l, j: (0, 0)),
            pl.BlockSpec((None, d, tn), lambda l, j: (l, 0, j)),
            pl.BlockSpec((None, 1, tn), lambda l, j: (l, 0, j)),
        ],
        out_specs=pl.BlockSpec((None, 1, tn), lambda l, j: (l, 0, j)),
        out_shape=jax.ShapeDtypeStruct((depth, 1, n), F32),
        compiler_params=_params(("parallel", "parallel")),
        name="ada_mod",
    )(c_col, ada_w, ada_b3)


def _rope_kernel(p_ref, inv_ref, sgn_ref, c_ref, s_ref):
    ang = p_ref[...].astype(F32) * inv_ref[...]
    c_ref[...] = jnp.cos(ang)
    s_ref[...] = jnp.sin(ang) * sgn_ref[...]


def _rope_tables(pos_col, cfg):
    s = pos_col.shape[0]
    hd = cfg.attn_head_dim
    half = hd // 2
    inv = cfg.rope_theta ** (-jnp.arange(half, dtype=F32) / half)
    inv2 = jnp.concatenate([inv, inv]).reshape(1, hd)
    sgn = jnp.concatenate([-jnp.ones((half,), F32), jnp.ones((half,), F32)]).reshape(1, hd)
    ts = _tile(s, 1024)
    spec_t = pl.BlockSpec((ts, hd), lambda i: (i, 0))
    spec_c = pl.BlockSpec((1, hd), lambda i: (0, 0))
    return pl.pallas_call(
        _rope_kernel,
        grid=(s // ts,),
        in_specs=[pl.BlockSpec((ts, 1), lambda i: (i, 0)), spec_c, spec_c],
        out_specs=[spec_t, spec_t],
        out_shape=[jax.ShapeDtypeStruct((s, hd), F32)] * 2,
        compiler_params=_params(("parallel",)),
        name="rope_tables",
    )(pos_col, inv2, sgn)


def _norm_mod_kernel(x_ref, g_ref, sc_ref, sh_ref, o_ref, *rest, eps, dilations):
    x = x_ref[...]
    ms = jnp.mean(x * x, axis=-1, keepdims=True)
    y = x * lax.rsqrt(ms + eps) * g_ref[...]
    hb = (y * (1.0 + sc_ref[...]) + sh_ref[...]).astype(o_ref.dtype)
    o_ref[...] = hb
    ts = hb.shape[0]
    for p_ref, dil in zip(rest, dilations):
        n = ts // dil
        hp = _dot(_perm_matrix(ts, n, hb.dtype), hb).astype(p_ref.dtype)
        for r in range(dil):
            p_ref[r] = hp[r * n:(r + 1) * n, :]


def _norm_mod(x, g3, mod, layer, i_shift, i_scale, cfg, dilations=()):
    s, d = x.shape
    ts = _tile(s, 256)
    out_specs = [pl.BlockSpec((ts, d), lambda i: (i, 0))]
    out_shape = [jax.ShapeDtypeStruct((s, d), BF16)]
    for dil in dilations:
        assert ts % (dil * 2 * SUBLANES) == 0
        out_specs.append(pl.BlockSpec((dil, ts // dil, d), lambda i: (0, i, 0)))
        out_shape.append(jax.ShapeDtypeStruct((dil, s // dil, d), BF16))
    return pl.pallas_call(
        functools.partial(_norm_mod_kernel, eps=cfg.norm_eps, dilations=tuple(dilations)),
        grid=(s // ts,),
        in_specs=[
            pl.BlockSpec((ts, d), lambda i: (i, 0)),
            pl.BlockSpec((None, 1, d), lambda i: (layer, 0, 0)),
            pl.BlockSpec((None, 1, d), lambda i: (layer, 0, i_scale)),
            pl.BlockSpec((None, 1, d), lambda i: (layer, 0, i_shift)),
        ],
        out_specs=out_specs,
        out_shape=out_shape,
        compiler_params=_params(("parallel",)),
        name="norm_mod",
    )(x, g3, mod, mod)


def _final_norm_kernel(x_ref, g_ref, o_ref, *, eps):
    x = x_ref[...]
    ms = jnp.mean(x * x, axis=-1, keepdims=True)
    o_ref[...] = x * lax.rsqrt(ms + eps) * g_ref[...]


def _final_norm(x, g2, cfg):
    s, d = x.shape
    ts = _tile(s, 256)
    return pl.pallas_call(
        functools.partial(_final_norm_kernel, eps=cfg.norm_eps),
        grid=(s // ts,),
        in_specs=[pl.BlockSpec((ts, d), lambda i: (i, 0)), pl.BlockSpec((1, d), lambda i: (0, 0))],
        out_specs=pl.BlockSpec((ts, d), lambda i: (i, 0)),
        out_shape=jax.ShapeDtypeStruct((s, d), F32),
        compiler_params=_params(("parallel",)),
        name="final_norm",
    )(x, g2)


def _proj_kernel(x_ref, w_ref, cos_ref, sin_ref, o_ref, *, j_rope0, j_rope_k, j_rope1, q_scale, hd):
    acc = _dot(x_ref[...], w_ref[...])
    j = pl.program_id(1)
    is_rope = jnp.logical_and(j >= j_rope0, j < j_rope1)
    scale = jnp.where(j < j_rope_k, q_scale, 1.0).astype(F32)
    cos = jnp.where(is_rope, cos_ref[...] * scale, 1.0)
    sin = jnp.where(is_rope, sin_ref[...] * scale, 0.0)
    for c in range(acc.shape[1] // hd):
        a = acc[:, c * hd:(c + 1) * hd]
        o_ref[:, c * hd:(c + 1) * hd] = (a * cos + pltpu.roll(a, hd // 2, 1) * sin).astype(o_ref.dtype)


def _attn_proj(h, w_attn, cos_t, sin_t, g, cfg):
    m, k = h.shape
    hd = cfg.attn_head_dim
    gw = cfg.group_width
    tm = _tile(m, 1024)
    tn = _tile(gw, 1024)
    per = gw // tn
    sect = cfg.attn_width // tn
    tbl0 = g * (m // tm)
    return pl.pallas_call(
        functools.partial(_proj_kernel, j_rope0=0, j_rope_k=per, j_rope1=2 * per,
                          q_scale=float(hd) ** -0.5, hd=hd),
        grid=(m // tm, 3 * per),
        in_specs=[
            pl.BlockSpec((tm, k), lambda i, j: (i, 0)),
            pl.BlockSpec((k, tn), lambda i, j: (0, (j // per) * sect + g * per + j % per)),
            pl.BlockSpec((tm, hd), lambda i, j: (tbl0 + i, 0)),
            pl.BlockSpec((tm, hd), lambda i, j: (tbl0 + i, 0)),
        ],
        out_specs=pl.BlockSpec((tm, tn), lambda i, j: (i, j)),
        out_shape=jax.ShapeDtypeStruct((m, 3 * gw), BF16),
        compiler_params=_params(("parallel", "parallel")),
        name=f"attn_proj_{g}",
    )(h, w_attn, cos_t, sin_t)


def _mm_plain_kernel(x_ref, w_ref, o_ref):
    o_ref[...] = _dot(x_ref[...], w_ref[...].astype(BF16)).astype(o_ref.dtype)


def _mm_relu2_kernel(x_ref, w_ref, o_ref):
    a = jnp.maximum(_dot(x_ref[...], w_ref[...].astype(BF16)), 0.0)
    o_ref[...] = (a * a).astype(o_ref.dtype)


def _mm_resid_kernel(x_ref, w_ref, r_ref, g_ref, o_ref):
    o_ref[...] = r_ref[...] + g_ref[...] * _dot(x_ref[...], w_ref[...].astype(BF16))


def _mm(kernel_fn, x, w3, layer, n, out_dtype, *, tm=1024, tn=512, name):
    m, k = x.shape
    assert w3.shape[1] == k and n <= w3.shape[2]
    tm, tn = _tile(m, tm), _tile(n, tn)
    return pl.pallas_call(
        kernel_fn,
        grid=(m // tm, n // tn),
        in_specs=[pl.BlockSpec((tm, k), lambda i, j: (i, 0)),
                  pl.BlockSpec((None, k, tn), lambda i, j: (layer, 0, j))],
        out_specs=pl.BlockSpec((tm, tn), lambda i, j: (i, j)),
        out_shape=jax.ShapeDtypeStruct((m, n), out_dtype),
        compiler_params=_params(("parallel", "parallel")),
        name=name,
    )(x, w3)


def _mm_resid(x, w3, resid, mod, layer, i_gate, *, tm, tn, name):
    m, k = x.shape
    n = w3.shape[2]
    assert w3.shape[1] == k
    tm, tn = _tile(m, tm), _tile(n, tn)
    gate_blocks = n // tn
    return pl.pallas_call(
        _mm_resid_kernel,
        grid=(m // tm, n // tn),
        in_specs=[
            pl.BlockSpec((tm, k), lambda i, j: (i, 0)),
            pl.BlockSpec((None, k, tn), lambda i, j: (layer, 0, j)),
            pl.BlockSpec((tm, tn), lambda i, j: (i, j)),
            pl.BlockSpec((None, 1, tn), lambda i, j: (layer, 0, i_gate * gate_blocks + j)),
        ],
        out_specs=pl.BlockSpec((tm, tn), lambda i, j: (i, j)),
        out_shape=jax.ShapeDtypeStruct((m, n), F32),
        compiler_params=_params(("parallel", "parallel")),
        name=name,
    )(x, w3, resid, mod)


def _merge_kernel(h_ref, og_ref, oa_ref, wga_ref, wgb_ref, bga_ref, bgb_ref, wa_ref, wb_ref, o_ref):
    h = h_ref[...]
    ga = _sigmoid(_dot(h, wga_ref[...]) + bga_ref[...])
    gb = _sigmoid(_dot(h, wgb_ref[...]) + bgb_ref[...])
    a = _dot(og_ref[...], wa_ref[...])
    b = _dot(oa_ref[...], wb_ref[...])
    o_ref[...] = (ga * a + gb * b).astype(o_ref.dtype)


def _merge(h, o_gla, o_attn, w_gate, b_gate2, w_a, w_b):
    m, d = h.shape
    tm, tn = _tile(m, 512), _tile(d, 512)
    nb = d // tn
    x_spec = lambda kdim: pl.BlockSpec((tm, kdim), lambda j, i: (i, 0))
    w_spec = lambda kdim, off: pl.BlockSpec((kdim, tn), lambda j, i: (0, j + off))
    b_spec = lambda off: pl.BlockSpec((1, tn), lambda j, i: (0, j + off))
    return pl.pallas_call(
        _merge_kernel,
        grid=(nb, m // tm),
        in_specs=[
            x_spec(d), x_spec(o_gla.shape[1]), x_spec(o_attn.shape[1]),
            w_spec(d, 0), w_spec(d, nb), b_spec(0), b_spec(nb),
            w_spec(w_a.shape[0], 0), w_spec(w_b.shape[0], 0),
        ],
        out_specs=pl.BlockSpec((tm, tn), lambda j, i: (i, j)),
        out_shape=jax.ShapeDtypeStruct((m, d), BF16),
        compiler_params=_params(("parallel", "parallel")),
        name="gated_merge",
    )(h, o_gla, o_attn, w_gate, w_gate, b_gate2, b_gate2, w_a, w_b)


def _split_bf16(x, pieces):
    out = []
    for _ in range(pieces - 1):
        p = x.astype(BF16)
        out.append(p)
        x = x - p.astype(F32)
    out.append(x.astype(BF16))
    return out


def _gate_kernel(h_ref, wl_ref, wc_ref, ba_ref, b_ref, *, chunk, tau, rank):
    tm = h_ref.shape[0]
    x = _dot(h_ref[...], wl_ref[...])
    hi = x.astype(BF16).astype(F32)
    lane = lax.broadcasted_iota(jnp.int32, x.shape, 1)
    a = jnp.where(lane < 2 * rank, hi, x - hi).astype(BF16)
    logits = _dot(a, wc_ref[...]) + ba_ref[...]
    g = (jnp.minimum(logits, 0.0) - jnp.log(1.0 + jnp.exp(-jnp.abs(logits)))) * (1.0 / tau)
    ri = lax.broadcasted_iota(jnp.int32, (tm, tm), 0)
    ci = lax.broadcasted_iota(jnp.int32, (tm, tm), 1)
    tril = jnp.where(jnp.logical_and(ri >= ci, (ri ^ ci) < chunk), 1.0, 0.0).astype(BF16)
    p1, p2, p3 = _split_bf16(g, 3)
    b_ref[...] = _dot(tril, p1) + _dot(tril, p2) + _dot(tril, p3)


def _gla_gate(h, w_glr3, w_alpha_cat, b_alpha2, cfg):
    s, d = h.shape
    n = w_alpha_cat.shape[1]
    tm = _tile(s, 256)
    assert tm % cfg.gla_chunk == 0 and 3 * cfg.gate_rank <= LANES
    return pl.pallas_call(
        functools.partial(_gate_kernel, chunk=cfg.gla_chunk, tau=cfg.gate_tau, rank=cfg.gate_rank),
        grid=(s // tm,),
        in_specs=[
            pl.BlockSpec((tm, d), lambda i: (i, 0)),
            pl.BlockSpec((d, LANES), lambda i: (0, 0)),
            pl.BlockSpec((LANES, n), lambda i: (0, 0)),
            pl.BlockSpec((1, n), lambda i: (0, 0)),
        ],
        out_specs=pl.BlockSpec((tm, n), lambda i: (i, 0)),
        out_shape=jax.ShapeDtypeStruct((s, n), F32),
        compiler_params=_params(("parallel",)),
        name="gla_gate",
    )(h, w_glr3, w_alpha_cat, b_alpha2)


def _gla_kernel(q_ref, k_ref, v_ref, gr_ref, b_ref, gn_ref, o_ref, st_ref, *, chunk, heads, q_scale, eps):
    rows = q_ref.shape[0]
    dk = q_ref.shape[1] // heads
    dv = v_ref.shape[1] // heads
    sub = SUBLANES

    @pl.when(pl.program_id(1) == 0)
    def _():
        st_ref[...] = jnp.zeros_like(st_ref)

    ri = lax.broadcasted_iota(jnp.int32, (chunk, chunk), 0)
    ci = lax.broadcasted_iota(jnp.int32, (chunk, chunk), 1)
    sub_i = lax.broadcasted_iota(jnp.int32, (sub, chunk), 0)
    lane_i = lax.broadcasted_iota(jnp.int32, (sub, chunk), 1)

    def head_chunk(rs, hh):
        ksl = slice(hh * dk, (hh + 1) * dk)
        vsl = slice(hh * dv, (hh + 1) * dv)
        q = q_ref[rs, ksl].astype(F32) * q_scale
        k = k_ref[rs, ksl].astype(F32)
        vb = v_ref[rs, vsl]
        b = b_ref[rs, ksl]
        b_last = b[chunk - 1:chunk, :]

        st = st_ref[hh]
        qe = (q * jnp.exp(b)).astype(BF16)
        o = _dot_nt(qe, st.astype(BF16))

        s = jnp.zeros((chunk, chunk), F32)
        g_blk = chunk // 2
        while g_blk >= sub:
            span = 2 * g_blk
            zero = jnp.zeros((g_blk, dk), F32)
            q_parts, k_parts = [], []
            for s0 in range(0, chunk, span):
                ref = jnp.broadcast_to(b[s0 + g_blk - 1:s0 + g_blk, :], (g_blk, dk))
                lo, up = slice(s0, s0 + g_blk), slice(s0 + g_blk, s0 + span)
                q_parts += [zero, q[up] * jnp.exp(b[up] - ref)]
                k_parts += [k[lo] * jnp.exp(ref - b[lo]), zero]
            qg = jnp.concatenate(q_parts, axis=0).astype(BF16)
            kg = jnp.concatenate(k_parts, axis=0).astype(BF16)
            same = (ri ^ ci) < span
            s = s + jnp.where(same, _dot_nt(qg, kg), 0.0)
            g_blk //= 2
        diag_blocks = []
        for blk in range(chunk // sub):
            lo = blk * sub
            qi, ki, bi = q[lo:lo + sub, :], k[lo:lo + sub, :], b[lo:lo + sub, :]
            acc = jnp.zeros((sub, chunk), F32)
            for jj in range(sub):
                t = qi * ki[jj:jj + 1, :] * jnp.exp(jnp.minimum(bi - bi[jj:jj + 1, :], 0.0))
                col = jnp.sum(t, axis=1, keepdims=True)
                keep = jnp.logical_and(lane_i == lo + jj, sub_i >= jj)
                acc = jnp.where(keep, col, acc)
            diag_blocks.append(acc)
        s = s + jnp.concatenate(diag_blocks, axis=0)
        o = o + _dot(s.astype(BF16), vb)

        kd = (k * jnp.exp(b_last - b)).astype(BF16)
        st_ref[hh] = st * jnp.exp(b_last) + _dot_tn(vb, kd)

        ms = jnp.mean(o * o, axis=-1, keepdims=True)
        y = o * lax.rsqrt(ms + eps) * gn_ref[...]
        gr = gr_ref[rs, vsl].astype(F32)
        o_ref[rs, vsl] = (y * (gr * _sigmoid(gr))).astype(o_ref.dtype)

    def chunk_body(c, carry):
        rs = pl.ds(pl.multiple_of(c * chunk, chunk), chunk)
        for hh in range(heads):
            head_chunk(rs, hh)
        return carry

    lax.fori_loop(0, rows // chunk, chunk_body, 0)


def _gla(proj, b_cum, gla_norm2, cfg, heads_per_step=2):
    s = proj.shape[0]
    nh, dk, dv = cfg.gla_heads, cfg.dk_head, cfg.dv_head
    hp = heads_per_step
    rows = _tile(s, 256)
    assert nh % hp == 0 and rows % cfg.gla_chunk == 0 and cfg.gla_chunk % (2 * SUBLANES) == 0
    assert cfg.gla_chunk & (cfg.gla_chunk - 1) == 0, "chunk must be a power of two"
    assert cfg.gla_dk % (hp * dk) == 0 and (2 * cfg.gla_dk) % (hp * dv) == 0
    k_off = cfg.gla_dk // (hp * dk)
    v_off = (2 * cfg.gla_dk) // (hp * dv)
    r_off = (2 * cfg.gla_dk + cfg.gla_dv) // (hp * dv)
    return pl.pallas_call(
        functools.partial(_gla_kernel, chunk=cfg.gla_chunk, heads=hp,
                          q_scale=float(dk) ** -0.5, eps=cfg.norm_eps),
        grid=(nh // hp, s // rows),
        in_specs=[
            pl.BlockSpec((rows, hp * dk), lambda h, t: (t, h)),
            pl.BlockSpec((rows, hp * dk), lambda h, t: (t, k_off + h)),
            pl.BlockSpec((rows, hp * dv), lambda h, t: (t, v_off + h)),
            pl.BlockSpec((rows, hp * dv), lambda h, t: (t, r_off + h)),
            pl.BlockSpec((rows, hp * dk), lambda h, t: (t, h)),
            pl.BlockSpec((1, dv), lambda h, t: (0, 0)),
        ],
        out_specs=pl.BlockSpec((rows, hp * dv), lambda h, t: (t, h)),
        out_shape=jax.ShapeDtypeStruct((s, cfg.gla_dv), BF16),
        scratch_shapes=[pltpu.VMEM((hp, dv, dk), F32)],
        compiler_params=_params(("parallel", "arbitrary")),
        name="gla",
    )(proj, proj, proj, proj, b_cum, gla_norm2)


def _attn_kernel(q_ref, kp_ref, kc_ref, vp_ref, vc_ref, o_ref, lse_ref, s_scr, p_scr, *, heads, hd):
    blk = q_ref.shape[0]
    n = pl.program_id(1)
    qi = lax.broadcasted_iota(jnp.int32, (blk, 2 * blk), 0)
    kj = lax.broadcasted_iota(jnp.int32, (blk, 2 * blk), 1)
    dist = blk + qi - kj
    band = jnp.logical_and(dist >= 0, dist <= blk)
    mask = jnp.logical_and(band, jnp.logical_or(kj >= blk, n > 0))
    lane = lax.broadcasted_iota(jnp.int32, (blk, LANES), 1)
    for hh in range(heads):
        sl = slice(hh * hd, (hh + 1) * hd)
        q = q_ref[:, sl]
        s = jnp.concatenate([_dot_nt(q, kp_ref[:, sl]), _dot_nt(q, kc_ref[:, sl])], axis=1)
        s_scr[hh] = jnp.where(mask, s, -jnp.inf)
    lse_tile = jnp.zeros((blk, LANES), F32)
    for hh in range(heads):
        s = s_scr[hh]
        m = jnp.max(s, axis=1, keepdims=True)
        p = jnp.exp(s - m)
        den = jnp.sum(p, axis=1, keepdims=True)
        p_scr[hh] = (p * (1.0 / den)).astype(BF16)
        lse_tile = jnp.where(lane == hh, m + jnp.log(den), lse_tile)
    lse_ref[...] = lse_tile
    for hh in range(heads):
        sl = slice(hh * hd, (hh + 1) * hd)
        p = p_scr[hh]
        o = _dot(p[:, :blk], vp_ref[:, sl]) + _dot(p[:, blk:], vc_ref[:, sl])
        o_ref[:, sl] = o.astype(o_ref.dtype)


def _attn_group(qkv, window, dilation, cfg, name):
    dil, u, w3 = qkv.shape
    gw = cfg.group_width
    heads = cfg.attn_heads_per_group
    blk = window // dilation
    assert dil == dilation and w3 == 3 * gw and u % blk == 0 and heads <= LANES
    nb = u // blk

    def cur(part):
        return pl.BlockSpec((None, blk, gw), lambda r, n: (r, n, part))

    def prev(part):
        return pl.BlockSpec((None, blk, gw), lambda r, n: (r, jnp.maximum(n - 1, 0), part))

    return pl.pallas_call(
        functools.partial(_attn_kernel, heads=heads, hd=cfg.attn_head_dim),
        grid=(dilation, nb),
        in_specs=[cur(0), prev(1), cur(1), prev(2), cur(2)],
        out_specs=[pl.BlockSpec((None, blk, gw), lambda r, n: (r, n, 0)),
                   pl.BlockSpec((None, blk, LANES), lambda r, n: (r, n, 0))],
        out_shape=[jax.ShapeDtypeStruct((dilation, u, gw), BF16),
                   jax.ShapeDtypeStruct((dilation, u, LANES), F32)],
        scratch_shapes=[pltpu.VMEM((heads, blk, 2 * blk), F32), pltpu.VMEM((heads, blk, 2 * blk), BF16)],
        compiler_params=_params(("parallel", "arbitrary")),
        name=name,
    )(qkv, qkv, qkv, qkv, qkv)


def _combine_kernel(*refs, dilations, heads, hd):
    ng = len(dilations)
    o_refs, l_refs, out_ref, scr = refs[:ng], refs[ng:2 * ng], refs[2 * ng], refs[2 * ng + 1:]
    ts = out_ref.shape[0]
    o_tok, l_tok = [], []
    for gi, dil in enumerate(dilations):
        o_s, l_s = scr[2 * gi], scr[2 * gi + 1]
        o_g = o_refs[gi][...].reshape(ts, o_s.shape[1])
        l_g = l_refs[gi][...].reshape(ts, l_s.shape[1])
        if dil > 1:
            o_s[...] = _dot(_perm_matrix(ts, dil, o_g.dtype), o_g)
            l_s[...] = _dot_f32(_perm_matrix(ts, dil, F32), l_g)
        else:
            o_s[...] = o_g.astype(F32)
            l_s[...] = l_g
        o_tok.append(o_s)
        l_tok.append(l_s)
    ls = [r[...] for r in l_tok]
    m = functools.reduce(jnp.maximum, ls)
    es = [jnp.exp(l - m) for l in ls]
    inv = 1.0 / functools.reduce(lambda a, b: a + b, es)
    ws = [e * inv for e in es]
    for hh in range(heads):
        sl = slice(hh * hd, (hh + 1) * hd)
        acc = ws[0][:, hh:hh + 1] * o_tok[0][:, sl]
        for gi in range(1, ng):
            acc = acc + ws[gi][:, hh:hh + 1] * o_tok[gi][:, sl]
        out_ref[:, sl] = acc.astype(out_ref.dtype)


def _combine(outs, lses, dilations, cfg):
    gw = cfg.group_width
    s = outs[0].shape[0] * outs[0].shape[1]
    ts = _tile(s, 512)
    in_specs, scratch = [], []
    for dil in dilations:
        assert ts % (dil * 2 * SUBLANES) == 0
        in_specs.append(pl.BlockSpec((dil, ts // dil, gw), lambda i: (0, i, 0)))
    for dil in dilations:
        in_specs.append(pl.BlockSpec((dil, ts // dil, LANES), lambda i: (0, i, 0)))
        scratch += [pltpu.VMEM((ts, gw), F32), pltpu.VMEM((ts, LANES), F32)]
    return pl.pallas_call(
        functools.partial(_combine_kernel, dilations=tuple(dilations), heads=cfg.attn_heads_per_group,
                          hd=cfg.attn_head_dim),
        grid=(s // ts,),
        in_specs=in_specs,
        out_specs=pl.BlockSpec((ts, gw), lambda i: (i, 0)),
        out_shape=jax.ShapeDtypeStruct((s, gw), BF16),
        scratch_shapes=scratch,
        compiler_params=_params(("parallel",)),
        name="attn_combine",
    )(*outs, *lses)


def _forward(cfg, x, c, positions, ada_w, ada_b, norm_mix, norm_mlp, w_in, w_alpha_up, b_alpha,
             gla_norm, w_branch_gla, w_branch_attn, w_gate, b_gate, w_out, w_mlp_in, w_mlp_out,
             norm_final):
    bsz, s, d = x.shape
    assert bsz == 1 and d == cfg.d_model
    depth = cfg.depth
    x2 = x.reshape(s, d)
    mod = _ada_mod(c.reshape(d, 1), ada_w, ada_b.reshape(depth, 1, -1))
    dils = [dil for _, dil in cfg.dilated_groups]
    perm_dils = sorted({dil for dil in dils if dil > 1})
    pos = positions.reshape(s)
    pos_all = jnp.concatenate([pos.reshape(s // dil, dil).T.reshape(s) for dil in dils]).reshape(-1, 1)
    cos_t, sin_t = _rope_tables(pos_all, cfg)
    g_mix = norm_mix.reshape(depth, 1, d)
    g_mlp = norm_mlp.reshape(depth, 1, d)
    gw, rank = cfg.gla_width, cfg.gate_rank
    w_mlp_out_b = w_mlp_out.astype(BF16)

    for l in range(depth):
        w_attn = w_in[l, :, gw + rank:].astype(BF16)
        w_glr = w_in[l, :, gw:gw + rank]
        w_glr3 = jnp.pad(jnp.concatenate([w_glr] * 3, axis=1), ((0, 0), (0, LANES - 3 * rank))).astype(BF16)
        wa_hi, wa_lo = _split_bf16(w_alpha_up[l], 2)
        w_alpha_cat = jnp.pad(jnp.concatenate([wa_hi, wa_lo, wa_hi], axis=0), ((0, LANES - 3 * rank), (0, 0)))

        hs = _norm_mod(x2, g_mix, mod, l, 0, 1, cfg, dilations=perm_dils)
        h = hs[0]
        h_by_dil = {1: h, **{dil: hp.reshape(s, d) for dil, hp in zip(perm_dils, hs[1:])}}
        proj = _mm(_mm_plain_kernel, h, w_in, l, gw, BF16, name="gla_proj")
        b_cum = _gla_gate(h, w_glr3, w_alpha_cat, b_alpha[l].reshape(1, -1), cfg)
        o_gla = _gla(proj, b_cum, gla_norm[l].reshape(1, -1), cfg)
        outs, lses = [], []
        for g, (window, dil) in enumerate(cfg.dilated_groups):
            qkv = _attn_proj(h_by_dil[dil], w_attn, cos_t, sin_t, g, cfg)
            o_g, lse_g = _attn_group(qkv.reshape(dil, s // dil, -1), window, dil, cfg, f"dilated_attn_{g}")
            outs.append(o_g)
            lses.append(lse_g)
        o_attn = _combine(outs, lses, dils, cfg)
        merged = _merge(h, o_gla, o_attn, w_gate[l].astype(BF16), b_gate[l].reshape(1, -1),
                        w_branch_gla[l].astype(BF16), w_branch_attn[l].astype(BF16))
        x2 = _mm_resid(merged, w_out, x2, mod, l, 2, tm=1024, tn=512, name="out_proj")

        h = _norm_mod(x2, g_mlp, mod, l, 3, 4, cfg)[0]
        a = _mm(_mm_relu2_kernel, h, w_mlp_in, l, w_mlp_in.shape[2], BF16, name="mlp_in")
        x2 = _mm_resid(a, w_mlp_out_b, x2, mod, l, 5, tm=512, tn=256, name="mlp_out")

    return _final_norm(x2, norm_final.reshape(1, d), cfg).reshape(bsz, s, d)


def kernel(x, c, positions, ada_w, ada_b, norm_mix, norm_mlp, w_in, w_alpha_up, b_alpha, gla_norm,
           w_branch_gla, w_branch_attn, w_gate, b_gate, w_out, w_mlp_in, w_mlp_out, norm_final):
    return _forward(Config(), x, c, positions, ada_w, ada_b, norm_mix, norm_mlp, w_in, w_alpha_up,
                    b_alpha, gla_norm, w_branch_gla, w_branch_attn, w_gate, b_gate, w_out, w_mlp_in,
                    w_mlp_out, norm_final)
```

```python
import functools
from typing import NamedTuple

import jax
import jax.numpy as jnp
from jax import lax
from jax.experimental import pallas as pl
from jax.experimental.pallas import tpu as pltpu

F32 = jnp.float32
BF16 = jnp.bfloat16
LANES = 128
SUBLANES = 8
VMEM_LIMIT = 56 * 1024 * 1024
GLA_BLOCK = 256
LOG2_E = 1.4426950408889634


class Config(NamedTuple):
    d_model: int = 4096
    depth: int = 2
    gla_heads: int = 4
    gate_rank: int = 16
    gate_tau: float = 16.0
    gla_chunk: int = 64
    attn_head_dim: int = 128
    attn_heads_per_group: int = 16
    dilated_groups: tuple = ((128, 1), (512, 4), (2048, 16))
    rope_theta: float = 10000.0
    ff_mult: int = 4
    n_mod: int = 6
    norm_eps: float = 1e-6

    @property
    def gla_dk(self):
        return self.d_model // 2

    @property
    def gla_dv(self):
        return self.d_model

    @property
    def dk_head(self):
        return self.gla_dk // self.gla_heads

    @property
    def dv_head(self):
        return self.gla_dv // self.gla_heads

    @property
    def group_width(self):
        return self.attn_heads_per_group * self.attn_head_dim

    @property
    def attn_width(self):
        return len(self.dilated_groups) * self.group_width

    @property
    def gla_width(self):
        return 2 * self.gla_dk + 2 * self.gla_dv


def _params(sem):
    return pltpu.CompilerParams(dimension_semantics=sem, vmem_limit_bytes=VMEM_LIMIT)


def _sigmoid(x):
    return 1.0 / (1.0 + jnp.exp(-x))


def _dot(a, b):
    return jnp.dot(a, b, preferred_element_type=F32)


def _dot_nt(a, b):
    return lax.dot_general(a, b, (((1,), (1,)), ((), ())), preferred_element_type=F32)


def _dot_tn(a, b):
    return lax.dot_general(a, b, (((0,), (0,)), ((), ())), preferred_element_type=F32)


def _dot_f32(a, b):
    return jnp.dot(a, b, preferred_element_type=F32, precision=lax.Precision.HIGHEST)


def _tile(n, pref):
    t = min(n, pref)
    assert n % t == 0, (n, pref)
    return t


def _perm_matrix(size, inner, dtype):
    outer = size // inner
    assert inner * outer == size and inner & (inner - 1) == 0 and outer & (outer - 1) == 0
    i = lax.broadcasted_iota(jnp.int32, (size, size), 0)
    j = lax.broadcasted_iota(jnp.int32, (size, size), 1)
    src = (i & (inner - 1)) * outer + (i >> (inner.bit_length() - 1))
    return (j == src).astype(dtype)


def _ada_kernel(c_ref, w_ref, b_ref, o_ref, *, rows):
    d, tn = w_ref.shape

    def body(i, acc):
        r = pl.multiple_of(i * rows, rows)
        cc = c_ref[pl.ds(r, rows), :]
        cond = cc * _sigmoid(cc)
        p = w_ref[pl.ds(r, rows), :] * cond
        return acc + jnp.sum(p.reshape(rows // SUBLANES, SUBLANES, tn), axis=0)

    acc = lax.fori_loop(0, d // rows, body, jnp.zeros((SUBLANES, tn), F32))
    o_ref[...] = jnp.sum(acc, axis=0, keepdims=True) + b_ref[...]


def _ada_mod(c_col, ada_w, ada_b3):
    depth, d, n = ada_w.shape
    tn = _tile(n, 512)
    rows = _tile(d, 256)
    return pl.pallas_call(
        functools.partial(_ada_kernel, rows=rows),
        grid=(depth, n // tn),
        in_specs=[
            pl.BlockSpec((d, 1), lambda l, j: (0, 0)),
            pl.BlockSpec((None, d, tn), lambda l, j: (l, 0, j)),
            pl.BlockSpec((None, 1, tn), lambda l, j: (l, 0, j)),
        ],
        out_specs=pl.BlockSpec((None, 1, tn), lambda l, j: (l, 0, j)),
        out_shape=jax.ShapeDtypeStruct((depth, 1, n), F32),
        compiler_params=_params(("parallel", "parallel")),
        name="ada_mod",
    )(c_col, ada_w, ada_b3)


def _rope_kernel(p_ref, inv_ref, sgn_ref, c_ref, s_ref):
    ang = p_ref[...].astype(F32) * inv_ref[...]
    c_ref[...] = jnp.cos(ang)
    s_ref[...] = jnp.sin(ang) * sgn_ref[...]


def _rope_tables(pos_col, cfg):
    s = pos_col.shape[0]
    hd = cfg.attn_head_dim
    half = hd // 2
    inv = cfg.rope_theta ** (-jnp.arange(half, dtype=F32) / half)
    inv2 = jnp.concatenate([inv, inv]).reshape(1, hd)
    sgn = jnp.concatenate([-jnp.ones((half,), F32), jnp.ones((half,), F32)]).reshape(1, hd)
    ts = _tile(s, 1024)
    spec_t = pl.BlockSpec((ts, hd), lambda i: (i, 0))
    spec_c = pl.BlockSpec((1, hd), lambda i: (0, 0))
    return pl.pallas_call(
        _rope_kernel,
        grid=(s // ts,),
        in_specs=[pl.BlockSpec((ts, 1), lambda i: (i, 0)), spec_c, spec_c],
        out_specs=[spec_t, spec_t],
        out_shape=[jax.ShapeDtypeStruct((s, hd), F32)] * 2,
        compiler_params=_params(("parallel",)),
        name="rope_tables",
    )(pos_col, inv2, sgn)


def _norm_mod_kernel(x_ref, g_ref, sc_ref, sh_ref, o_ref, *rest, eps, dilations):
    x = x_ref[...]
    ms = jnp.mean(x * x, axis=-1, keepdims=True)
    y = x * lax.rsqrt(ms + eps) * g_ref[...]
    hb = (y * (1.0 + sc_ref[...]) + sh_ref[...]).astype(o_ref.dtype)
    o_ref[...] = hb
    ts = hb.shape[0]
    for p_ref, dil in zip(rest, dilations):
        n = ts // dil
        hp = _dot(_perm_matrix(ts, n, hb.dtype), hb).astype(p_ref.dtype)
        for r in range(dil):
            p_ref[r] = hp[r * n:(r + 1) * n, :]


def _norm_mod(x, g3, mod, layer, i_shift, i_scale, cfg, dilations=()):
    s, d = x.shape
    ts = _tile(s, 256)
    out_specs = [pl.BlockSpec((ts, d), lambda i: (i, 0))]
    out_shape = [jax.ShapeDtypeStruct((s, d), BF16)]
    for dil in dilations:
        assert ts % (dil * 2 * SUBLANES) == 0
        out_specs.append(pl.BlockSpec((dil, ts // dil, d), lambda i: (0, i, 0)))
        out_shape.append(jax.ShapeDtypeStruct((dil, s // dil, d), BF16))
    return pl.pallas_call(
        functools.partial(_norm_mod_kernel, eps=cfg.norm_eps, dilations=tuple(dilations)),
        grid=(s // ts,),
        in_specs=[
            pl.BlockSpec((ts, d), lambda i: (i, 0)),
            pl.BlockSpec((None, 1, d), lambda i: (layer, 0, 0)),
            pl.BlockSpec((None, 1, d), lambda i: (layer, 0, i_scale)),
            pl.BlockSpec((None, 1, d), lambda i: (layer, 0, i_shift)),
        ],
        out_specs=out_specs,
        out_shape=out_shape,
        compiler_params=_params(("parallel",)),
        name="norm_mod",
    )(x, g3, mod, mod)


def _final_norm_kernel(x_ref, g_ref, o_ref, *, eps):
    x = x_ref[...]
    ms = jnp.mean(x * x, axis=-1, keepdims=True)
    o_ref[...] = x * lax.rsqrt(ms + eps) * g_ref[...]


def _final_norm(x, g2, cfg):
    s, d = x.shape
    ts = _tile(s, 256)
    return pl.pallas_call(
        functools.partial(_final_norm_kernel, eps=cfg.norm_eps),
        grid=(s // ts,),
        in_specs=[pl.BlockSpec((ts, d), lambda i: (i, 0)), pl.BlockSpec((1, d), lambda i: (0, 0))],
        out_specs=pl.BlockSpec((ts, d), lambda i: (i, 0)),
        out_shape=jax.ShapeDtypeStruct((s, d), F32),
        compiler_params=_params(("parallel",)),
        name="final_norm",
    )(x, g2)


def _proj_kernel(x_ref, w_ref, cos_ref, sin_ref, o_ref, *, j_rope0, j_rope_k, j_rope1, q_scale, hd):
    acc = _dot(x_ref[...], w_ref[...])
    j = pl.program_id(1)
    is_rope = jnp.logical_and(j >= j_rope0, j < j_rope1)
    scale = jnp.where(j < j_rope_k, q_scale, 1.0).astype(F32)
    cos = jnp.where(is_rope, cos_ref[...] * scale, 1.0)
    sin = jnp.where(is_rope, sin_ref[...] * scale, 0.0)
    for c in range(acc.shape[1] // hd):
        a = acc[:, c * hd:(c + 1) * hd]
        o_ref[:, c * hd:(c + 1) * hd] = (a * cos + pltpu.roll(a, hd // 2, 1) * sin).astype(o_ref.dtype)


def _attn_proj(h, w_attn, cos_t, sin_t, g, cfg):
    m, k = h.shape
    hd = cfg.attn_head_dim
    gw = cfg.group_width
    tm = _tile(m, 1024)
    tn = _tile(gw, 1024)
    per = gw // tn
    sect = cfg.attn_width // tn
    tbl0 = g * (m // tm)
    return pl.pallas_call(
        functools.partial(_proj_kernel, j_rope0=0, j_rope_k=per, j_rope1=2 * per,
                          q_scale=float(hd) ** -0.5, hd=hd),
        grid=(m // tm, 3 * per),
        in_specs=[
            pl.BlockSpec((tm, k), lambda i, j: (i, 0)),
            pl.BlockSpec((k, tn), lambda i, j: (0, (j // per) * sect + g * per + j % per)),
            pl.BlockSpec((tm, hd), lambda i, j: (tbl0 + i, 0)),
            pl.BlockSpec((tm, hd), lambda i, j: (tbl0 + i, 0)),
        ],
        out_specs=pl.BlockSpec((tm, tn), lambda i, j: (i, j)),
        out_shape=jax.ShapeDtypeStruct((m, 3 * gw), BF16),
        compiler_params=_params(("parallel", "parallel")),
        name=f"attn_proj_{g}",
    )(h, w_attn, cos_t, sin_t)


def _mm_plain_kernel(x_ref, w_ref, o_ref):
    o_ref[...] = _dot(x_ref[...], w_ref[...]).astype(o_ref.dtype)


def _mm_relu2_kernel(x_ref, w_ref, o_ref):
    a = jnp.maximum(_dot(x_ref[...], w_ref[...]), 0.0)
    o_ref[...] = (a * a).astype(o_ref.dtype)


def _mm_resid_kernel(x_ref, w_ref, r_ref, g_ref, o_ref):
    o_ref[...] = r_ref[...] + g_ref[...] * _dot(x_ref[...], w_ref[...])


def _mm_resid_acc_kernel(x_ref, w_ref, r_ref, g_ref, o_ref, acc_ref):
    kk = pl.program_id(2)

    @pl.when(kk == 0)
    def _():
        acc_ref[...] = jnp.zeros_like(acc_ref)

    acc_ref[...] += _dot(x_ref[...], w_ref[...])

    @pl.when(kk == pl.num_programs(2) - 1)
    def _():
        o_ref[...] = r_ref[...] + g_ref[...] * acc_ref[...]


def _mm(kernel_fn, x, w3, layer, n, out_dtype, *, tm=1024, tn=1024, name):
    m, k = x.shape
    assert w3.shape[1] == k and n <= w3.shape[2]
    tm, tn = _tile(m, tm), _tile(n, tn)
    return pl.pallas_call(
        kernel_fn,
        grid=(m // tm, n // tn),
        in_specs=[pl.BlockSpec((tm, k), lambda i, j: (i, 0)),
                  pl.BlockSpec((None, k, tn), lambda i, j: (layer, 0, j))],
        out_specs=pl.BlockSpec((tm, tn), lambda i, j: (i, j)),
        out_shape=jax.ShapeDtypeStruct((m, n), out_dtype),
        compiler_params=_params(("parallel", "parallel")),
        name=name,
    )(x, w3)


def _mm_resid(x, w3, resid, mod, layer, i_gate, *, tk, name):
    m, k = x.shape
    n = w3.shape[2]
    assert w3.shape[1] == k
    tm, tn = _tile(m, 1024), _tile(n, 1024)
    gate_blocks = n // tn
    tk = _tile(k, tk)
    if tk == k:
        return pl.pallas_call(
            _mm_resid_kernel,
            grid=(m // tm, n // tn),
            in_specs=[
                pl.BlockSpec((tm, k), lambda i, j: (i, 0)),
                pl.BlockSpec((None, k, tn), lambda i, j: (layer, 0, j)),
                pl.BlockSpec((tm, tn), lambda i, j: (i, j)),
                pl.BlockSpec((None, 1, tn), lambda i, j: (layer, 0, i_gate * gate_blocks + j)),
            ],
            out_specs=pl.BlockSpec((tm, tn), lambda i, j: (i, j)),
            out_shape=jax.ShapeDtypeStruct((m, n), F32),
            compiler_params=_params(("parallel", "parallel")),
            name=name,
        )(x, w3, resid, mod)
    return pl.pallas_call(
        _mm_resid_acc_kernel,
        grid=(m // tm, n // tn, k // tk),
        in_specs=[
            pl.BlockSpec((tm, tk), lambda i, j, kk: (i, kk)),
            pl.BlockSpec((None, tk, tn), lambda i, j, kk: (layer, kk, j)),
            pl.BlockSpec((tm, tn), lambda i, j, kk: (i, j)),
            pl.BlockSpec((None, 1, tn), lambda i, j, kk: (layer, 0, i_gate * gate_blocks + j)),
        ],
        out_specs=pl.BlockSpec((tm, tn), lambda i, j, kk: (i, j)),
        out_shape=jax.ShapeDtypeStruct((m, n), F32),
        scratch_shapes=[pltpu.VMEM((tm, tn), F32)],
        compiler_params=_params(("parallel", "parallel", "arbitrary")),
        name=name,
    )(x, w3, resid, mod)


def _merge_kernel(h_ref, og_ref, oa_ref, wga_ref, wgb_ref, bga_ref, bgb_ref, wa_ref, wb_ref, o_ref):
    h = h_ref[...]
    ga = _sigmoid(_dot(h, wga_ref[...]) + bga_ref[...])
    gb = _sigmoid(_dot(h, wgb_ref[...]) + bgb_ref[...])
    a = _dot(og_ref[...], wa_ref[...])
    b = _dot(oa_ref[...], wb_ref[...])
    o_ref[...] = (ga * a + gb * b).astype(o_ref.dtype)


def _merge(h, o_gla, o_attn, w_gate, b_gate2, w_a, w_b):
    m, d = h.shape
    tm, tn = _tile(m, 512), _tile(d, 512)
    nb = d // tn
    x_spec = lambda kdim: pl.BlockSpec((tm, kdim), lambda j, i: (i, 0))
    w_spec = lambda kdim, off: pl.BlockSpec((kdim, tn), lambda j, i: (0, j + off))
    b_spec = lambda off: pl.BlockSpec((1, tn), lambda j, i: (0, j + off))
    return pl.pallas_call(
        _merge_kernel,
        grid=(nb, m // tm),
        in_specs=[
            x_spec(d), x_spec(o_gla.shape[1]), x_spec(o_attn.shape[1]),
            w_spec(d, 0), w_spec(d, nb), b_spec(0), b_spec(nb),
            w_spec(w_a.shape[0], 0), w_spec(w_b.shape[0], 0),
        ],
        out_specs=pl.BlockSpec((tm, tn), lambda j, i: (i, j)),
        out_shape=jax.ShapeDtypeStruct((m, d), BF16),
        compiler_params=_params(("parallel", "parallel")),
        name="gated_merge",
    )(h, o_gla, o_attn, w_gate, w_gate, b_gate2, b_gate2, w_a, w_b)


def _split_bf16(x, pieces):
    out = []
    for _ in range(pieces - 1):
        p = x.astype(BF16)
        out.append(p)
        x = x - p.astype(F32)
    out.append(x.astype(BF16))
    return out


def _gate_kernel(h_ref, wl_ref, wc_ref, ba_ref, b_ref, *, chunk, tau, rank):
    tm = h_ref.shape[0]
    x = _dot(h_ref[...], wl_ref[...])
    hi = x.astype(BF16).astype(F32)
    lane = lax.broadcasted_iota(jnp.int32, x.shape, 1)
    a = jnp.where(lane < 2 * rank, hi, x - hi).astype(BF16)
    logits = _dot(a, wc_ref[...]) + ba_ref[...]
    g = (jnp.minimum(logits, 0.0) - jnp.log(1.0 + jnp.exp(-jnp.abs(logits)))) * (LOG2_E / tau)
    ri = lax.broadcasted_iota(jnp.int32, (tm, tm), 0)
    ci = lax.broadcasted_iota(jnp.int32, (tm, tm), 1)
    tril = jnp.where(jnp.logical_and(ri >= ci, (ri ^ ci) < chunk), 1.0, 0.0).astype(BF16)
    p1, p2, p3 = _split_bf16(g, 3)
    b_ref[...] = _dot(tril, p1) + _dot(tril, p2) + _dot(tril, p3)


def _gla_gate(h, w_glr3, w_alpha_cat, b_alpha2, chunk, cfg):
    s, d = h.shape
    n = w_alpha_cat.shape[1]
    tm = _tile(s, max(256, chunk))
    assert tm % chunk == 0 and chunk & (chunk - 1) == 0 and 3 * cfg.gate_rank <= LANES
    return pl.pallas_call(
        functools.partial(_gate_kernel, chunk=chunk, tau=cfg.gate_tau, rank=cfg.gate_rank),
        grid=(s // tm,),
        in_specs=[
            pl.BlockSpec((tm, d), lambda i: (i, 0)),
            pl.BlockSpec((d, LANES), lambda i: (0, 0)),
            pl.BlockSpec((LANES, n), lambda i: (0, 0)),
            pl.BlockSpec((1, n), lambda i: (0, 0)),
        ],
        out_specs=pl.BlockSpec((tm, n), lambda i: (i, 0)),
        out_shape=jax.ShapeDtypeStruct((s, n), F32),
        compiler_params=_params(("parallel",)),
        name="gla_gate",
    )(h, w_glr3, w_alpha_cat, b_alpha2)


def _gla_block_kernel(q_ref, k_ref, v_ref, gr_ref, b_ref, gn_ref, o_ref, st_ref, *, heads, q_scale, eps):
    rows = q_ref.shape[0]
    dk = q_ref.shape[1] // heads
    dv = v_ref.shape[1] // heads
    sub = SUBLANES
    half = rows // 2

    @pl.when(pl.program_id(1) == 0)
    def _():
        st_ref[...] = jnp.zeros_like(st_ref)

    ri = lax.broadcasted_iota(jnp.int32, (half, half), 0)
    ci = lax.broadcasted_iota(jnp.int32, (half, half), 1)
    sub_i = lax.broadcasted_iota(jnp.int32, (sub, half), 0)
    lane_i = lax.broadcasted_iota(jnp.int32, (sub, half), 1)

    def cross_scores(q, k, b, g_blk):
        n = q.shape[0]
        span = 2 * g_blk
        zero = jnp.zeros((g_blk, dk), F32)
        q_parts, k_parts = [], []
        for s0 in range(0, n, span):
            ref = jnp.broadcast_to(b[s0 + g_blk - 1:s0 + g_blk, :], (g_blk, dk))
            lo, up = slice(s0, s0 + g_blk), slice(s0 + g_blk, s0 + span)
            q_parts += [zero, q[up] * jnp.exp2(b[up] - ref)]
            k_parts += [k[lo] * jnp.exp2(ref - b[lo]), zero]
        qg = jnp.concatenate(q_parts, axis=0).astype(BF16)
        kg = jnp.concatenate(k_parts, axis=0).astype(BF16)
        return _dot_nt(qg, kg)

    for hh in range(heads):
        ksl = slice(hh * dk, (hh + 1) * dk)
        vsl = slice(hh * dv, (hh + 1) * dv)

        q = q_ref[:, ksl].astype(F32) * q_scale
        k = k_ref[:, ksl].astype(F32)
        vb = v_ref[:, vsl]
        b = b_ref[:, ksl]
        b_last = b[rows - 1:rows, :]

        st = st_ref[hh]
        qe = (q * jnp.exp2(b)).astype(BF16)
        o = _dot_nt(qe, st.astype(BF16))

        b_mid = jnp.broadcast_to(b[half - 1:half, :], (half, dk))
        x_lo = _dot_nt((q[half:] * jnp.exp2(b[half:] - b_mid)).astype(BF16),
                       (k[:half] * jnp.exp2(b_mid - b[:half])).astype(BF16))
        a_halves = []
        for hf in range(2):
            rs = slice(hf * half, (hf + 1) * half)
            qh, kh, bh = q[rs], k[rs], b[rs]
            tiles = []
            for lo in range(0, half, sub):
                qi, ki, bi = qh[lo:lo + sub, :], kh[lo:lo + sub, :], bh[lo:lo + sub, :]
                acc = jnp.zeros((sub, half), F32)
                for jj in range(sub):
                    t = qi * ki[jj:jj + 1, :] * jnp.exp2(bi - bi[jj:jj + 1, :])
                    col = jnp.sum(t, axis=1, keepdims=True)
                    keep = jnp.logical_and(lane_i == lo + jj, sub_i >= jj)
                    acc = jnp.where(keep, col, acc)
                tiles.append(acc)
            a = jnp.concatenate(tiles, axis=0) + cross_scores(qh, kh, bh, half // 2)
            g_blk = half // 4
            while g_blk >= sub:
                same = (ri ^ ci) < 2 * g_blk
                a = a + jnp.where(same, cross_scores(qh, kh, bh, g_blk), 0.0)
                g_blk //= 2
            a_halves.append(a)
        top = jnp.concatenate([a_halves[0], jnp.zeros((half, half), F32)], axis=1)
        bot = jnp.concatenate([x_lo, a_halves[1]], axis=1)
        s = jnp.concatenate([top, bot], axis=0).astype(BF16)
        o = o + _dot(s, vb)

        kd = (k * jnp.exp2(b_last - b)).astype(BF16)
        st_ref[hh] = st * jnp.exp2(b_last) + _dot_tn(vb, kd)

        ms = jnp.mean(o * o, axis=-1, keepdims=True)
        y = o * lax.rsqrt(ms + eps) * gn_ref[...]
        gr = gr_ref[:, vsl].astype(F32)
        o_ref[:, vsl] = (y * (gr * _sigmoid(gr))).astype(o_ref.dtype)


def _gla_block(proj, b_cum, gla_norm2, rows, cfg, heads_per_step=2):
    s = proj.shape[0]
    nh, dk, dv = cfg.gla_heads, cfg.dk_head, cfg.dv_head
    hp = heads_per_step
    assert s % rows == 0 and nh % hp == 0 and rows & (rows - 1) == 0 and rows >= 8 * SUBLANES
    assert cfg.gla_dk % (hp * dk) == 0 and (2 * cfg.gla_dk) % (hp * dv) == 0
    k_off = cfg.gla_dk // (hp * dk)
    v_off = (2 * cfg.gla_dk) // (hp * dv)
    r_off = (2 * cfg.gla_dk + cfg.gla_dv) // (hp * dv)
    return pl.pallas_call(
        functools.partial(_gla_block_kernel, heads=hp, q_scale=float(dk) ** -0.5, eps=cfg.norm_eps),
        grid=(nh // hp, s // rows),
        in_specs=[
            pl.BlockSpec((rows, hp * dk), lambda h, t: (t, h)),
            pl.BlockSpec((rows, hp * dk), lambda h, t: (t, k_off + h)),
            pl.BlockSpec((rows, hp * dv), lambda h, t: (t, v_off + h)),
            pl.BlockSpec((rows, hp * dv), lambda h, t: (t, r_off + h)),
            pl.BlockSpec((rows, hp * dk), lambda h, t: (t, h)),
            pl.BlockSpec((1, dv), lambda h, t: (0, 0)),
        ],
        out_specs=pl.BlockSpec((rows, hp * dv), lambda h, t: (t, h)),
        out_shape=jax.ShapeDtypeStruct((s, cfg.gla_dv), BF16),
        scratch_shapes=[pltpu.VMEM((hp, dv, dk), F32)],
        compiler_params=_params(("parallel", "arbitrary")),
        name="gla",
    )(proj, proj, proj, proj, b_cum, gla_norm2)


def _attn_kernel(q_ref, kp_ref, kc_ref, vp_ref, vc_ref, o_ref, lse_ref, s_scr, p_scr, *, heads, hd):
    blk = q_ref.shape[0]
    n = pl.program_id(1)
    qi = lax.broadcasted_iota(jnp.int32, (blk, 2 * blk), 0)
    kj = lax.broadcasted_iota(jnp.int32, (blk, 2 * blk), 1)
    dist = blk + qi - kj
    band = jnp.logical_and(dist >= 0, dist <= blk)
    mask = jnp.logical_and(band, jnp.logical_or(kj >= blk, n > 0))
    lane = lax.broadcasted_iota(jnp.int32, (blk, LANES), 1)
    for hh in range(heads):
        sl = slice(hh * hd, (hh + 1) * hd)
        q = q_ref[:, sl]
        s = jnp.concatenate([_dot_nt(q, kp_ref[:, sl]), _dot_nt(q, kc_ref[:, sl])], axis=1)
        s_scr[hh] = jnp.where(mask, s, -jnp.inf)
    lse_tile = jnp.zeros((blk, LANES), F32)
    for hh in range(heads):
        s = s_scr[hh]
        m = jnp.max(s, axis=1, keepdims=True)
        p = jnp.exp(s - m)
        den = jnp.sum(p, axis=1, keepdims=True)
        p_scr[hh] = (p * (1.0 / den)).astype(BF16)
        lse_tile = jnp.where(lane == hh, m + jnp.log(den), lse_tile)
    lse_ref[...] = lse_tile
    for hh in range(heads):
        sl = slice(hh * hd, (hh + 1) * hd)
        p = p_scr[hh]
        o = _dot(p[:, :blk], vp_ref[:, sl]) + _dot(p[:, blk:], vc_ref[:, sl])
        o_ref[:, sl] = o.astype(o_ref.dtype)


def _attn_group(qkv, window, dilation, cfg, name):
    dil, u, w3 = qkv.shape
    gw = cfg.group_width
    heads = cfg.attn_heads_per_group
    blk = window // dilation
    assert dil == dilation and w3 == 3 * gw and u % blk == 0 and heads <= LANES
    nb = u // blk

    def cur(part):
        return pl.BlockSpec((None, blk, gw), lambda r, n: (r, n, part))

    def prev(part):
        return pl.BlockSpec((None, blk, gw), lambda r, n: (r, jnp.maximum(n - 1, 0), part))

    return pl.pallas_call(
        functools.partial(_attn_kernel, heads=heads, hd=cfg.attn_head_dim),
        grid=(dilation, nb),
        in_specs=[cur(0), prev(1), cur(1), prev(2), cur(2)],
        out_specs=[pl.BlockSpec((None, blk, gw), lambda r, n: (r, n, 0)),
                   pl.BlockSpec((None, blk, LANES), lambda r, n: (r, n, 0))],
        out_shape=[jax.ShapeDtypeStruct((dilation, u, gw), BF16),
                   jax.ShapeDtypeStruct((dilation, u, LANES), F32)],
        scratch_shapes=[pltpu.VMEM((heads, blk, 2 * blk), F32), pltpu.VMEM((heads, blk, 2 * blk), BF16)],
        compiler_params=_params(("parallel", "arbitrary")),
        name=name,
    )(qkv, qkv, qkv, qkv, qkv)


def _combine_kernel(*refs, dilations, hd):
    ng = len(dilations)
    o_refs, l_refs, out_ref, scr = refs[:ng], refs[ng:2 * ng], refs[2 * ng], refs[2 * ng + 1:]
    ts = out_ref.shape[0]
    o_tok, l_tok = [], []
    for gi, dil in enumerate(dilations):
        o_s, l_s = scr[2 * gi], scr[2 * gi + 1]
        o_g = o_refs[gi][...].reshape(ts, o_s.shape[1])
        l_g = l_refs[gi][...].reshape(ts, l_s.shape[1])
        if dil > 1:
            o_s[...] = _dot(_perm_matrix(ts, dil, o_g.dtype), o_g)
            l_s[...] = _dot_f32(_perm_matrix(ts, dil, F32), l_g)
        else:
            o_s[...] = o_g.astype(F32)
            l_s[...] = l_g
        o_tok.append(o_s)
        l_tok.append(l_s)
    ls = [r[...] for r in l_tok]
    m = functools.reduce(jnp.maximum, ls)
    es = [jnp.exp(l - m) for l in ls]
    inv = 1.0 / functools.reduce(lambda a, b: a + b, es)
    ws = [e * inv for e in es]
    for hh in range(out_ref.shape[1] // hd):
        sl = slice(hh * hd, (hh + 1) * hd)
        acc = ws[0][:, hh:hh + 1] * o_tok[0][:, sl]
        for gi in range(1, ng):
            acc = acc + ws[gi][:, hh:hh + 1] * o_tok[gi][:, sl]
        out_ref[:, sl] = acc.astype(out_ref.dtype)


def _combine(outs, lses, dilations, cfg):
    gw = cfg.group_width
    s = outs[0].shape[0] * outs[0].shape[1]
    ts = _tile(s, 512)
    in_specs, scratch = [], []
    for dil in dilations:
        assert ts % (dil * 2 * SUBLANES) == 0
        in_specs.append(pl.BlockSpec((dil, ts // dil, gw), lambda i: (0, i, 0)))
    for dil in dilations:
        in_specs.append(pl.BlockSpec((dil, ts // dil, LANES), lambda i: (0, i, 0)))
        scratch += [pltpu.VMEM((ts, gw), F32), pltpu.VMEM((ts, LANES), F32)]
    return pl.pallas_call(
        functools.partial(_combine_kernel, dilations=tuple(dilations), hd=cfg.attn_head_dim),
        grid=(s // ts,),
        in_specs=in_specs,
        out_specs=pl.BlockSpec((ts, gw), lambda i: (i, 0)),
        out_shape=jax.ShapeDtypeStruct((s, gw), BF16),
        scratch_shapes=scratch,
        compiler_params=_params(("parallel",)),
        name="attn_combine",
    )(*outs, *lses)


def _forward(cfg, x, c, positions, ada_w, ada_b, norm_mix, norm_mlp, w_in, w_alpha_up, b_alpha,
             gla_norm, w_branch_gla, w_branch_attn, w_gate, b_gate, w_out, w_mlp_in, w_mlp_out,
             norm_final):
    bsz, s, d = x.shape
    assert bsz == 1 and d == cfg.d_model
    depth = cfg.depth
    x2 = x.reshape(s, d)
    mod = _ada_mod(c.reshape(d, 1), ada_w, ada_b.reshape(depth, 1, -1))
    dils = [dil for _, dil in cfg.dilated_groups]
    perm_dils = sorted({dil for dil in dils if dil > 1})
    pos = positions.reshape(s)
    pos_all = jnp.concatenate([pos.reshape(s // dil, dil).T.reshape(s) for dil in dils]).reshape(-1, 1)
    cos_t, sin_t = _rope_tables(pos_all, cfg)
    g_mix = norm_mix.reshape(depth, 1, d)
    g_mlp = norm_mlp.reshape(depth, 1, d)
    gw, rank = cfg.gla_width, cfg.gate_rank
    w_in_b = w_in.astype(BF16)
    w_out_b = w_out.astype(BF16)
    w_mlp_in_b = w_mlp_in.astype(BF16)
    w_mlp_out_b = w_mlp_out.astype(BF16)

    for l in range(depth):
        w_attn = w_in_b[l, :, gw + rank:]
        w_glr = w_in[l, :, gw:gw + rank]
        w_glr3 = jnp.pad(jnp.concatenate([w_glr] * 3, axis=1), ((0, 0), (0, LANES - 3 * rank))).astype(BF16)
        wa_hi, wa_lo = _split_bf16(w_alpha_up[l], 2)
        w_alpha_cat = jnp.pad(jnp.concatenate([wa_hi, wa_lo, wa_hi], axis=0), ((0, LANES - 3 * rank), (0, 0)))

        hs = _norm_mod(x2, g_mix, mod, l, 0, 1, cfg, dilations=perm_dils)
        h = hs[0]
        h_by_dil = {1: h, **{dil: hp.reshape(s, d) for dil, hp in zip(perm_dils, hs[1:])}}
        proj = _mm(_mm_plain_kernel, h, w_in_b, l, gw, BF16, name="gla_proj")
        b_cum = _gla_gate(h, w_glr3, w_alpha_cat, b_alpha[l].reshape(1, -1), GLA_BLOCK, cfg)
        o_gla = _gla_block(proj, b_cum, gla_norm[l].reshape(1, -1), GLA_BLOCK, cfg)
        outs, lses = [], []
        for g, (window, dil) in enumerate(cfg.dilated_groups):
            qkv = _attn_proj(h_by_dil[dil], w_attn, cos_t, sin_t, g, cfg)
            o_g, lse_g = _attn_group(qkv.reshape(dil, s // dil, -1), window, dil, cfg, f"dilated_attn_{g}")
            outs.append(o_g)
            lses.append(lse_g)
        o_attn = _combine(outs, lses, dils, cfg)
        merged = _merge(h, o_gla, o_attn, w_gate[l].astype(BF16), b_gate[l].reshape(1, -1),
                        w_branch_gla[l].astype(BF16), w_branch_attn[l].astype(BF16))
        x2 = _mm_resid(merged, w_out_b, x2, mod, l, 2, tk=4096, name="out_proj")

        h = _norm_mod(x2, g_mlp, mod, l, 3, 4, cfg)[0]
        a = _mm(_mm_relu2_kernel, h, w_mlp_in_b, l, w_mlp_in_b.shape[2], BF16, name="mlp_in")
        x2 = _mm_resid(a, w_mlp_out_b, x2, mod, l, 5, tk=2048, name="mlp_out")

    return _final_norm(x2, norm_final.reshape(1, d), cfg).reshape(bsz, s, d)


def kernel(x, c, positions, ada_w, ada_b, norm_mix, norm_mlp, w_in, w_alpha_up, b_alpha, gla_norm,
           w_branch_gla, w_branch_attn, w_gate, b_gate, w_out, w_mlp_in, w_mlp_out, norm_final):
    return _forward(Config(), x, c, positions, ada_w, ada_b, norm_mix, norm_mlp, w_in, w_alpha_up,
                    b_alpha, gla_norm, w_branch_gla, w_branch_attn, w_gate, b_gate, w_out, w_mlp_in,
                    w_mlp_out, norm_final)
```

```python
import functools
from typing import NamedTuple

import jax
import jax.numpy as jnp
from jax import lax
from jax.experimental import pallas as pl
from jax.experimental.pallas import tpu as pltpu

F32 = jnp.float32
BF16 = jnp.bfloat16
LANES = 128
SUBLANES = 8
VMEM_LIMIT = 56 * 1024 * 1024
GLA_BLOCK = 256
LOG2_E = 1.4426950408889634


class Config(NamedTuple):
    d_model: int = 4096
    depth: int = 2
    gla_heads: int = 4
    gate_rank: int = 16
    gate_tau: float = 16.0
    gla_chunk: int = 64
    attn_head_dim: int = 128
    attn_heads_per_group: int = 16
    dilated_groups: tuple = ((128, 1), (512, 4), (2048, 16))
    rope_theta: float = 10000.0
    ff_mult: int = 4
    n_mod: int = 6
    norm_eps: float = 1e-6

    @property
    def gla_dk(self):
        return self.d_model // 2

    @property
    def gla_dv(self):
        return self.d_model

    @property
    def dk_head(self):
        return self.gla_dk // self.gla_heads

    @property
    def dv_head(self):
        return self.gla_dv // self.gla_heads

    @property
    def group_width(self):
        return self.attn_heads_per_group * self.attn_head_dim

    @property
    def attn_width(self):
        return len(self.dilated_groups) * self.group_width

    @property
    def gla_width(self):
        return 2 * self.gla_dk + 2 * self.gla_dv


def _params(sem):
    return pltpu.CompilerParams(dimension_semantics=sem, vmem_limit_bytes=VMEM_LIMIT)


def _sigmoid(x):
    return 1.0 / (1.0 + jnp.exp(-x))


def _dot(a, b):
    return jnp.dot(a, b, preferred_element_type=F32)


def _dot_nt(a, b):
    return lax.dot_general(a, b, (((1,), (1,)), ((), ())), preferred_element_type=F32)


def _dot_tn(a, b):
    return lax.dot_general(a, b, (((0,), (0,)), ((), ())), preferred_element_type=F32)


def _dot_f32(a, b):
    return jnp.dot(a, b, preferred_element_type=F32, precision=lax.Precision.HIGHEST)


def _tile(n, pref):
    t = min(n, pref)
    assert n % t == 0, (n, pref)
    return t


def _perm_matrix(size, inner, dtype):
    outer = size // inner
    assert inner * outer == size and inner & (inner - 1) == 0 and outer & (outer - 1) == 0
    i = lax.broadcasted_iota(jnp.int32, (size, size), 0)
    j = lax.broadcasted_iota(jnp.int32, (size, size), 1)
    src = (i & (inner - 1)) * outer + (i >> (inner.bit_length() - 1))
    return (j == src).astype(dtype)


def _ada_kernel(c_ref, w_ref, b_ref, o_ref, *, rows):
    d, tn = w_ref.shape

    def body(i, acc):
        r = pl.multiple_of(i * rows, rows)
        cc = c_ref[pl.ds(r, rows), :]
        cond = cc * _sigmoid(cc)
        p = w_ref[pl.ds(r, rows), :] * cond
        return acc + jnp.sum(p.reshape(rows // SUBLANES, SUBLANES, tn), axis=0)

    acc = lax.fori_loop(0, d // rows, body, jnp.zeros((SUBLANES, tn), F32))
    o_ref[...] = jnp.sum(acc, axis=0, keepdims=True) + b_ref[...]


def _ada_mod(c_col, ada_w, ada_b3):
    depth, d, n = ada_w.shape
    tn = _tile(n, 512)
    rows = _tile(d, 256)
    return pl.pallas_call(
        functools.partial(_ada_kernel, rows=rows),
        grid=(depth, n // tn),
        in_specs=[
            pl.BlockSpec((d, 1), lambda l, j: (0, 0)),
            pl.BlockSpec((None, d, tn), lambda l, j: (l, 0, j)),
            pl.BlockSpec((None, 1, tn), lambda l, j: (l, 0, j)),
        ],
        out_specs=pl.BlockSpec((None, 1, tn), lambda l, j: (l, 0, j)),
        out_shape=jax.ShapeDtypeStruct((depth, 1, n), F32),
        compiler_params=_params(("parallel", "parallel")),
        name="ada_mod",
    )(c_col, ada_w, ada_b3)


def _rope_kernel(p_ref, inv_ref, sgn_ref, c_ref, s_ref):
    ang = p_ref[...].astype(F32) * inv_ref[...]
    c_ref[...] = jnp.cos(ang)
    s_ref[...] = jnp.sin(ang) * sgn_ref[...]


def _rope_tables(pos_col, cfg):
    s = pos_col.shape[0]
    hd = cfg.attn_head_dim
    half = hd // 2
    inv = cfg.rope_theta ** (-jnp.arange(half, dtype=F32) / half)
    inv2 = jnp.concatenate([inv, inv]).reshape(1, hd)
    sgn = jnp.concatenate([-jnp.ones((half,), F32), jnp.ones((half,), F32)]).reshape(1, hd)
    ts = _tile(s, 1024)
    spec_t = pl.BlockSpec((ts, hd), lambda i: (i, 0))
    spec_c = pl.BlockSpec((1, hd), lambda i: (0, 0))
    return pl.pallas_call(
        _rope_kernel,
        grid=(s // ts,),
        in_specs=[pl.BlockSpec((ts, 1), lambda i: (i, 0)), spec_c, spec_c],
        out_specs=[spec_t, spec_t],
        out_shape=[jax.ShapeDtypeStruct((s, hd), F32)] * 2,
        compiler_params=_params(("parallel",)),
        name="rope_tables",
    )(pos_col, inv2, sgn)


def _norm_mod_kernel(x_ref, g_ref, sc_ref, sh_ref, o_ref, *rest, eps, dilations):
    x = x_ref[...]
    ms = jnp.mean(x * x, axis=-1, keepdims=True)
    y = x * lax.rsqrt(ms + eps) * g_ref[...]
    hb = (y * (1.0 + sc_ref[...]) + sh_ref[...]).astype(o_ref.dtype)
    o_ref[...] = hb
    ts = hb.shape[0]
    for p_ref, dil in zip(rest, dilations):
        n = ts // dil
        hp = _dot(_perm_matrix(ts, n, hb.dtype), hb).astype(p_ref.dtype)
        for r in range(dil):
            p_ref[r] = hp[r * n:(r + 1) * n, :]


def _norm_mod(x, g3, mod, layer, i_shift, i_scale, cfg, dilations=()):
    s, d = x.shape
    ts = _tile(s, 256)
    out_specs = [pl.BlockSpec((ts, d), lambda i: (i, 0))]
    out_shape = [jax.ShapeDtypeStruct((s, d), BF16)]
    for dil in dilations:
        assert ts % (dil * 2 * SUBLANES) == 0
        out_specs.append(pl.BlockSpec((dil, ts // dil, d), lambda i: (0, i, 0)))
        out_shape.append(jax.ShapeDtypeStruct((dil, s // dil, d), BF16))
    return pl.pallas_call(
        functools.partial(_norm_mod_kernel, eps=cfg.norm_eps, dilations=tuple(dilations)),
        grid=(s // ts,),
        in_specs=[
            pl.BlockSpec((ts, d), lambda i: (i, 0)),
            pl.BlockSpec((None, 1, d), lambda i: (layer, 0, 0)),
            pl.BlockSpec((None, 1, d), lambda i: (layer, 0, i_scale)),
            pl.BlockSpec((None, 1, d), lambda i: (layer, 0, i_shift)),
        ],
        out_specs=out_specs,
        out_shape=out_shape,
        compiler_params=_params(("parallel",)),
        name="norm_mod",
    )(x, g3, mod, mod)


def _final_norm_kernel(x_ref, g_ref, o_ref, *, eps):
    x = x_ref[...]
    ms = jnp.mean(x * x, axis=-1, keepdims=True)
    o_ref[...] = x * lax.rsqrt(ms + eps) * g_ref[...]


def _final_norm(x, g2, cfg):
    s, d = x.shape
    ts = _tile(s, 256)
    return pl.pallas_call(
        functools.partial(_final_norm_kernel, eps=cfg.norm_eps),
        grid=(s // ts,),
        in_specs=[pl.BlockSpec((ts, d), lambda i: (i, 0)), pl.BlockSpec((1, d), lambda i: (0, 0))],
        out_specs=pl.BlockSpec((ts, d), lambda i: (i, 0)),
        out_shape=jax.ShapeDtypeStruct((s, d), F32),
        compiler_params=_params(("parallel",)),
        name="final_norm",
    )(x, g2)


def _proj_kernel(x_ref, w_ref, cos_ref, sin_ref, o_ref, *, j_rope0, j_rope_k, j_rope1, q_scale, hd):
    acc = _dot(x_ref[...], w_ref[...])
    j = pl.program_id(1)
    is_rope = jnp.logical_and(j >= j_rope0, j < j_rope1)
    scale = jnp.where(j < j_rope_k, q_scale, 1.0).astype(F32)
    cos = jnp.where(is_rope, cos_ref[...] * scale, 1.0)
    sin = jnp.where(is_rope, sin_ref[...] * scale, 0.0)
    for c in range(acc.shape[1] // hd):
        a = acc[:, c * hd:(c + 1) * hd]
        o_ref[:, c * hd:(c + 1) * hd] = (a * cos + pltpu.roll(a, hd // 2, 1) * sin).astype(o_ref.dtype)


def _attn_proj(h, w_attn, cos_t, sin_t, g, cfg):
    m, k = h.shape
    hd = cfg.attn_head_dim
    gw = cfg.group_width
    tm = _tile(m, 1024)
    tn = _tile(gw, 1024)
    per = gw // tn
    sect = cfg.attn_width // tn
    tbl0 = g * (m // tm)
    return pl.pallas_call(
        functools.partial(_proj_kernel, j_rope0=0, j_rope_k=per, j_rope1=2 * per,
                          q_scale=float(hd) ** -0.5, hd=hd),
        grid=(m // tm, 3 * per),
        in_specs=[
            pl.BlockSpec((tm, k), lambda i, j: (i, 0)),
            pl.BlockSpec((k, tn), lambda i, j: (0, (j // per) * sect + g * per + j % per)),
            pl.BlockSpec((tm, hd), lambda i, j: (tbl0 + i, 0)),
            pl.BlockSpec((tm, hd), lambda i, j: (tbl0 + i, 0)),
        ],
        out_specs=pl.BlockSpec((tm, tn), lambda i, j: (i, j)),
        out_shape=jax.ShapeDtypeStruct((m, 3 * gw), BF16),
        compiler_params=_params(("parallel", "parallel")),
        name=f"attn_proj_{g}",
    )(h, w_attn, cos_t, sin_t)


def _mm_plain_kernel(x_ref, w_ref, o_ref):
    o_ref[...] = _dot(x_ref[...], w_ref[...]).astype(o_ref.dtype)


def _mm_relu2_kernel(x_ref, w_ref, o_ref):
    a = jnp.maximum(_dot(x_ref[...], w_ref[...]), 0.0)
    o_ref[...] = (a * a).astype(o_ref.dtype)


def _mm_resid_kernel(x_ref, w_ref, r_ref, g_ref, o_ref):
    o_ref[...] = r_ref[...] + g_ref[...] * _dot(x_ref[...], w_ref[...])


def _mm_resid_acc_kernel(x_ref, w_ref, r_ref, g_ref, o_ref, acc_ref):
    kk = pl.program_id(2)

    @pl.when(kk == 0)
    def _():
        acc_ref[...] = jnp.zeros_like(acc_ref)

    acc_ref[...] += _dot(x_ref[...], w_ref[...])

    @pl.when(kk == pl.num_programs(2) - 1)
    def _():
        o_ref[...] = r_ref[...] + g_ref[...] * acc_ref[...]


def _mm(kernel_fn, x, w3, layer, n, out_dtype, *, tm=1024, tn=1024, name):
    m, k = x.shape
    assert w3.shape[1] == k and n <= w3.shape[2]
    tm, tn = _tile(m, tm), _tile(n, tn)
    return pl.pallas_call(
        kernel_fn,
        grid=(m // tm, n // tn),
        in_specs=[pl.BlockSpec((tm, k), lambda i, j: (i, 0)),
                  pl.BlockSpec((None, k, tn), lambda i, j: (layer, 0, j))],
        out_specs=pl.BlockSpec((tm, tn), lambda i, j: (i, j)),
        out_shape=jax.ShapeDtypeStruct((m, n), out_dtype),
        compiler_params=_params(("parallel", "parallel")),
        name=name,
    )(x, w3)


def _mm_resid(x, w3, resid, mod, layer, i_gate, *, tk, name):
    m, k = x.shape
    n = w3.shape[2]
    assert w3.shape[1] == k
    tm, tn = _tile(m, 1024), _tile(n, 1024)
    gate_blocks = n // tn
    tk = _tile(k, tk)
    if tk == k:
        return pl.pallas_call(
            _mm_resid_kernel,
            grid=(m // tm, n // tn),
            in_specs=[
                pl.BlockSpec((tm, k), lambda i, j: (i, 0)),
                pl.BlockSpec((None, k, tn), lambda i, j: (layer, 0, j)),
                pl.BlockSpec((tm, tn), lambda i, j: (i, j)),
                pl.BlockSpec((None, 1, tn), lambda i, j: (layer, 0, i_gate * gate_blocks + j)),
            ],
            out_specs=pl.BlockSpec((tm, tn), lambda i, j: (i, j)),
            out_shape=jax.ShapeDtypeStruct((m, n), F32),
            compiler_params=_params(("parallel", "parallel")),
            name=name,
        )(x, w3, resid, mod)
    return pl.pallas_call(
        _mm_resid_acc_kernel,
        grid=(m // tm, n // tn, k // tk),
        in_specs=[
            pl.BlockSpec((tm, tk), lambda i, j, kk: (i, kk)),
            pl.BlockSpec((None, tk, tn), lambda i, j, kk: (layer, kk, j)),
            pl.BlockSpec((tm, tn), lambda i, j, kk: (i, j), pipeline_mode=pl.Buffered(1)),
            pl.BlockSpec((None, 1, tn), lambda i, j, kk: (layer, 0, i_gate * gate_blocks + j)),
        ],
        out_specs=pl.BlockSpec((tm, tn), lambda i, j, kk: (i, j)),
        out_shape=jax.ShapeDtypeStruct((m, n), F32),
        scratch_shapes=[pltpu.VMEM((tm, tn), F32)],
        compiler_params=_params(("parallel", "parallel", "arbitrary")),
        name=name,
    )(x, w3, resid, mod)


def _merge_kernel(h_ref, og_ref, oa_ref, wga_ref, wgb_ref, bga_ref, bgb_ref, wa_ref, wb_ref, o_ref):
    h = h_ref[...]
    ga = _sigmoid(_dot(h, wga_ref[...]) + bga_ref[...])
    gb = _sigmoid(_dot(h, wgb_ref[...]) + bgb_ref[...])
    a = _dot(og_ref[...], wa_ref[...])
    b = _dot(oa_ref[...], wb_ref[...])
    o_ref[...] = (ga * a + gb * b).astype(o_ref.dtype)


def _merge(h, o_gla, o_attn, w_gate, b_gate2, w_a, w_b):
    m, d = h.shape
    tm, tn = _tile(m, 512), _tile(d, 512)
    nb = d // tn
    x_spec = lambda kdim: pl.BlockSpec((tm, kdim), lambda j, i: (i, 0))
    w_spec = lambda kdim, off: pl.BlockSpec((kdim, tn), lambda j, i: (0, j + off))
    b_spec = lambda off: pl.BlockSpec((1, tn), lambda j, i: (0, j + off))
    return pl.pallas_call(
        _merge_kernel,
        grid=(nb, m // tm),
        in_specs=[
            x_spec(d), x_spec(o_gla.shape[1]), x_spec(o_attn.shape[1]),
            w_spec(d, 0), w_spec(d, nb), b_spec(0), b_spec(nb),
            w_spec(w_a.shape[0], 0), w_spec(w_b.shape[0], 0),
        ],
        out_specs=pl.BlockSpec((tm, tn), lambda j, i: (i, j)),
        out_shape=jax.ShapeDtypeStruct((m, d), BF16),
        compiler_params=_params(("parallel", "parallel")),
        name="gated_merge",
    )(h, o_gla, o_attn, w_gate, w_gate, b_gate2, b_gate2, w_a, w_b)


def _split_bf16(x, pieces):
    out = []
    for _ in range(pieces - 1):
        p = x.astype(BF16)
        out.append(p)
        x = x - p.astype(F32)
    out.append(x.astype(BF16))
    return out


def _gate_kernel(h_ref, wl_ref, wc_ref, ba_ref, b_ref, *, chunk, tau, rank):
    tm = h_ref.shape[0]
    x = _dot(h_ref[...], wl_ref[...])
    hi = x.astype(BF16).astype(F32)
    lane = lax.broadcasted_iota(jnp.int32, x.shape, 1)
    a = jnp.where(lane < 2 * rank, hi, x - hi).astype(BF16)
    logits = _dot(a, wc_ref[...]) + ba_ref[...]
    g = (jnp.minimum(logits, 0.0) - jnp.log(1.0 + jnp.exp(-jnp.abs(logits)))) * (LOG2_E / tau)
    ri = lax.broadcasted_iota(jnp.int32, (tm, tm), 0)
    ci = lax.broadcasted_iota(jnp.int32, (tm, tm), 1)
    tril = jnp.where(jnp.logical_and(ri >= ci, (ri ^ ci) < chunk), 1.0, 0.0).astype(BF16)
    p1, p2, p3 = _split_bf16(g, 3)
    b_ref[...] = _dot(tril, p1) + _dot(tril, p2) + _dot(tril, p3)


def _gla_gate(h, w_glr3, w_alpha_cat, b_alpha2, chunk, cfg):
    s, d = h.shape
    n = w_alpha_cat.shape[1]
    tm = _tile(s, max(256, chunk))
    assert tm % chunk == 0 and chunk & (chunk - 1) == 0 and 3 * cfg.gate_rank <= LANES
    return pl.pallas_call(
        functools.partial(_gate_kernel, chunk=chunk, tau=cfg.gate_tau, rank=cfg.gate_rank),
        grid=(s // tm,),
        in_specs=[
            pl.BlockSpec((tm, d), lambda i: (i, 0)),
            pl.BlockSpec((d, LANES), lambda i: (0, 0)),
            pl.BlockSpec((LANES, n), lambda i: (0, 0)),
            pl.BlockSpec((1, n), lambda i: (0, 0)),
        ],
        out_specs=pl.BlockSpec((tm, n), lambda i: (i, 0)),
        out_shape=jax.ShapeDtypeStruct((s, n), F32),
        compiler_params=_params(("parallel",)),
        name="gla_gate",
    )(h, w_glr3, w_alpha_cat, b_alpha2)


def _gla_block_kernel(q_ref, k_ref, v_ref, gr_ref, b_ref, gn_ref, o_ref, st_ref, *, heads, q_scale, eps):
    rows = q_ref.shape[0]
    dk = q_ref.shape[1] // heads
    dv = v_ref.shape[1] // heads
    sub = SUBLANES
    half = rows // 2

    @pl.when(pl.program_id(1) == 0)
    def _():
        st_ref[...] = jnp.zeros_like(st_ref)

    ri = lax.broadcasted_iota(jnp.int32, (half, half), 0)
    ci = lax.broadcasted_iota(jnp.int32, (half, half), 1)
    sub_i = lax.broadcasted_iota(jnp.int32, (sub, half), 0)
    lane_i = lax.broadcasted_iota(jnp.int32, (sub, half), 1)

    def cross_scores(q, k, b, g_blk):
        n = q.shape[0]
        span = 2 * g_blk
        zero = jnp.zeros((g_blk, dk), F32)
        q_parts, k_parts = [], []
        for s0 in range(0, n, span):
            ref = jnp.broadcast_to(b[s0 + g_blk - 1:s0 + g_blk, :], (g_blk, dk))
            lo, up = slice(s0, s0 + g_blk), slice(s0 + g_blk, s0 + span)
            q_parts += [zero, q[up] * jnp.exp2(b[up] - ref)]
            k_parts += [k[lo] * jnp.exp2(ref - b[lo]), zero]
        qg = jnp.concatenate(q_parts, axis=0).astype(BF16)
        kg = jnp.concatenate(k_parts, axis=0).astype(BF16)
        return _dot_nt(qg, kg)

    for hh in range(heads):
        ksl = slice(hh * dk, (hh + 1) * dk)
        vsl = slice(hh * dv, (hh + 1) * dv)

        q = q_ref[:, ksl].astype(F32) * q_scale
        k = k_ref[:, ksl].astype(F32)
        vb = v_ref[:, vsl]
        b = b_ref[:, ksl]
        b_last = b[rows - 1:rows, :]

        st = st_ref[hh]
        qe = (q * jnp.exp2(b)).astype(BF16)
        o = _dot_nt(qe, st.astype(BF16))

        b_mid = jnp.broadcast_to(b[half - 1:half, :], (half, dk))
        x_lo = _dot_nt((q[half:] * jnp.exp2(b[half:] - b_mid)).astype(BF16),
                       (k[:half] * jnp.exp2(b_mid - b[:half])).astype(BF16))
        a_halves = []
        for hf in range(2):
            rs = slice(hf * half, (hf + 1) * half)
            qh, kh, bh = q[rs], k[rs], b[rs]
            tiles = []
            for lo in range(0, half, sub):
                qi, ki, bi = qh[lo:lo + sub, :], kh[lo:lo + sub, :], bh[lo:lo + sub, :]
                acc = jnp.zeros((sub, half), F32)
                for jj in range(sub):
                    t = qi * ki[jj:jj + 1, :] * jnp.exp2(bi - bi[jj:jj + 1, :])
                    col = jnp.sum(t, axis=1, keepdims=True)
                    keep = jnp.logical_and(lane_i == lo + jj, sub_i >= jj)
                    acc = jnp.where(keep, col, acc)
                tiles.append(acc)
            a = jnp.concatenate(tiles, axis=0) + cross_scores(qh, kh, bh, half // 2)
            g_blk = half // 4
            while g_blk >= sub:
                same = (ri ^ ci) < 2 * g_blk
                a = a + jnp.where(same, cross_scores(qh, kh, bh, g_blk), 0.0)
                g_blk //= 2
            a_halves.append(a)
        top = jnp.concatenate([a_halves[0], jnp.zeros((half, half), F32)], axis=1)
        bot = jnp.concatenate([x_lo, a_halves[1]], axis=1)
        s = jnp.concatenate([top, bot], axis=0).astype(BF16)
        o = o + _dot(s, vb)

        kd = (k * jnp.exp2(b_last - b)).astype(BF16)
        st_ref[hh] = st * jnp.exp2(b_last) + _dot_tn(vb, kd)

        ms = jnp.mean(o * o, axis=-1, keepdims=True)
        y = o * lax.rsqrt(ms + eps) * gn_ref[...]
        gr = gr_ref[:, vsl].astype(F32)
        o_ref[:, vsl] = (y * (gr * _sigmoid(gr))).astype(o_ref.dtype)


def _gla_block(proj, b_cum, gla_norm2, rows, cfg, heads_per_step=2):
    s = proj.shape[0]
    nh, dk, dv = cfg.gla_heads, cfg.dk_head, cfg.dv_head
    hp = heads_per_step
    assert s % rows == 0 and nh % hp == 0 and rows & (rows - 1) == 0 and rows >= 8 * SUBLANES
    assert cfg.gla_dk % (hp * dk) == 0 and (2 * cfg.gla_dk) % (hp * dv) == 0
    k_off = cfg.gla_dk // (hp * dk)
    v_off = (2 * cfg.gla_dk) // (hp * dv)
    r_off = (2 * cfg.gla_dk + cfg.gla_dv) // (hp * dv)
    return pl.pallas_call(
        functools.partial(_gla_block_kernel, heads=hp, q_scale=float(dk) ** -0.5, eps=cfg.norm_eps),
        grid=(nh // hp, s // rows),
        in_specs=[
            pl.BlockSpec((rows, hp * dk), lambda h, t: (t, h)),
            pl.BlockSpec((rows, hp * dk), lambda h, t: (t, k_off + h)),
            pl.BlockSpec((rows, hp * dv), lambda h, t: (t, v_off + h)),
            pl.BlockSpec((rows, hp * dv), lambda h, t: (t, r_off + h)),
            pl.BlockSpec((rows, hp * dk), lambda h, t: (t, h)),
            pl.BlockSpec((1, dv), lambda h, t: (0, 0)),
        ],
        out_specs=pl.BlockSpec((rows, hp * dv), lambda h, t: (t, h)),
        out_shape=jax.ShapeDtypeStruct((s, cfg.gla_dv), BF16),
        scratch_shapes=[pltpu.VMEM((hp, dv, dk), F32)],
        compiler_params=_params(("parallel", "arbitrary")),
        name="gla",
    )(proj, proj, proj, proj, b_cum, gla_norm2)


def _attn_kernel(q_ref, kp_ref, kc_ref, vp_ref, vc_ref, o_ref, lse_ref, s_scr, p_scr, *, heads, hd, blk):
    nq = q_ref.shape[0] // blk
    n = pl.program_id(1)
    qi = lax.broadcasted_iota(jnp.int32, (blk, 2 * blk), 0)
    kj = lax.broadcasted_iota(jnp.int32, (blk, 2 * blk), 1)
    dist = blk + qi - kj
    band = jnp.logical_and(dist >= 0, dist <= blk)
    first = jnp.logical_and(band, jnp.logical_or(kj >= blk, n > 0))
    lane = lax.broadcasted_iota(jnp.int32, (blk, LANES), 1)
    for b in range(nq):
        rows = slice(b * blk, (b + 1) * blk)
        keys = slice((b - 1) * blk, (b + 1) * blk)
        mask = first if b == 0 else band
        slot = b % 2
        for hh in range(heads):
            sl = slice(hh * hd, (hh + 1) * hd)
            q = q_ref[rows, sl]
            if b == 0:
                s = jnp.concatenate([_dot_nt(q, kp_ref[:, sl]), _dot_nt(q, kc_ref[rows, sl])], axis=1)
            else:
                s = _dot_nt(q, kc_ref[keys, sl])
            s_scr[slot, hh] = jnp.where(mask, s, -jnp.inf)
        lse_tile = jnp.zeros((blk, LANES), F32)
        for hh in range(heads):
            s = s_scr[slot, hh]
            m = jnp.max(s, axis=1, keepdims=True)
            p = jnp.exp(s - m)
            den = jnp.sum(p, axis=1, keepdims=True)
            p_scr[slot, hh] = (p * (1.0 / den)).astype(BF16)
            lse_tile = jnp.where(lane == hh, m + jnp.log(den), lse_tile)
        lse_ref[rows, :] = lse_tile
        for hh in range(heads):
            sl = slice(hh * hd, (hh + 1) * hd)
            p = p_scr[slot, hh]
            if b == 0:
                o = _dot(p[:, :blk], vp_ref[:, sl]) + _dot(p[:, blk:], vc_ref[rows, sl])
            else:
                o = _dot(p, vc_ref[keys, sl])
            o_ref[rows, sl] = o.astype(o_ref.dtype)


def _attn_group(qkv, window, dilation, cfg, name, blocks_per_step=4):
    dil, u, w3 = qkv.shape
    gw = cfg.group_width
    heads = cfg.attn_heads_per_group
    blk = window // dilation
    assert dil == dilation and w3 == 3 * gw and u % blk == 0 and heads <= LANES
    nb = u // blk
    nq = blocks_per_step
    while nb % nq:
        nq //= 2
    rows = nq * blk

    def cur(part):
        return pl.BlockSpec((None, rows, gw), lambda r, n: (r, n, part))

    def prev(part):
        return pl.BlockSpec((None, blk, gw), lambda r, n: (r, jnp.maximum(n * nq - 1, 0), part))

    return pl.pallas_call(
        functools.partial(_attn_kernel, heads=heads, hd=cfg.attn_head_dim, blk=blk),
        grid=(dilation, nb // nq),
        in_specs=[cur(0), prev(1), cur(1), prev(2), cur(2)],
        out_specs=[pl.BlockSpec((None, rows, gw), lambda r, n: (r, n, 0)),
                   pl.BlockSpec((None, rows, LANES), lambda r, n: (r, n, 0))],
        out_shape=[jax.ShapeDtypeStruct((dilation, u, gw), BF16),
                   jax.ShapeDtypeStruct((dilation, u, LANES), F32)],
        scratch_shapes=[pltpu.VMEM((2, heads, blk, 2 * blk), F32), pltpu.VMEM((2, heads, blk, 2 * blk), BF16)],
        compiler_params=_params(("parallel", "arbitrary")),
        name=name,
    )(qkv, qkv, qkv, qkv, qkv)


def _combine_kernel(*refs, dilations, hd):
    ng = len(dilations)
    o_refs, l_refs, out_ref = refs[:ng], refs[ng:2 * ng], refs[2 * ng]
    ts, gw = out_ref.shape
    perms = [_perm_matrix(ts, dil, BF16) if dil > 1 else None for dil in dilations]
    ls = []
    for gi in range(ng):
        l_g = l_refs[gi][...].reshape(ts, LANES)
        if perms[gi] is not None:
            l_g = functools.reduce(lambda a, b: a + b, [_dot(perms[gi], p) for p in _split_bf16(l_g, 3)])
        ls.append(l_g)
    m = functools.reduce(jnp.maximum, ls)
    es = [jnp.exp(l - m) for l in ls]
    inv = 1.0 / functools.reduce(lambda a, b: a + b, es)
    ws = [e * inv for e in es]
    width = 2 * hd if gw % (2 * hd) == 0 else hd
    for c0 in range(0, gw, width):
        o_c = []
        for gi in range(ng):
            o_g = o_refs[gi][:, :, c0:c0 + width].reshape(ts, width)
            o_c.append(_dot(perms[gi], o_g) if perms[gi] is not None else o_g.astype(F32))
        for c in range(c0, c0 + width, hd):
            hh = c // hd
            acc = None
            for gi in range(ng):
                term = ws[gi][:, hh:hh + 1] * o_c[gi][:, c - c0:c - c0 + hd]
                acc = term if acc is None else acc + term
            out_ref[:, c:c + hd] = acc.astype(out_ref.dtype)


def _combine(outs, lses, dilations, cfg):
    gw = cfg.group_width
    s = outs[0].shape[0] * outs[0].shape[1]
    ts = _tile(s, 512)
    in_specs = []
    for dil in dilations:
        assert ts % (dil * 2 * SUBLANES) == 0
        in_specs.append(pl.BlockSpec((dil, ts // dil, gw), lambda i: (0, i, 0)))
    for dil in dilations:
        in_specs.append(pl.BlockSpec((dil, ts // dil, LANES), lambda i: (0, i, 0)))
    return pl.pallas_call(
        functools.partial(_combine_kernel, dilations=tuple(dilations), hd=cfg.attn_head_dim),
        grid=(s // ts,),
        in_specs=in_specs,
        out_specs=pl.BlockSpec((ts, gw), lambda i: (i, 0)),
        out_shape=jax.ShapeDtypeStruct((s, gw), BF16),
        compiler_params=_params(("parallel",)),
        name="attn_combine",
    )(*outs, *lses)


def _forward(cfg, x, c, positions, ada_w, ada_b, norm_mix, norm_mlp, w_in, w_alpha_up, b_alpha,
             gla_norm, w_branch_gla, w_branch_attn, w_gate, b_gate, w_out, w_mlp_in, w_mlp_out,
             norm_final):
    bsz, s, d = x.shape
    assert bsz == 1 and d == cfg.d_model
    depth = cfg.depth
    x2 = x.reshape(s, d)
    mod = _ada_mod(c.reshape(d, 1), ada_w, ada_b.reshape(depth, 1, -1))
    dils = [dil for _, dil in cfg.dilated_groups]
    perm_dils = sorted({dil for dil in dils if dil > 1})
    pos = positions.reshape(s)
    pos_all = jnp.concatenate([pos.reshape(s // dil, dil).T.reshape(s) for dil in dils]).reshape(-1, 1)
    cos_t, sin_t = _rope_tables(pos_all, cfg)
    g_mix = norm_mix.reshape(depth, 1, d)
    g_mlp = norm_mlp.reshape(depth, 1, d)
    gw, rank = cfg.gla_width, cfg.gate_rank
    w_in_b = w_in.astype(BF16)
    w_out_b = w_out.astype(BF16)
    w_mlp_in_b = w_mlp_in.astype(BF16)
    w_mlp_out_b = w_mlp_out.astype(BF16)

    for l in range(depth):
        w_attn = w_in_b[l, :, gw + rank:]
        w_glr = w_in[l, :, gw:gw + rank]
        w_glr3 = jnp.pad(jnp.concatenate([w_glr] * 3, axis=1), ((0, 0), (0, LANES - 3 * rank))).astype(BF16)
        wa_hi, wa_lo = _split_bf16(w_alpha_up[l], 2)
        w_alpha_cat = jnp.pad(jnp.concatenate([wa_hi, wa_lo, wa_hi], axis=0), ((0, LANES - 3 * rank), (0, 0)))

        hs = _norm_mod(x2, g_mix, mod, l, 0, 1, cfg, dilations=perm_dils)
        h = hs[0]
        h_by_dil = {1: h, **{dil: hp.reshape(s, d) for dil, hp in zip(perm_dils, hs[1:])}}
        proj = _mm(_mm_plain_kernel, h, w_in_b, l, gw, BF16, name="gla_proj")
        b_cum = _gla_gate(h, w_glr3, w_alpha_cat, b_alpha[l].reshape(1, -1), GLA_BLOCK, cfg)
        o_gla = _gla_block(proj, b_cum, gla_norm[l].reshape(1, -1), GLA_BLOCK, cfg)
        outs, lses = [], []
        for g, (window, dil) in enumerate(cfg.dilated_groups):
            qkv = _attn_proj(h_by_dil[dil], w_attn, cos_t, sin_t, g, cfg)
            o_g, lse_g = _attn_group(qkv.reshape(dil, s // dil, -1), window, dil, cfg, f"dilated_attn_{g}")
            outs.append(o_g)
            lses.append(lse_g)
        o_attn = _combine(outs, lses, dils, cfg)
        merged = _merge(h, o_gla, o_attn, w_gate[l].astype(BF16), b_gate[l].reshape(1, -1),
                        w_branch_gla[l].astype(BF16), w_branch_attn[l].astype(BF16))
        x2 = _mm_resid(merged, w_out_b, x2, mod, l, 2, tk=4096, name="out_proj")

        h = _norm_mod(x2, g_mlp, mod, l, 3, 4, cfg)[0]
        a = _mm(_mm_relu2_kernel, h, w_mlp_in_b, l, w_mlp_in_b.shape[2], BF16, name="mlp_in")
        x2 = _mm_resid(a, w_mlp_out_b, x2, mod, l, 5, tk=4096, name="mlp_out")

    return _final_norm(x2, norm_final.reshape(1, d), cfg).reshape(bsz, s, d)


def kernel(x, c, positions, ada_w, ada_b, norm_mix, norm_mlp, w_in, w_alpha_up, b_alpha, gla_norm,
           w_branch_gla, w_branch_attn, w_gate, b_gate, w_out, w_mlp_in, w_mlp_out, norm_final):
    return _forward(Config(), x, c, positions, ada_w, ada_b, norm_mix, norm_mlp, w_in, w_alpha_up,
                    b_alpha, gla_norm, w_branch_gla, w_branch_attn, w_gate, b_gate, w_out, w_mlp_in,
                    w_mlp_out, norm_final)
```

```python
import functools
from typing import NamedTuple

import jax
import jax.numpy as jnp
from jax import lax
from jax.experimental import pallas as pl
from jax.experimental.pallas import tpu as pltpu

F32 = jnp.float32
BF16 = jnp.bfloat16
LANES = 128
SUBLANES = 8
VMEM_LIMIT = 56 * 1024 * 1024
GLA_BLOCK = 256
LOG2_E = 1.4426950408889634


class Config(NamedTuple):
    d_model: int = 4096
    depth: int = 2
    gla_heads: int = 4
    gate_rank: int = 16
    gate_tau: float = 16.0
    gla_chunk: int = 64
    attn_head_dim: int = 128
    attn_heads_per_group: int = 16
    dilated_groups: tuple = ((128, 1), (512, 4), (2048, 16))
    rope_theta: float = 10000.0
    ff_mult: int = 4
    n_mod: int = 6
    norm_eps: float = 1e-6

    @property
    def gla_dk(self):
        return self.d_model // 2

    @property
    def gla_dv(self):
        return self.d_model

    @property
    def dk_head(self):
        return self.gla_dk // self.gla_heads

    @property
    def dv_head(self):
        return self.gla_dv // self.gla_heads

    @property
    def group_width(self):
        return self.attn_heads_per_group * self.attn_head_dim

    @property
    def attn_width(self):
        return len(self.dilated_groups) * self.group_width

    @property
    def gla_width(self):
        return 2 * self.gla_dk + 2 * self.gla_dv


def _params(sem):
    return pltpu.CompilerParams(dimension_semantics=sem, vmem_limit_bytes=VMEM_LIMIT)


def _sigmoid(x):
    return 1.0 / (1.0 + jnp.exp(-x))


def _dot(a, b):
    return jnp.dot(a, b, preferred_element_type=F32)


def _dot_nt(a, b):
    return lax.dot_general(a, b, (((1,), (1,)), ((), ())), preferred_element_type=F32)


def _dot_tn(a, b):
    return lax.dot_general(a, b, (((0,), (0,)), ((), ())), preferred_element_type=F32)


def _dot_f32(a, b):
    return jnp.dot(a, b, preferred_element_type=F32, precision=lax.Precision.HIGHEST)


def _tile(n, pref):
    t = min(n, pref)
    assert n % t == 0, (n, pref)
    return t


def _perm_matrix(size, inner, dtype):
    outer = size // inner
    assert inner * outer == size and inner & (inner - 1) == 0 and outer & (outer - 1) == 0
    i = lax.broadcasted_iota(jnp.int32, (size, size), 0)
    j = lax.broadcasted_iota(jnp.int32, (size, size), 1)
    src = (i & (inner - 1)) * outer + (i >> (inner.bit_length() - 1))
    return (j == src).astype(dtype)


def _ada_kernel(c_ref, w_ref, b_ref, o_ref, *, rows):
    d, tn = w_ref.shape

    def body(i, acc):
        r = pl.multiple_of(i * rows, rows)
        cc = c_ref[pl.ds(r, rows), :]
        cond = cc * _sigmoid(cc)
        p = w_ref[pl.ds(r, rows), :] * cond
        return acc + jnp.sum(p.reshape(rows // SUBLANES, SUBLANES, tn), axis=0)

    acc = lax.fori_loop(0, d // rows, body, jnp.zeros((SUBLANES, tn), F32))
    o_ref[...] = jnp.sum(acc, axis=0, keepdims=True) + b_ref[...]


def _ada_mod(c_col, ada_w, ada_b3):
    depth, d, n = ada_w.shape
    tn = _tile(n, 512)
    rows = _tile(d, 256)
    return pl.pallas_call(
        functools.partial(_ada_kernel, rows=rows),
        grid=(depth, n // tn),
        in_specs=[
            pl.BlockSpec((d, 1), lambda l, j: (0, 0)),
            pl.BlockSpec((None, d, tn), lambda l, j: (l, 0, j)),
            pl.BlockSpec((None, 1, tn), lambda l, j: (l, 0, j)),
        ],
        out_specs=pl.BlockSpec((None, 1, tn), lambda l, j: (l, 0, j)),
        out_shape=jax.ShapeDtypeStruct((depth, 1, n), F32),
        compiler_params=_params(("parallel", "parallel")),
        name="ada_mod",
    )(c_col, ada_w, ada_b3)


def _rope_kernel(p_ref, inv_ref, sgn_ref, c_ref, s_ref):
    ang = p_ref[...].astype(F32) * inv_ref[...]
    c_ref[...] = jnp.cos(ang)
    s_ref[...] = jnp.sin(ang) * sgn_ref[...]


def _rope_tables(pos_col, cfg):
    s = pos_col.shape[0]
    hd = cfg.attn_head_dim
    half = hd // 2
    inv = cfg.rope_theta ** (-jnp.arange(half, dtype=F32) / half)
    inv2 = jnp.concatenate([inv, inv]).reshape(1, hd)
    sgn = jnp.concatenate([-jnp.ones((half,), F32), jnp.ones((half,), F32)]).reshape(1, hd)
    ts = _tile(s, 1024)
    spec_t = pl.BlockSpec((ts, hd), lambda i: (i, 0))
    spec_c = pl.BlockSpec((1, hd), lambda i: (0, 0))
    return pl.pallas_call(
        _rope_kernel,
        grid=(s // ts,),
        in_specs=[pl.BlockSpec((ts, 1), lambda i: (i, 0)), spec_c, spec_c],
        out_specs=[spec_t, spec_t],
        out_shape=[jax.ShapeDtypeStruct((s, hd), F32)] * 2,
        compiler_params=_params(("parallel",)),
        name="rope_tables",
    )(pos_col, inv2, sgn)


def _norm_mod_kernel(x_ref, g_ref, sc_ref, sh_ref, o_ref, *rest, eps, dilations):
    x = x_ref[...]
    ms = jnp.mean(x * x, axis=-1, keepdims=True)
    y = x * lax.rsqrt(ms + eps) * g_ref[...]
    hb = (y * (1.0 + sc_ref[...]) + sh_ref[...]).astype(o_ref.dtype)
    o_ref[...] = hb
    ts = hb.shape[0]
    for p_ref, dil in zip(rest, dilations):
        n = ts // dil
        hp = _dot(_perm_matrix(ts, n, hb.dtype), hb).astype(p_ref.dtype)
        for r in range(dil):
            p_ref[r] = hp[r * n:(r + 1) * n, :]


def _norm_mod(x, g3, mod, layer, i_shift, i_scale, cfg, dilations=()):
    s, d = x.shape
    ts = _tile(s, 256)
    out_specs = [pl.BlockSpec((ts, d), lambda i: (i, 0))]
    out_shape = [jax.ShapeDtypeStruct((s, d), BF16)]
    for dil in dilations:
        assert ts % (dil * 2 * SUBLANES) == 0
        out_specs.append(pl.BlockSpec((dil, ts // dil, d), lambda i: (0, i, 0)))
        out_shape.append(jax.ShapeDtypeStruct((dil, s // dil, d), BF16))
    return pl.pallas_call(
        functools.partial(_norm_mod_kernel, eps=cfg.norm_eps, dilations=tuple(dilations)),
        grid=(s // ts,),
        in_specs=[
            pl.BlockSpec((ts, d), lambda i: (i, 0)),
            pl.BlockSpec((None, 1, d), lambda i: (layer, 0, 0)),
            pl.BlockSpec((None, 1, d), lambda i: (layer, 0, i_scale)),
            pl.BlockSpec((None, 1, d), lambda i: (layer, 0, i_shift)),
        ],
        out_specs=out_specs,
        out_shape=out_shape,
        compiler_params=_params(("parallel",)),
        name="norm_mod",
    )(x, g3, mod, mod)


def _final_norm_kernel(x_ref, g_ref, o_ref, *, eps):
    x = x_ref[...]
    ms = jnp.mean(x * x, axis=-1, keepdims=True)
    o_ref[...] = x * lax.rsqrt(ms + eps) * g_ref[...]


def _final_norm(x, g2, cfg):
    s, d = x.shape
    ts = _tile(s, 256)
    return pl.pallas_call(
        functools.partial(_final_norm_kernel, eps=cfg.norm_eps),
        grid=(s // ts,),
        in_specs=[pl.BlockSpec((ts, d), lambda i: (i, 0)), pl.BlockSpec((1, d), lambda i: (0, 0))],
        out_specs=pl.BlockSpec((ts, d), lambda i: (i, 0)),
        out_shape=jax.ShapeDtypeStruct((s, d), F32),
        compiler_params=_params(("parallel",)),
        name="final_norm",
    )(x, g2)


def _proj_kernel(x_ref, wt_ref, cos_ref, sin_ref, o_ref, *, j_rope0, j_rope_k, j_rope1, q_scale, hd):
    acc = _dot_nt(x_ref[...], wt_ref[...])
    j = pl.program_id(1)
    is_rope = jnp.logical_and(j >= j_rope0, j < j_rope1)
    scale = jnp.where(j < j_rope_k, q_scale, 1.0).astype(F32)
    cos = jnp.where(is_rope, cos_ref[...] * scale, 1.0)
    sin = jnp.where(is_rope, sin_ref[...] * scale, 0.0)
    for c in range(acc.shape[1] // hd):
        a = acc[:, c * hd:(c + 1) * hd]
        o_ref[:, c * hd:(c + 1) * hd] = (a * cos + pltpu.roll(a, hd // 2, 1) * sin).astype(o_ref.dtype)


def _attn_proj(h, wt_attn, cos_t, sin_t, g, cfg):
    m, k = h.shape
    hd = cfg.attn_head_dim
    gw = cfg.group_width
    tm = _tile(m, 1024)
    tn = _tile(gw, 1024)
    per = gw // tn
    sect = cfg.attn_width // tn
    tbl0 = g * (m // tm)
    return pl.pallas_call(
        functools.partial(_proj_kernel, j_rope0=0, j_rope_k=per, j_rope1=2 * per,
                          q_scale=float(hd) ** -0.5, hd=hd),
        grid=(m // tm, 3 * per),
        in_specs=[
            pl.BlockSpec((tm, k), lambda i, j: (i, 0)),
            pl.BlockSpec((tn, k), lambda i, j: ((j // per) * sect + g * per + j % per, 0)),
            pl.BlockSpec((tm, hd), lambda i, j: (tbl0 + i, 0)),
            pl.BlockSpec((tm, hd), lambda i, j: (tbl0 + i, 0)),
        ],
        out_specs=pl.BlockSpec((tm, tn), lambda i, j: (i, j)),
        out_shape=jax.ShapeDtypeStruct((m, 3 * gw), BF16),
        compiler_params=_params(("parallel", "parallel")),
        name=f"attn_proj_{g}",
    )(h, wt_attn, cos_t, sin_t)


def _mm_plain_kernel(x_ref, w_ref, o_ref):
    o_ref[...] = _dot(x_ref[...], w_ref[...]).astype(o_ref.dtype)


def _mm_plain_nt_kernel(x_ref, wt_ref, o_ref):
    o_ref[...] = _dot_nt(x_ref[...], wt_ref[...]).astype(o_ref.dtype)


def _mm_nt(x, wt3, layer, row0, n, out_dtype, *, tm=1024, tn=1024, name):
    m, k = x.shape
    assert wt3.shape[2] == k and row0 % tn == 0 and row0 + n <= wt3.shape[1]
    tm, tn = _tile(m, tm), _tile(n, tn)
    j0 = row0 // tn
    return pl.pallas_call(
        _mm_plain_nt_kernel,
        grid=(m // tm, n // tn),
        in_specs=[pl.BlockSpec((tm, k), lambda i, j: (i, 0)),
                  pl.BlockSpec((None, tn, k), lambda i, j: (layer, j0 + j, 0))],
        out_specs=pl.BlockSpec((tm, tn), lambda i, j: (i, j)),
        out_shape=jax.ShapeDtypeStruct((m, n), out_dtype),
        compiler_params=_params(("parallel", "parallel")),
        name=name,
    )(x, wt3)


def _mm_relu2_kernel(x_ref, w_ref, o_ref):
    a = jnp.maximum(_dot(x_ref[...], w_ref[...]), 0.0)
    o_ref[...] = (a * a).astype(o_ref.dtype)


def _mm_resid_kernel(x_ref, w_ref, r_ref, g_ref, o_ref):
    o_ref[...] = r_ref[...] + g_ref[...] * _dot(x_ref[...], w_ref[...])


def _mm_resid_acc_kernel(x_ref, w_ref, r_ref, g_ref, o_ref, acc_ref):
    kk = pl.program_id(2)

    @pl.when(kk == 0)
    def _():
        acc_ref[...] = jnp.zeros_like(acc_ref)

    acc_ref[...] += _dot(x_ref[...], w_ref[...])

    @pl.when(kk == pl.num_programs(2) - 1)
    def _():
        o_ref[...] = r_ref[...] + g_ref[...] * acc_ref[...]


def _mm(kernel_fn, x, w3, layer, n, out_dtype, *, tm=1024, tn=1024, name):
    m, k = x.shape
    assert w3.shape[1] == k and n <= w3.shape[2]
    tm, tn = _tile(m, tm), _tile(n, tn)
    return pl.pallas_call(
        kernel_fn,
        grid=(m // tm, n // tn),
        in_specs=[pl.BlockSpec((tm, k), lambda i, j: (i, 0)),
                  pl.BlockSpec((None, k, tn), lambda i, j: (layer, 0, j))],
        out_specs=pl.BlockSpec((tm, tn), lambda i, j: (i, j)),
        out_shape=jax.ShapeDtypeStruct((m, n), out_dtype),
        compiler_params=_params(("parallel", "parallel")),
        name=name,
    )(x, w3)


def _mm_resid(x, w3, resid, mod, layer, i_gate, *, tk, name):
    m, k = x.shape
    n = w3.shape[2]
    assert w3.shape[1] == k
    tm, tn = _tile(m, 1024), _tile(n, 1024)
    gate_blocks = n // tn
    tk = _tile(k, tk)
    if tk == k:
        return pl.pallas_call(
            _mm_resid_kernel,
            grid=(m // tm, n // tn),
            in_specs=[
                pl.BlockSpec((tm, k), lambda i, j: (i, 0)),
                pl.BlockSpec((None, k, tn), lambda i, j: (layer, 0, j)),
                pl.BlockSpec((tm, tn), lambda i, j: (i, j)),
                pl.BlockSpec((None, 1, tn), lambda i, j: (layer, 0, i_gate * gate_blocks + j)),
            ],
            out_specs=pl.BlockSpec((tm, tn), lambda i, j: (i, j)),
            out_shape=jax.ShapeDtypeStruct((m, n), F32),
            compiler_params=_params(("parallel", "parallel")),
            name=name,
        )(x, w3, resid, mod)
    return pl.pallas_call(
        _mm_resid_acc_kernel,
        grid=(m // tm, n // tn, k // tk),
        in_specs=[
            pl.BlockSpec((tm, tk), lambda i, j, kk: (i, kk)),
            pl.BlockSpec((None, tk, tn), lambda i, j, kk: (layer, kk, j)),
            pl.BlockSpec((tm, tn), lambda i, j, kk: (i, j)),
            pl.BlockSpec((None, 1, tn), lambda i, j, kk: (layer, 0, i_gate * gate_blocks + j)),
        ],
        out_specs=pl.BlockSpec((tm, tn), lambda i, j, kk: (i, j)),
        out_shape=jax.ShapeDtypeStruct((m, n), F32),
        scratch_shapes=[pltpu.VMEM((tm, tn), F32)],
        compiler_params=_params(("parallel", "parallel", "arbitrary")),
        name=name,
    )(x, w3, resid, mod)


def _merge_kernel(h_ref, og_ref, oa_ref, wga_ref, wgb_ref, bga_ref, bgb_ref, wa_ref, wb_ref, o_ref):
    h = h_ref[...]
    ga = _sigmoid(_dot(h, wga_ref[...]) + bga_ref[...])
    gb = _sigmoid(_dot(h, wgb_ref[...]) + bgb_ref[...])
    a = _dot(og_ref[...], wa_ref[...])
    b = _dot(oa_ref[...], wb_ref[...])
    o_ref[...] = (ga * a + gb * b).astype(o_ref.dtype)


def _merge(h, o_gla, o_attn, w_gate, b_gate2, w_a, w_b):
    m, d = h.shape
    tm, tn = _tile(m, 512), _tile(d, 512)
    nb = d // tn
    x_spec = lambda kdim: pl.BlockSpec((tm, kdim), lambda j, i: (i, 0))
    w_spec = lambda kdim, off: pl.BlockSpec((kdim, tn), lambda j, i: (0, j + off))
    b_spec = lambda off: pl.BlockSpec((1, tn), lambda j, i: (0, j + off))
    return pl.pallas_call(
        _merge_kernel,
        grid=(nb, m // tm),
        in_specs=[
            x_spec(d), x_spec(o_gla.shape[1]), x_spec(o_attn.shape[1]),
            w_spec(d, 0), w_spec(d, nb), b_spec(0), b_spec(nb),
            w_spec(w_a.shape[0], 0), w_spec(w_b.shape[0], 0),
        ],
        out_specs=pl.BlockSpec((tm, tn), lambda j, i: (i, j)),
        out_shape=jax.ShapeDtypeStruct((m, d), BF16),
        compiler_params=_params(("parallel", "parallel")),
        name="gated_merge",
    )(h, o_gla, o_attn, w_gate, w_gate, b_gate2, b_gate2, w_a, w_b)


def _split_bf16(x, pieces):
    out = []
    for _ in range(pieces - 1):
        p = x.astype(BF16)
        out.append(p)
        x = x - p.astype(F32)
    out.append(x.astype(BF16))
    return out


def _gate_kernel(h_ref, wl_ref, wc_ref, ba_ref, b_ref, *, chunk, tau, rank):
    tm = h_ref.shape[0]
    x = _dot(h_ref[...], wl_ref[...])
    hi = x.astype(BF16).astype(F32)
    lane = lax.broadcasted_iota(jnp.int32, x.shape, 1)
    a = jnp.where(lane < 2 * rank, hi, x - hi).astype(BF16)
    logits = _dot(a, wc_ref[...]) + ba_ref[...]
    g = (jnp.minimum(logits, 0.0) - jnp.log(1.0 + jnp.exp(-jnp.abs(logits)))) * (LOG2_E / tau)
    ri = lax.broadcasted_iota(jnp.int32, (tm, tm), 0)
    ci = lax.broadcasted_iota(jnp.int32, (tm, tm), 1)
    tril = jnp.where(jnp.logical_and(ri >= ci, (ri ^ ci) < chunk), 1.0, 0.0).astype(BF16)
    p1, p2, p3 = _split_bf16(g, 3)
    b_ref[...] = _dot(tril, p1) + _dot(tril, p2) + _dot(tril, p3)


def _gla_gate(h, w_glr3, w_alpha_cat, b_alpha2, chunk, cfg):
    s, d = h.shape
    n = w_alpha_cat.shape[1]
    tm = _tile(s, max(256, chunk))
    assert tm % chunk == 0 and chunk & (chunk - 1) == 0 and 3 * cfg.gate_rank <= LANES
    return pl.pallas_call(
        functools.partial(_gate_kernel, chunk=chunk, tau=cfg.gate_tau, rank=cfg.gate_rank),
        grid=(s // tm,),
        in_specs=[
            pl.BlockSpec((tm, d), lambda i: (i, 0)),
            pl.BlockSpec((d, LANES), lambda i: (0, 0)),
            pl.BlockSpec((LANES, n), lambda i: (0, 0)),
            pl.BlockSpec((1, n), lambda i: (0, 0)),
        ],
        out_specs=pl.BlockSpec((tm, n), lambda i: (i, 0)),
        out_shape=jax.ShapeDtypeStruct((s, n), F32),
        compiler_params=_params(("parallel",)),
        name="gla_gate",
    )(h, w_glr3, w_alpha_cat, b_alpha2)


def _gla_block_kernel(q_ref, k_ref, v_ref, gr_ref, b_ref, gn_ref, o_ref, st_ref, *, heads, q_scale, eps):
    rows = q_ref.shape[0]
    dk = q_ref.shape[1] // heads
    dv = v_ref.shape[1] // heads
    sub = SUBLANES
    half = rows // 2

    @pl.when(pl.program_id(1) == 0)
    def _():
        st_ref[...] = jnp.zeros_like(st_ref)

    ri = lax.broadcasted_iota(jnp.int32, (half, half), 0)
    ci = lax.broadcasted_iota(jnp.int32, (half, half), 1)
    sub_i = lax.broadcasted_iota(jnp.int32, (sub, half), 0)
    lane_i = lax.broadcasted_iota(jnp.int32, (sub, half), 1)

    def cross_scores(q, k, b, g_blk):
        n = q.shape[0]
        span = 2 * g_blk
        zero = jnp.zeros((g_blk, dk), F32)
        q_parts, k_parts = [], []
        for s0 in range(0, n, span):
            ref = jnp.broadcast_to(b[s0 + g_blk - 1:s0 + g_blk, :], (g_blk, dk))
            lo, up = slice(s0, s0 + g_blk), slice(s0 + g_blk, s0 + span)
            q_parts += [zero, q[up] * jnp.exp2(b[up] - ref)]
            k_parts += [k[lo] * jnp.exp2(ref - b[lo]), zero]
        qg = jnp.concatenate(q_parts, axis=0).astype(BF16)
        kg = jnp.concatenate(k_parts, axis=0).astype(BF16)
        return _dot_nt(qg, kg)

    for hh in range(heads):
        ksl = slice(hh * dk, (hh + 1) * dk)
        vsl = slice(hh * dv, (hh + 1) * dv)

        q = q_ref[:, ksl].astype(F32) * q_scale
        k = k_ref[:, ksl].astype(F32)
        vb = v_ref[:, vsl]
        b = b_ref[:, ksl]
        b_last = b[rows - 1:rows, :]

        st = st_ref[hh]
        qe = (q * jnp.exp2(b)).astype(BF16)
        o = _dot_nt(qe, st.astype(BF16))

        b_mid = jnp.broadcast_to(b[half - 1:half, :], (half, dk))
        x_lo = _dot_nt((q[half:] * jnp.exp2(b[half:] - b_mid)).astype(BF16),
                       (k[:half] * jnp.exp2(b_mid - b[:half])).astype(BF16))
        a_halves = []
        for hf in range(2):
            rs = slice(hf * half, (hf + 1) * half)
            qh, kh, bh = q[rs], k[rs], b[rs]
            tiles = []
            for lo in range(0, half, sub):
                qi, ki, bi = qh[lo:lo + sub, :], kh[lo:lo + sub, :], bh[lo:lo + sub, :]
                acc = jnp.zeros((sub, half), F32)
                for jj in range(sub):
                    t = qi * ki[jj:jj + 1, :] * jnp.exp2(bi - bi[jj:jj + 1, :])
                    col = jnp.sum(t, axis=1, keepdims=True)
                    keep = jnp.logical_and(lane_i == lo + jj, sub_i >= jj)
                    acc = jnp.where(keep, col, acc)
                tiles.append(acc)
            a = jnp.concatenate(tiles, axis=0) + cross_scores(qh, kh, bh, half // 2)
            g_blk = half // 4
            while g_blk >= sub:
                same = (ri ^ ci) < 2 * g_blk
                a = a + jnp.where(same, cross_scores(qh, kh, bh, g_blk), 0.0)
                g_blk //= 2
            a_halves.append(a)
        top = jnp.concatenate([a_halves[0], jnp.zeros((half, half), F32)], axis=1)
        bot = jnp.concatenate([x_lo, a_halves[1]], axis=1)
        s = jnp.concatenate([top, bot], axis=0).astype(BF16)
        o = o + _dot(s, vb)

        kd = (k * jnp.exp2(b_last - b)).astype(BF16)
        st_ref[hh] = st * jnp.exp2(b_last) + _dot_tn(vb, kd)

        ms = jnp.mean(o * o, axis=-1, keepdims=True)
        y = o * lax.rsqrt(ms + eps) * gn_ref[...]
        gr = gr_ref[:, vsl].astype(F32)
        o_ref[:, vsl] = (y * (gr * _sigmoid(gr))).astype(o_ref.dtype)


def _gla_block(proj, b_cum, gla_norm2, rows, cfg, heads_per_step=2):
    s = proj.shape[0]
    nh, dk, dv = cfg.gla_heads, cfg.dk_head, cfg.dv_head
    hp = heads_per_step
    assert s % rows == 0 and nh % hp == 0 and rows & (rows - 1) == 0 and rows >= 8 * SUBLANES
    assert cfg.gla_dk % (hp * dk) == 0 and (2 * cfg.gla_dk) % (hp * dv) == 0
    k_off = cfg.gla_dk // (hp * dk)
    v_off = (2 * cfg.gla_dk) // (hp * dv)
    r_off = (2 * cfg.gla_dk + cfg.gla_dv) // (hp * dv)
    return pl.pallas_call(
        functools.partial(_gla_block_kernel, heads=hp, q_scale=float(dk) ** -0.5, eps=cfg.norm_eps),
        grid=(nh // hp, s // rows),
        in_specs=[
            pl.BlockSpec((rows, hp * dk), lambda h, t: (t, h)),
            pl.BlockSpec((rows, hp * dk), lambda h, t: (t, k_off + h)),
            pl.BlockSpec((rows, hp * dv), lambda h, t: (t, v_off + h)),
            pl.BlockSpec((rows, hp * dv), lambda h, t: (t, r_off + h)),
            pl.BlockSpec((rows, hp * dk), lambda h, t: (t, h)),
            pl.BlockSpec((1, dv), lambda h, t: (0, 0)),
        ],
        out_specs=pl.BlockSpec((rows, hp * dv), lambda h, t: (t, h)),
        out_shape=jax.ShapeDtypeStruct((s, cfg.gla_dv), BF16),
        scratch_shapes=[pltpu.VMEM((hp, dv, dk), F32)],
        compiler_params=_params(("parallel", "arbitrary")),
        name="gla",
    )(proj, proj, proj, proj, b_cum, gla_norm2)


def _attn_kernel(q_ref, kp_ref, kc_ref, vp_ref, vc_ref, o_ref, lse_ref, s_scr, p_scr, *, heads, hd, blk):
    nq = q_ref.shape[0] // blk
    n = pl.program_id(1)
    qi = lax.broadcasted_iota(jnp.int32, (blk, 2 * blk), 0)
    kj = lax.broadcasted_iota(jnp.int32, (blk, 2 * blk), 1)
    dist = blk + qi - kj
    band = jnp.logical_and(dist >= 0, dist <= blk)
    first = jnp.logical_and(band, jnp.logical_or(kj >= blk, n > 0))
    lane = lax.broadcasted_iota(jnp.int32, (blk, LANES), 1)
    for b in range(nq):
        rows = slice(b * blk, (b + 1) * blk)
        keys = slice((b - 1) * blk, (b + 1) * blk)
        mask = first if b == 0 else band
        slot = b % 2
        for hh in range(heads):
            sl = slice(hh * hd, (hh + 1) * hd)
            q = q_ref[rows, sl]
            if b == 0:
                s = jnp.concatenate([_dot_nt(q, kp_ref[:, sl]), _dot_nt(q, kc_ref[rows, sl])], axis=1)
            else:
                s = _dot_nt(q, kc_ref[keys, sl])
            s_scr[slot, hh] = jnp.where(mask, s, -jnp.inf)
        lse_tile = jnp.zeros((blk, LANES), F32)
        for hh in range(heads):
            s = s_scr[slot, hh]
            m = jnp.max(s, axis=1, keepdims=True)
            p = jnp.exp(s - m)
            den = jnp.sum(p, axis=1, keepdims=True)
            p_scr[slot, hh] = (p * (1.0 / den)).astype(BF16)
            lse_tile = jnp.where(lane == hh, m + jnp.log(den), lse_tile)
        lse_ref[rows, :] = lse_tile
        for hh in range(heads):
            sl = slice(hh * hd, (hh + 1) * hd)
            p = p_scr[slot, hh]
            if b == 0:
                o = _dot(p[:, :blk], vp_ref[:, sl]) + _dot(p[:, blk:], vc_ref[rows, sl])
            else:
                o = _dot(p, vc_ref[keys, sl])
            o_ref[rows, sl] = o.astype(o_ref.dtype)


def _attn_group(qkv, window, dilation, cfg, name, blocks_per_step=4):
    dil, u, w3 = qkv.shape
    gw = cfg.group_width
    heads = cfg.attn_heads_per_group
    blk = window // dilation
    assert dil == dilation and w3 == 3 * gw and u % blk == 0 and heads <= LANES
    nb = u // blk
    nq = blocks_per_step
    while nb % nq:
        nq //= 2
    rows = nq * blk

    def cur(part):
        return pl.BlockSpec((None, rows, gw), lambda r, n: (r, n, part))

    def prev(part):
        return pl.BlockSpec((None, blk, gw), lambda r, n: (r, jnp.maximum(n * nq - 1, 0), part))

    return pl.pallas_call(
        functools.partial(_attn_kernel, heads=heads, hd=cfg.attn_head_dim, blk=blk),
        grid=(dilation, nb // nq),
        in_specs=[cur(0), prev(1), cur(1), prev(2), cur(2)],
        out_specs=[pl.BlockSpec((None, rows, gw), lambda r, n: (r, n, 0)),
                   pl.BlockSpec((None, rows, LANES), lambda r, n: (r, n, 0))],
        out_shape=[jax.ShapeDtypeStruct((dilation, u, gw), BF16),
                   jax.ShapeDtypeStruct((dilation, u, LANES), F32)],
        scratch_shapes=[pltpu.VMEM((2, heads, blk, 2 * blk), F32), pltpu.VMEM((2, heads, blk, 2 * blk), BF16)],
        compiler_params=_params(("parallel", "arbitrary")),
        name=name,
    )(qkv, qkv, qkv, qkv, qkv)


def _combine_kernel(*refs, dilations, hd):
    ng = len(dilations)
    o_refs, l_refs, out_ref = refs[:ng], refs[ng:2 * ng], refs[2 * ng]
    ts, gw = out_ref.shape
    perms = [_perm_matrix(ts, dil, BF16) if dil > 1 else None for dil in dilations]
    ls = []
    for gi in range(ng):
        l_g = l_refs[gi][...].reshape(ts, LANES)
        if perms[gi] is not None:
            l_g = functools.reduce(lambda a, b: a + b, [_dot(perms[gi], p) for p in _split_bf16(l_g, 3)])
        ls.append(l_g)
    m = functools.reduce(jnp.maximum, ls)
    es = [jnp.exp(l - m) for l in ls]
    inv = 1.0 / functools.reduce(lambda a, b: a + b, es)
    ws = [e * inv for e in es]
    width = 2 * hd if gw % (2 * hd) == 0 else hd
    for c0 in range(0, gw, width):
        o_c = []
        for gi in range(ng):
            o_g = o_refs[gi][:, :, c0:c0 + width].reshape(ts, width)
            o_c.append(_dot(perms[gi], o_g) if perms[gi] is not None else o_g.astype(F32))
        for c in range(c0, c0 + width, hd):
            hh = c // hd
            acc = None
            for gi in range(ng):
                term = ws[gi][:, hh:hh + 1] * o_c[gi][:, c - c0:c - c0 + hd]
                acc = term if acc is None else acc + term
            out_ref[:, c:c + hd] = acc.astype(out_ref.dtype)


def _combine(outs, lses, dilations, cfg):
    gw = cfg.group_width
    s = outs[0].shape[0] * outs[0].shape[1]
    ts = _tile(s, 512)
    in_specs = []
    for dil in dilations:
        assert ts % (dil * 2 * SUBLANES) == 0
        in_specs.append(pl.BlockSpec((dil, ts // dil, gw), lambda i: (0, i, 0)))
    for dil in dilations:
        in_specs.append(pl.BlockSpec((dil, ts // dil, LANES), lambda i: (0, i, 0)))
    return pl.pallas_call(
        functools.partial(_combine_kernel, dilations=tuple(dilations), hd=cfg.attn_head_dim),
        grid=(s // ts,),
        in_specs=in_specs,
        out_specs=pl.BlockSpec((ts, gw), lambda i: (i, 0)),
        out_shape=jax.ShapeDtypeStruct((s, gw), BF16),
        compiler_params=_params(("parallel",)),
        name="attn_combine",
    )(*outs, *lses)


def _forward(cfg, x, c, positions, ada_w, ada_b, norm_mix, norm_mlp, w_in, w_alpha_up, b_alpha,
             gla_norm, w_branch_gla, w_branch_attn, w_gate, b_gate, w_out, w_mlp_in, w_mlp_out,
             norm_final):
    bsz, s, d = x.shape
    assert bsz == 1 and d == cfg.d_model
    depth = cfg.depth
    x2 = x.reshape(s, d)
    mod = _ada_mod(c.reshape(d, 1), ada_w, ada_b.reshape(depth, 1, -1))
    dils = [dil for _, dil in cfg.dilated_groups]
    perm_dils = sorted({dil for dil in dils if dil > 1})
    pos = positions.reshape(s)
    pos_all = jnp.concatenate([pos.reshape(s // dil, dil).T.reshape(s) for dil in dils]).reshape(-1, 1)
    cos_t, sin_t = _rope_tables(pos_all, cfg)
    g_mix = norm_mix.reshape(depth, 1, d)
    g_mlp = norm_mlp.reshape(depth, 1, d)
    gw, rank = cfg.gla_width, cfg.gate_rank
    w_in_t = jnp.swapaxes(w_in, 1, 2).astype(BF16)
    w_out_b = w_out.astype(BF16)
    w_mlp_in_b = w_mlp_in.astype(BF16)
    w_mlp_out_b = w_mlp_out.astype(BF16)

    for l in range(depth):
        wt_attn = w_in_t[l, gw + rank:, :]
        w_glr = w_in[l, :, gw:gw + rank]
        w_glr3 = jnp.pad(jnp.concatenate([w_glr] * 3, axis=1), ((0, 0), (0, LANES - 3 * rank))).astype(BF16)
        wa_hi, wa_lo = _split_bf16(w_alpha_up[l], 2)
        w_alpha_cat = jnp.pad(jnp.concatenate([wa_hi, wa_lo, wa_hi], axis=0), ((0, LANES - 3 * rank), (0, 0)))

        hs = _norm_mod(x2, g_mix, mod, l, 0, 1, cfg, dilations=perm_dils)
        h = hs[0]
        h_by_dil = {1: h, **{dil: hp.reshape(s, d) for dil, hp in zip(perm_dils, hs[1:])}}
        proj = _mm_nt(h, w_in_t, l, 0, gw, BF16, name="gla_proj")
        b_cum = _gla_gate(h, w_glr3, w_alpha_cat, b_alpha[l].reshape(1, -1), GLA_BLOCK, cfg)
        o_gla = _gla_block(proj, b_cum, gla_norm[l].reshape(1, -1), GLA_BLOCK, cfg)
        outs, lses = [], []
        for g, (window, dil) in enumerate(cfg.dilated_groups):
            qkv = _attn_proj(h_by_dil[dil], wt_attn, cos_t, sin_t, g, cfg)
            o_g, lse_g = _attn_group(qkv.reshape(dil, s // dil, -1), window, dil, cfg, f"dilated_attn_{g}")
            outs.append(o_g)
            lses.append(lse_g)
        o_attn = _combine(outs, lses, dils, cfg)
        merged = _merge(h, o_gla, o_attn, w_gate[l].astype(BF16), b_gate[l].reshape(1, -1),
                        w_branch_gla[l].astype(BF16), w_branch_attn[l].astype(BF16))
        x2 = _mm_resid(merged, w_out_b, x2, mod, l, 2, tk=4096, name="out_proj")

        h = _norm_mod(x2, g_mlp, mod, l, 3, 4, cfg)[0]
        a = _mm(_mm_relu2_kernel, h, w_mlp_in_b, l, w_mlp_in_b.shape[2], BF16, name="mlp_in")
        x2 = _mm_resid(a, w_mlp_out_b, x2, mod, l, 5, tk=2048, name="mlp_out")

    return _final_norm(x2, norm_final.reshape(1, d), cfg).reshape(bsz, s, d)


def kernel(x, c, positions, ada_w, ada_b, norm_mix, norm_mlp, w_in, w_alpha_up, b_alpha, gla_norm,
           w_branch_gla, w_branch_attn, w_gate, b_gate, w_out, w_mlp_in, w_mlp_out, norm_final):
    return _forward(Config(), x, c, positions, ada_w, ada_b, norm_mix, norm_mlp, w_in, w_alpha_up,
                    b_alpha, gla_norm, w_branch_gla, w_branch_attn, w_gate, b_gate, w_out, w_mlp_in,
                    w_mlp_out, norm_final)
```

```python
import functools
from typing import NamedTuple

import jax
import jax.numpy as jnp
from jax import lax
from jax.experimental import pallas as pl
from jax.experimental.pallas import tpu as pltpu

F32 = jnp.float32
BF16 = jnp.bfloat16
LANES = 128
SUBLANES = 8
VMEM_LIMIT = 56 * 1024 * 1024
GLA_BLOCK = 256
LOG2_E = 1.4426950408889634


class Config(NamedTuple):
    d_model: int = 4096
    depth: int = 2
    gla_heads: int = 4
    gate_rank: int = 16
    gate_tau: float = 16.0
    gla_chunk: int = 64
    attn_head_dim: int = 128
    attn_heads_per_group: int = 16
    dilated_groups: tuple = ((128, 1), (512, 4), (2048, 16))
    rope_theta: float = 10000.0
    ff_mult: int = 4
    n_mod: int = 6
    norm_eps: float = 1e-6

    @property
    def gla_dk(self):
        return self.d_model // 2

    @property
    def gla_dv(self):
        return self.d_model

    @property
    def dk_head(self):
        return self.gla_dk // self.gla_heads

    @property
    def dv_head(self):
        return self.gla_dv // self.gla_heads

    @property
    def group_width(self):
        return self.attn_heads_per_group * self.attn_head_dim

    @property
    def attn_width(self):
        return len(self.dilated_groups) * self.group_width

    @property
    def gla_width(self):
        return 2 * self.gla_dk + 2 * self.gla_dv


def _params(sem):
    return pltpu.CompilerParams(dimension_semantics=sem, vmem_limit_bytes=VMEM_LIMIT)


def _sigmoid(x):
    return 1.0 / (1.0 + jnp.exp(-x))


def _dot(a, b):
    return jnp.dot(a, b, preferred_element_type=F32)


def _dot_nt(a, b):
    return lax.dot_general(a, b, (((1,), (1,)), ((), ())), preferred_element_type=F32)


def _dot_tn(a, b):
    return lax.dot_general(a, b, (((0,), (0,)), ((), ())), preferred_element_type=F32)


def _dot_f32(a, b):
    return jnp.dot(a, b, preferred_element_type=F32, precision=lax.Precision.HIGHEST)


def _tile(n, pref):
    t = min(n, pref)
    assert n % t == 0, (n, pref)
    return t


def _perm_matrix(size, inner, dtype):
    outer = size // inner
    assert inner * outer == size and inner & (inner - 1) == 0 and outer & (outer - 1) == 0
    i = lax.broadcasted_iota(jnp.int32, (size, size), 0)
    j = lax.broadcasted_iota(jnp.int32, (size, size), 1)
    src = (i & (inner - 1)) * outer + (i >> (inner.bit_length() - 1))
    return (j == src).astype(dtype)


def _ada_kernel(c_ref, w_ref, b_ref, o_ref, *, rows):
    d, tn = w_ref.shape

    def body(i, acc):
        r = pl.multiple_of(i * rows, rows)
        cc = c_ref[pl.ds(r, rows), :]
        cond = cc * _sigmoid(cc)
        p = w_ref[pl.ds(r, rows), :] * cond
        return acc + jnp.sum(p.reshape(rows // SUBLANES, SUBLANES, tn), axis=0)

    acc = lax.fori_loop(0, d // rows, body, jnp.zeros((SUBLANES, tn), F32))
    o_ref[...] = jnp.sum(acc, axis=0, keepdims=True) + b_ref[...]


def _ada_mod(c_col, ada_w, ada_b3):
    depth, d, n = ada_w.shape
    tn = _tile(n, 512)
    rows = _tile(d, 256)
    return pl.pallas_call(
        functools.partial(_ada_kernel, rows=rows),
        grid=(depth, n // tn),
        in_specs=[
            pl.BlockSpec((d, 1), lambda l, j: (0, 0)),
            pl.BlockSpec((None, d, tn), lambda l, j: (l, 0, j)),
            pl.BlockSpec((None, 1, tn), lambda l, j: (l, 0, j)),
        ],
        out_specs=pl.BlockSpec((None, 1, tn), lambda l, j: (l, 0, j)),
        out_shape=jax.ShapeDtypeStruct((depth, 1, n), F32),
        compiler_params=_params(("parallel", "parallel")),
        name="ada_mod",
    )(c_col, ada_w, ada_b3)


def _rope_kernel(p_ref, inv_ref, sgn_ref, c_ref, s_ref):
    ang = p_ref[...].astype(F32) * inv_ref[...]
    c_ref[...] = jnp.cos(ang)
    s_ref[...] = jnp.sin(ang) * sgn_ref[...]


def _rope_tables(pos_col, cfg):
    s = pos_col.shape[0]
    hd = cfg.attn_head_dim
    half = hd // 2
    inv = cfg.rope_theta ** (-jnp.arange(half, dtype=F32) / half)
    inv2 = jnp.concatenate([inv, inv]).reshape(1, hd)
    sgn = jnp.concatenate([-jnp.ones((half,), F32), jnp.ones((half,), F32)]).reshape(1, hd)
    ts = _tile(s, 1024)
    spec_t = pl.BlockSpec((ts, hd), lambda i: (i, 0))
    spec_c = pl.BlockSpec((1, hd), lambda i: (0, 0))
    return pl.pallas_call(
        _rope_kernel,
        grid=(s // ts,),
        in_specs=[pl.BlockSpec((ts, 1), lambda i: (i, 0)), spec_c, spec_c],
        out_specs=[spec_t, spec_t],
        out_shape=[jax.ShapeDtypeStruct((s, hd), F32)] * 2,
        compiler_params=_params(("parallel",)),
        name="rope_tables",
    )(pos_col, inv2, sgn)


def _norm_mod_kernel(x_ref, g_ref, sc_ref, sh_ref, o_ref, *rest, eps, dilations):
    x = x_ref[...]
    ms = jnp.mean(x * x, axis=-1, keepdims=True)
    y = x * lax.rsqrt(ms + eps) * g_ref[...]
    hb = (y * (1.0 + sc_ref[...]) + sh_ref[...]).astype(o_ref.dtype)
    o_ref[...] = hb
    ts = hb.shape[0]
    for p_ref, dil in zip(rest, dilations):
        n = ts // dil
        hp = _dot(_perm_matrix(ts, n, hb.dtype), hb).astype(p_ref.dtype)
        for r in range(dil):
            p_ref[r] = hp[r * n:(r + 1) * n, :]


def _norm_mod(x, g3, mod, layer, i_shift, i_scale, cfg, dilations=()):
    s, d = x.shape
    ts = _tile(s, 256)
    out_specs = [pl.BlockSpec((ts, d), lambda i: (i, 0))]
    out_shape = [jax.ShapeDtypeStruct((s, d), BF16)]
    for dil in dilations:
        assert ts % (dil * 2 * SUBLANES) == 0
        out_specs.append(pl.BlockSpec((dil, ts // dil, d), lambda i: (0, i, 0)))
        out_shape.append(jax.ShapeDtypeStruct((dil, s // dil, d), BF16))
    return pl.pallas_call(
        functools.partial(_norm_mod_kernel, eps=cfg.norm_eps, dilations=tuple(dilations)),
        grid=(s // ts,),
        in_specs=[
            pl.BlockSpec((ts, d), lambda i: (i, 0)),
            pl.BlockSpec((None, 1, d), lambda i: (layer, 0, 0)),
            pl.BlockSpec((None, 1, d), lambda i: (layer, 0, i_scale)),
            pl.BlockSpec((None, 1, d), lambda i: (layer, 0, i_shift)),
        ],
        out_specs=out_specs,
        out_shape=out_shape,
        compiler_params=_params(("parallel",)),
        name="norm_mod",
    )(x, g3, mod, mod)


def _final_norm_kernel(x_ref, g_ref, o_ref, *, eps):
    x = x_ref[...]
    ms = jnp.mean(x * x, axis=-1, keepdims=True)
    o_ref[...] = x * lax.rsqrt(ms + eps) * g_ref[...]


def _final_norm(x, g2, cfg):
    s, d = x.shape
    ts = _tile(s, 256)
    return pl.pallas_call(
        functools.partial(_final_norm_kernel, eps=cfg.norm_eps),
        grid=(s // ts,),
        in_specs=[pl.BlockSpec((ts, d), lambda i: (i, 0)), pl.BlockSpec((1, d), lambda i: (0, 0))],
        out_specs=pl.BlockSpec((ts, d), lambda i: (i, 0)),
        out_shape=jax.ShapeDtypeStruct((s, d), F32),
        compiler_params=_params(("parallel",)),
        name="final_norm",
    )(x, g2)


def _proj_kernel(x_ref, wt_ref, cos_ref, sin_ref, o_ref, *, q_scale, hd):
    acc = _dot_nt(x_ref[...], wt_ref[0])
    sec = pl.program_id(1)
    is_rope = sec < 2
    scale = jnp.where(sec == 0, q_scale, 1.0).astype(F32)
    cos = jnp.where(is_rope, cos_ref[...] * scale, 1.0)
    sin = jnp.where(is_rope, sin_ref[...] * scale, 0.0)
    for c in range(acc.shape[1] // hd):
        a = acc[:, c * hd:(c + 1) * hd]
        o_ref[:, c * hd:(c + 1) * hd] = (a * cos + pltpu.roll(a, hd // 2, 1) * sin).astype(o_ref.dtype)


def _attn_proj(h, wt3, layer, row0, cos_t, sin_t, g, cfg):
    m, k = h.shape
    hd = cfg.attn_head_dim
    gw = cfg.group_width
    tm = _tile(m, 1024)
    tn = _tile(gw, 1024)
    per = gw // tn
    sect = cfg.attn_width // tn
    tbl0 = g * (m // tm)
    assert row0 % tn == 0
    first_blk = row0 // tn + g * per
    return pl.pallas_call(
        functools.partial(_proj_kernel, q_scale=float(hd) ** -0.5, hd=hd),
        grid=(m // tm, 3, per),
        in_specs=[
            pl.BlockSpec((tm, k), lambda i, s, p: (i, 0)),
            pl.BlockSpec((1, tn, k), lambda i, s, p: (layer, first_blk + s * sect + p, 0)),
            pl.BlockSpec((tm, hd), lambda i, s, p: (tbl0 + i, 0)),
            pl.BlockSpec((tm, hd), lambda i, s, p: (tbl0 + i, 0)),
        ],
        out_specs=pl.BlockSpec((tm, tn), lambda i, s, p: (i, s * per + p)),
        out_shape=jax.ShapeDtypeStruct((m, 3 * gw), BF16),
        compiler_params=_params(("parallel", "parallel", "parallel")),
        name=f"attn_proj_{g}",
    )(h, wt3, cos_t, sin_t)


def _mm_plain_kernel(x_ref, w_ref, o_ref):
    o_ref[...] = _dot(x_ref[...], w_ref[...]).astype(o_ref.dtype)


def _mm_plain_nt_kernel(x_ref, wt_ref, o_ref):
    o_ref[...] = _dot_nt(x_ref[...], wt_ref[...]).astype(o_ref.dtype)


def _mm_nt(x, wt3, layer, row0, n, out_dtype, *, tm=1024, tn=1024, name):
    m, k = x.shape
    assert wt3.shape[2] == k and row0 % tn == 0 and row0 + n <= wt3.shape[1]
    tm, tn = _tile(m, tm), _tile(n, tn)
    j0 = row0 // tn
    return pl.pallas_call(
        _mm_plain_nt_kernel,
        grid=(m // tm, n // tn),
        in_specs=[pl.BlockSpec((tm, k), lambda i, j: (i, 0)),
                  pl.BlockSpec((None, tn, k), lambda i, j: (layer, j0 + j, 0))],
        out_specs=pl.BlockSpec((tm, tn), lambda i, j: (i, j)),
        out_shape=jax.ShapeDtypeStruct((m, n), out_dtype),
        compiler_params=_params(("parallel", "parallel")),
        name=name,
    )(x, wt3)


def _mm_relu2_kernel(x_ref, w_ref, o_ref):
    a = jnp.maximum(_dot(x_ref[...], w_ref[...]), 0.0)
    o_ref[...] = (a * a).astype(o_ref.dtype)


def _mm_resid_kernel(x_ref, w_ref, r_ref, g_ref, o_ref):
    o_ref[...] = r_ref[...] + g_ref[...] * _dot(x_ref[...], w_ref[...])


def _mm_resid_acc_kernel(x_ref, w_ref, r_ref, g_ref, o_ref, acc_ref):
    kk = pl.program_id(2)

    @pl.when(kk == 0)
    def _():
        acc_ref[...] = jnp.zeros_like(acc_ref)

    acc_ref[...] += _dot(x_ref[...], w_ref[...])

    @pl.when(kk == pl.num_programs(2) - 1)
    def _():
        o_ref[...] = r_ref[...] + g_ref[...] * acc_ref[...]


def _mm(kernel_fn, x, w3, layer, n, out_dtype, *, tm=1024, tn=1024, name):
    m, k = x.shape
    assert w3.shape[1] == k and n <= w3.shape[2]
    tm, tn = _tile(m, tm), _tile(n, tn)
    return pl.pallas_call(
        kernel_fn,
        grid=(m // tm, n // tn),
        in_specs=[pl.BlockSpec((tm, k), lambda i, j: (i, 0)),
                  pl.BlockSpec((None, k, tn), lambda i, j: (layer, 0, j))],
        out_specs=pl.BlockSpec((tm, tn), lambda i, j: (i, j)),
        out_shape=jax.ShapeDtypeStruct((m, n), out_dtype),
        compiler_params=_params(("parallel", "parallel")),
        name=name,
    )(x, w3)


def _mm_resid(x, w3, resid, mod, layer, i_gate, *, tk, name):
    m, k = x.shape
    n = w3.shape[2]
    assert w3.shape[1] == k
    tm, tn = _tile(m, 1024), _tile(n, 1024)
    gate_blocks = n // tn
    tk = _tile(k, tk)
    if tk == k:
        return pl.pallas_call(
            _mm_resid_kernel,
            grid=(m // tm, n // tn),
            in_specs=[
                pl.BlockSpec((tm, k), lambda i, j: (i, 0)),
                pl.BlockSpec((None, k, tn), lambda i, j: (layer, 0, j)),
                pl.BlockSpec((tm, tn), lambda i, j: (i, j)),
                pl.BlockSpec((None, 1, tn), lambda i, j: (layer, 0, i_gate * gate_blocks + j)),
            ],
            out_specs=pl.BlockSpec((tm, tn), lambda i, j: (i, j)),
            out_shape=jax.ShapeDtypeStruct((m, n), F32),
            compiler_params=_params(("parallel", "parallel")),
            name=name,
        )(x, w3, resid, mod)
    return pl.pallas_call(
        _mm_resid_acc_kernel,
        grid=(m // tm, n // tn, k // tk),
        in_specs=[
            pl.BlockSpec((tm, tk), lambda i, j, kk: (i, kk)),
            pl.BlockSpec((None, tk, tn), lambda i, j, kk: (layer, kk, j)),
            pl.BlockSpec((tm, tn), lambda i, j, kk: (i, j)),
            pl.BlockSpec((None, 1, tn), lambda i, j, kk: (layer, 0, i_gate * gate_blocks + j)),
        ],
        out_specs=pl.BlockSpec((tm, tn), lambda i, j, kk: (i, j)),
        out_shape=jax.ShapeDtypeStruct((m, n), F32),
        scratch_shapes=[pltpu.VMEM((tm, tn), F32)],
        compiler_params=_params(("parallel", "parallel", "arbitrary")),
        name=name,
    )(x, w3, resid, mod)


def _merge_kernel(h_ref, og_ref, oa_ref, wga_ref, wgb_ref, bga_ref, bgb_ref, wa_ref, wb_ref, o_ref):
    h = h_ref[...]
    ga = _sigmoid(_dot(h, wga_ref[...]) + bga_ref[...])
    gb = _sigmoid(_dot(h, wgb_ref[...]) + bgb_ref[...])
    a = _dot(og_ref[...], wa_ref[...])
    b = _dot(oa_ref[...], wb_ref[...])
    o_ref[...] = (ga * a + gb * b).astype(o_ref.dtype)


def _merge(h, o_gla, o_attn, w_gate3, b_gate3, w_a3, w_b3, layer):
    m, d = h.shape
    tm, tn = _tile(m, 512), _tile(d, 512)
    nb = d // tn
    x_spec = lambda kdim: pl.BlockSpec((tm, kdim), lambda j, i: (i, 0))
    w_spec = lambda kdim, off: pl.BlockSpec((None, kdim, tn), lambda j, i: (layer, 0, j + off))
    b_spec = lambda off: pl.BlockSpec((None, 1, tn), lambda j, i: (layer, 0, j + off))
    return pl.pallas_call(
        _merge_kernel,
        grid=(nb, m // tm),
        in_specs=[
            x_spec(d), x_spec(o_gla.shape[1]), x_spec(o_attn.shape[1]),
            w_spec(d, 0), w_spec(d, nb), b_spec(0), b_spec(nb),
            w_spec(w_a3.shape[1], 0), w_spec(w_b3.shape[1], 0),
        ],
        out_specs=pl.BlockSpec((tm, tn), lambda j, i: (i, j)),
        out_shape=jax.ShapeDtypeStruct((m, d), BF16),
        compiler_params=_params(("parallel", "parallel")),
        name="gated_merge",
    )(h, o_gla, o_attn, w_gate3, w_gate3, b_gate3, b_gate3, w_a3, w_b3)


def _split_bf16(x, pieces):
    out = []
    for _ in range(pieces - 1):
        p = x.astype(BF16)
        out.append(p)
        x = x - p.astype(F32)
    out.append(x.astype(BF16))
    return out


def _gate_kernel(h_ref, wl_ref, wc_ref, ba_ref, b_ref, *, chunk, tau, rank):
    tm = h_ref.shape[0]
    x = _dot(h_ref[...], wl_ref[...])
    hi = x.astype(BF16).astype(F32)
    lane = lax.broadcasted_iota(jnp.int32, x.shape, 1)
    a = jnp.where(lane < 2 * rank, hi, x - hi).astype(BF16)
    logits = _dot(a, wc_ref[...]) + ba_ref[...]
    g = (jnp.minimum(logits, 0.0) - jnp.log(1.0 + jnp.exp(-jnp.abs(logits)))) * (LOG2_E / tau)
    ri = lax.broadcasted_iota(jnp.int32, (tm, tm), 0)
    ci = lax.broadcasted_iota(jnp.int32, (tm, tm), 1)
    tril = jnp.where(jnp.logical_and(ri >= ci, (ri ^ ci) < chunk), 1.0, 0.0).astype(BF16)
    p1, p2, p3 = _split_bf16(g, 3)
    b_ref[...] = _dot(tril, p1) + _dot(tril, p2) + _dot(tril, p3)


def _gla_gate(h, w_glr3, w_alpha_cat, b_alpha2, chunk, cfg):
    s, d = h.shape
    n = w_alpha_cat.shape[1]
    tm = _tile(s, max(256, chunk))
    assert tm % chunk == 0 and chunk & (chunk - 1) == 0 and 3 * cfg.gate_rank <= LANES
    return pl.pallas_call(
        functools.partial(_gate_kernel, chunk=chunk, tau=cfg.gate_tau, rank=cfg.gate_rank),
        grid=(s // tm,),
        in_specs=[
            pl.BlockSpec((tm, d), lambda i: (i, 0)),
            pl.BlockSpec((d, LANES), lambda i: (0, 0)),
            pl.BlockSpec((LANES, n), lambda i: (0, 0)),
            pl.BlockSpec((1, n), lambda i: (0, 0)),
        ],
        out_specs=pl.BlockSpec((tm, n), lambda i: (i, 0)),
        out_shape=jax.ShapeDtypeStruct((s, n), F32),
        compiler_params=_params(("parallel",)),
        name="gla_gate",
    )(h, w_glr3, w_alpha_cat, b_alpha2)


def _gla_block_kernel(q_ref, k_ref, v_ref, gr_ref, b_ref, gn_ref, o_ref, st_ref, *, heads, q_scale, eps):
    rows = q_ref.shape[0]
    dk = q_ref.shape[1] // heads
    dv = v_ref.shape[1] // heads
    sub = SUBLANES
    half = rows // 2

    @pl.when(pl.program_id(1) == 0)
    def _():
        st_ref[...] = jnp.zeros_like(st_ref)

    ri = lax.broadcasted_iota(jnp.int32, (half, half), 0)
    ci = lax.broadcasted_iota(jnp.int32, (half, half), 1)
    sub_i = lax.broadcasted_iota(jnp.int32, (sub, half), 0)
    lane_i = lax.broadcasted_iota(jnp.int32, (sub, half), 1)

    def cross_scores(q, k, b, g_blk):
        n = q.shape[0]
        span = 2 * g_blk
        zero = jnp.zeros((g_blk, dk), F32)
        q_parts, k_parts = [], []
        for s0 in range(0, n, span):
            ref = jnp.broadcast_to(b[s0 + g_blk - 1:s0 + g_blk, :], (g_blk, dk))
            lo, up = slice(s0, s0 + g_blk), slice(s0 + g_blk, s0 + span)
            q_parts += [zero, q[up] * jnp.exp2(b[up] - ref)]
            k_parts += [k[lo] * jnp.exp2(ref - b[lo]), zero]
        qg = jnp.concatenate(q_parts, axis=0).astype(BF16)
        kg = jnp.concatenate(k_parts, axis=0).astype(BF16)
        return _dot_nt(qg, kg)

    for hh in range(heads):
        ksl = slice(hh * dk, (hh + 1) * dk)
        vsl = slice(hh * dv, (hh + 1) * dv)

        q = q_ref[:, ksl].astype(F32) * q_scale
        k = k_ref[:, ksl].astype(F32)
        vb = v_ref[:, vsl]
        b = b_ref[:, ksl]
        b_last = b[rows - 1:rows, :]

        st = st_ref[hh]
        qe = (q * jnp.exp2(b)).astype(BF16)
        o = _dot_nt(qe, st.astype(BF16))

        b_mid = jnp.broadcast_to(b[half - 1:half, :], (half, dk))
        x_lo = _dot_nt((q[half:] * jnp.exp2(b[half:] - b_mid)).astype(BF16),
                       (k[:half] * jnp.exp2(b_mid - b[:half])).astype(BF16))
        a_halves = []
        for hf in range(2):
            rs = slice(hf * half, (hf + 1) * half)
            qh, kh, bh = q[rs], k[rs], b[rs]
            tiles = []
            for lo in range(0, half, sub):
                qi, ki, bi = qh[lo:lo + sub, :], kh[lo:lo + sub, :], bh[lo:lo + sub, :]
                acc = jnp.zeros((sub, half), F32)
                for jj in range(sub):
                    t = qi * ki[jj:jj + 1, :] * jnp.exp2(bi - bi[jj:jj + 1, :])
                    col = jnp.sum(t, axis=1, keepdims=True)
                    keep = jnp.logical_and(lane_i == lo + jj, sub_i >= jj)
                    acc = jnp.where(keep, col, acc)
                tiles.append(acc)
            a = jnp.concatenate(tiles, axis=0) + cross_scores(qh, kh, bh, half // 2)
            g_blk = half // 4
            while g_blk >= sub:
                same = (ri ^ ci) < 2 * g_blk
                a = a + jnp.where(same, cross_scores(qh, kh, bh, g_blk), 0.0)
                g_blk //= 2
            a_halves.append(a)
        top = jnp.concatenate([a_halves[0], jnp.zeros((half, half), F32)], axis=1)
        bot = jnp.concatenate([x_lo, a_halves[1]], axis=1)
        s = jnp.concatenate([top, bot], axis=0).astype(BF16)
        o = o + _dot(s, vb)

        kd = (k * jnp.exp2(b_last - b)).astype(BF16)
        st_ref[hh] = st * jnp.exp2(b_last) + _dot_tn(vb, kd)

        ms = jnp.mean(o * o, axis=-1, keepdims=True)
        y = o * lax.rsqrt(ms + eps) * gn_ref[...]
        gr = gr_ref[:, vsl].astype(F32)
        o_ref[:, vsl] = (y * (gr * _sigmoid(gr))).astype(o_ref.dtype)


def _gla_block(proj, b_cum, gla_norm2, rows, cfg, heads_per_step=2):
    s = proj.shape[0]
    nh, dk, dv = cfg.gla_heads, cfg.dk_head, cfg.dv_head
    hp = heads_per_step
    assert s % rows == 0 and nh % hp == 0 and rows & (rows - 1) == 0 and rows >= 8 * SUBLANES
    assert cfg.gla_dk % (hp * dk) == 0 and (2 * cfg.gla_dk) % (hp * dv) == 0
    k_off = cfg.gla_dk // (hp * dk)
    v_off = (2 * cfg.gla_dk) // (hp * dv)
    r_off = (2 * cfg.gla_dk + cfg.gla_dv) // (hp * dv)
    return pl.pallas_call(
        functools.partial(_gla_block_kernel, heads=hp, q_scale=float(dk) ** -0.5, eps=cfg.norm_eps),
        grid=(nh // hp, s // rows),
        in_specs=[
            pl.BlockSpec((rows, hp * dk), lambda h, t: (t, h)),
            pl.BlockSpec((rows, hp * dk), lambda h, t: (t, k_off + h)),
            pl.BlockSpec((rows, hp * dv), lambda h, t: (t, v_off + h)),
            pl.BlockSpec((rows, hp * dv), lambda h, t: (t, r_off + h)),
            pl.BlockSpec((rows, hp * dk), lambda h, t: (t, h)),
            pl.BlockSpec((1, dv), lambda h, t: (0, 0)),
        ],
        out_specs=pl.BlockSpec((rows, hp * dv), lambda h, t: (t, h)),
        out_shape=jax.ShapeDtypeStruct((s, cfg.gla_dv), BF16),
        scratch_shapes=[pltpu.VMEM((hp, dv, dk), F32)],
        compiler_params=_params(("parallel", "arbitrary")),
        name="gla",
    )(proj, proj, proj, proj, b_cum, gla_norm2)


def _attn_kernel(q_ref, kp_ref, kc_ref, vp_ref, vc_ref, o_ref, lse_ref, s_scr, p_scr, *, heads, hd, blk):
    nq = q_ref.shape[0] // blk
    n = pl.program_id(1)
    qi = lax.broadcasted_iota(jnp.int32, (blk, 2 * blk), 0)
    kj = lax.broadcasted_iota(jnp.int32, (blk, 2 * blk), 1)
    dist = blk + qi - kj
    band = jnp.logical_and(dist >= 0, dist <= blk)
    first = jnp.logical_and(band, jnp.logical_or(kj >= blk, n > 0))
    lane = lax.broadcasted_iota(jnp.int32, (blk, LANES), 1)
    for b in range(nq):
        rows = slice(b * blk, (b + 1) * blk)
        keys = slice((b - 1) * blk, (b + 1) * blk)
        mask = first if b == 0 else band
        slot = b % 2
        for hh in range(heads):
            sl = slice(hh * hd, (hh + 1) * hd)
            q = q_ref[rows, sl]
            if b == 0:
                s = jnp.concatenate([_dot_nt(q, kp_ref[:, sl]), _dot_nt(q, kc_ref[rows, sl])], axis=1)
            else:
                s = _dot_nt(q, kc_ref[keys, sl])
            s_scr[slot, hh] = jnp.where(mask, s, -jnp.inf)
        lse_tile = jnp.zeros((blk, LANES), F32)
        for hh in range(heads):
            s = s_scr[slot, hh]
            m = jnp.max(s, axis=1, keepdims=True)
            p = jnp.exp(s - m)
            den = jnp.sum(p, axis=1, keepdims=True)
            p_scr[slot, hh] = (p * (1.0 / den)).astype(BF16)
            lse_tile = jnp.where(lane == hh, m + jnp.log(den), lse_tile)
        lse_ref[rows, :] = lse_tile
        for hh in range(heads):
            sl = slice(hh * hd, (hh + 1) * hd)
            p = p_scr[slot, hh]
            if b == 0:
                o = _dot(p[:, :blk], vp_ref[:, sl]) + _dot(p[:, blk:], vc_ref[rows, sl])
            else:
                o = _dot(p, vc_ref[keys, sl])
            o_ref[rows, sl] = o.astype(o_ref.dtype)


def _attn_group(qkv, window, dilation, cfg, name, blocks_per_step=4):
    dil, u, w3 = qkv.shape
    gw = cfg.group_width
    heads = cfg.attn_heads_per_group
    blk = window // dilation
    assert dil == dilation and w3 == 3 * gw and u % blk == 0 and heads <= LANES
    nb = u // blk
    nq = blocks_per_step
    while nb % nq:
        nq //= 2
    rows = nq * blk

    def cur(part):
        return pl.BlockSpec((None, rows, gw), lambda r, n: (r, n, part))

    def prev(part):
        return pl.BlockSpec((None, blk, gw), lambda r, n: (r, jnp.maximum(n * nq - 1, 0), part))

    return pl.pallas_call(
        functools.partial(_attn_kernel, heads=heads, hd=cfg.attn_head_dim, blk=blk),
        grid=(dilation, nb // nq),
        in_specs=[cur(0), prev(1), cur(1), prev(2), cur(2)],
        out_specs=[pl.BlockSpec((None, rows, gw), lambda r, n: (r, n, 0)),
                   pl.BlockSpec((None, rows, LANES), lambda r, n: (r, n, 0))],
        out_shape=[jax.ShapeDtypeStruct((dilation, u, gw), BF16),
                   jax.ShapeDtypeStruct((dilation, u, LANES), F32)],
        scratch_shapes=[pltpu.VMEM((2, heads, blk, 2 * blk), F32), pltpu.VMEM((2, heads, blk, 2 * blk), BF16)],
        compiler_params=_params(("parallel", "arbitrary")),
        name=name,
    )(qkv, qkv, qkv, qkv, qkv)


def _combine_kernel(*refs, dilations, hd):
    ng = len(dilations)
    o_refs, l_refs, out_ref = refs[:ng], refs[ng:2 * ng], refs[2 * ng]
    ts, gw = out_ref.shape
    perms = [_perm_matrix(ts, dil, BF16) if dil > 1 else None for dil in dilations]
    ls = []
    for gi in range(ng):
        l_g = l_refs[gi][...].reshape(ts, LANES)
        if perms[gi] is not None:
            l_g = functools.reduce(lambda a, b: a + b, [_dot(perms[gi], p) for p in _split_bf16(l_g, 3)])
        ls.append(l_g)
    m = functools.reduce(jnp.maximum, ls)
    es = [jnp.exp(l - m) for l in ls]
    inv = 1.0 / functools.reduce(lambda a, b: a + b, es)
    ws = [e * inv for e in es]
    width = 2 * hd if gw % (2 * hd) == 0 else hd
    for c0 in range(0, gw, width):
        o_c = []
        for gi in range(ng):
            o_g = o_refs[gi][:, :, c0:c0 + width].reshape(ts, width)
            o_c.append(_dot(perms[gi], o_g) if perms[gi] is not None else o_g.astype(F32))
        for c in range(c0, c0 + width, hd):
            hh = c // hd
            acc = None
            for gi in range(ng):
                term = ws[gi][:, hh:hh + 1] * o_c[gi][:, c - c0:c - c0 + hd]
                acc = term if acc is None else acc + term
            out_ref[:, c:c + hd] = acc.astype(out_ref.dtype)


def _combine(outs, lses, dilations, cfg):
    gw = cfg.group_width
    s = outs[0].shape[0] * outs[0].shape[1]
    ts = _tile(s, 512)
    in_specs = []
    for dil in dilations:
        assert ts % (dil * 2 * SUBLANES) == 0
        in_specs.append(pl.BlockSpec((dil, ts // dil, gw), lambda i: (0, i, 0)))
    for dil in dilations:
        in_specs.append(pl.BlockSpec((dil, ts // dil, LANES), lambda i: (0, i, 0)))
    return pl.pallas_call(
        functools.partial(_combine_kernel, dilations=tuple(dilations), hd=cfg.attn_head_dim),
        grid=(s // ts,),
        in_specs=in_specs,
        out_specs=pl.BlockSpec((ts, gw), lambda i: (i, 0)),
        out_shape=jax.ShapeDtypeStruct((s, gw), BF16),
        compiler_params=_params(("parallel",)),
        name="attn_combine",
    )(*outs, *lses)


def _forward(cfg, x, c, positions, ada_w, ada_b, norm_mix, norm_mlp, w_in, w_alpha_up, b_alpha,
             gla_norm, w_branch_gla, w_branch_attn, w_gate, b_gate, w_out, w_mlp_in, w_mlp_out,
             norm_final):
    bsz, s, d = x.shape
    assert bsz == 1 and d == cfg.d_model
    depth = cfg.depth
    x2 = x.reshape(s, d)
    mod = _ada_mod(c.reshape(d, 1), ada_w, ada_b.reshape(depth, 1, -1))
    dils = [dil for _, dil in cfg.dilated_groups]
    perm_dils = sorted({dil for dil in dils if dil > 1})
    pos = positions.reshape(s)
    pos_all = jnp.concatenate([pos.reshape(s // dil, dil).T.reshape(s) for dil in dils]).reshape(-1, 1)
    cos_t, sin_t = _rope_tables(pos_all, cfg)
    g_mix = norm_mix.reshape(depth, 1, d)
    g_mlp = norm_mlp.reshape(depth, 1, d)
    gw, rank = cfg.gla_width, cfg.gate_rank
    w_in_t = jnp.swapaxes(w_in, 1, 2).astype(BF16)
    w_attn_t = w_in_t[:, gw + rank:]
    w_gate_b = w_gate.astype(BF16)
    b_gate3 = b_gate.reshape(depth, 1, -1)
    w_bgla_b = w_branch_gla.astype(BF16)
    w_battn_b = w_branch_attn.astype(BF16)
    w_out_b = w_out.astype(BF16)
    w_mlp_in_b = w_mlp_in.astype(BF16)
    w_mlp_out_b = w_mlp_out.astype(BF16)

    for l in range(depth):
        w_glr = w_in[l, :, gw:gw + rank]
        w_glr3 = jnp.pad(jnp.concatenate([w_glr] * 3, axis=1), ((0, 0), (0, LANES - 3 * rank))).astype(BF16)
        wa_hi, wa_lo = _split_bf16(w_alpha_up[l], 2)
        w_alpha_cat = jnp.pad(jnp.concatenate([wa_hi, wa_lo, wa_hi], axis=0), ((0, LANES - 3 * rank), (0, 0)))

        hs = _norm_mod(x2, g_mix, mod, l, 0, 1, cfg, dilations=perm_dils)
        h = hs[0]
        h_by_dil = {1: h, **{dil: hp.reshape(s, d) for dil, hp in zip(perm_dils, hs[1:])}}
        proj = _mm_nt(h, w_in_t, l, 0, gw, BF16, name="gla_proj")
        b_cum = _gla_gate(h, w_glr3, w_alpha_cat, b_alpha[l].reshape(1, -1), GLA_BLOCK, cfg)
        o_gla = _gla_block(proj, b_cum, gla_norm[l].reshape(1, -1), GLA_BLOCK, cfg)
        outs, lses = [], []
        for g, (window, dil) in enumerate(cfg.dilated_groups):
            qkv = _attn_proj(h_by_dil[dil], w_attn_t, l, 0, cos_t, sin_t, g, cfg)
            o_g, lse_g = _attn_group(qkv.reshape(dil, s // dil, -1), window, dil, cfg, f"dilated_attn_{g}")
            outs.append(o_g)
            lses.append(lse_g)
        o_attn = _combine(outs, lses, dils, cfg)
        merged = _merge(h, o_gla, o_attn, w_gate_b, b_gate3, w_bgla_b, w_battn_b, l)
        x2 = _mm_resid(merged, w_out_b, x2, mod, l, 2, tk=4096, name="out_proj")

        h = _norm_mod(x2, g_mlp, mod, l, 3, 4, cfg)[0]
        a = _mm(_mm_relu2_kernel, h, w_mlp_in_b, l, w_mlp_in_b.shape[2], BF16, name="mlp_in")
        x2 = _mm_resid(a, w_mlp_out_b, x2, mod, l, 5, tk=2048, name="mlp_out")

    return _final_norm(x2, norm_final.reshape(1, d), cfg).reshape(bsz, s, d)


def kernel(x, c, positions, ada_w, ada_b, norm_mix, norm_mlp, w_in, w_alpha_up, b_alpha, gla_norm,
           w_branch_gla, w_branch_attn, w_gate, b_gate, w_out, w_mlp_in, w_mlp_out, norm_final):
    return _forward(Config(), x, c, positions, ada_w, ada_b, norm_mix, norm_mlp, w_in, w_alpha_up,
                    b_alpha, gla_norm, w_branch_gla, w_branch_attn, w_gate, b_gate, w_out, w_mlp_in,
                    w_mlp_out, norm_final)
```

```python
import functools
from typing import NamedTuple

import jax
import jax.numpy as jnp
from jax import lax
from jax.experimental import pallas as pl
from jax.experimental.pallas import tpu as pltpu

F32 = jnp.float32
BF16 = jnp.bfloat16
LANES = 128
SUBLANES = 8
VMEM_LIMIT = 56 * 1024 * 1024
GLA_BLOCK = 256
LOG2_E = 1.4426950408889634
LN_2 = 0.6931471805599453


class Config(NamedTuple):
    d_model: int = 4096
    depth: int = 2
    gla_heads: int = 4
    gate_rank: int = 16
    gate_tau: float = 16.0
    attn_head_dim: int = 128
    attn_heads_per_group: int = 16
    dilated_groups: tuple = ((128, 1), (512, 4), (2048, 16))
    rope_theta: float = 10000.0
    ff_mult: int = 4
    n_mod: int = 6
    norm_eps: float = 1e-6

    @property
    def gla_dk(self):
        return self.d_model // 2

    @property
    def gla_dv(self):
        return self.d_model

    @property
    def dk_head(self):
        return self.gla_dk // self.gla_heads

    @property
    def dv_head(self):
        return self.gla_dv // self.gla_heads

    @property
    def group_width(self):
        return self.attn_heads_per_group * self.attn_head_dim

    @property
    def attn_width(self):
        return len(self.dilated_groups) * self.group_width

    @property
    def gla_width(self):
        return 2 * self.gla_dk + 2 * self.gla_dv


def _params(sem):
    return pltpu.CompilerParams(dimension_semantics=sem, vmem_limit_bytes=VMEM_LIMIT)


def _sigmoid(x):
    return 1.0 / (1.0 + jnp.exp(-x))


def _dot(a, b):
    return jnp.dot(a, b, preferred_element_type=F32)


def _dot_nt(a, b):
    return lax.dot_general(a, b, (((1,), (1,)), ((), ())), preferred_element_type=F32)


def _dot_tn(a, b):
    return lax.dot_general(a, b, (((0,), (0,)), ((), ())), preferred_element_type=F32)


def _tile(n, pref):
    t = min(n, pref)
    assert n % t == 0, (n, pref)
    return t


def _perm_matrix(size, inner, dtype):
    outer = size // inner
    assert inner * outer == size and inner & (inner - 1) == 0 and outer & (outer - 1) == 0
    i = lax.broadcasted_iota(jnp.int32, (size, size), 0)
    j = lax.broadcasted_iota(jnp.int32, (size, size), 1)
    src = (i & (inner - 1)) * outer + (i >> (inner.bit_length() - 1))
    return (j == src).astype(dtype)


def _ada_kernel(c_ref, w_ref, b_ref, o_ref, *, rows):
    d, tn = w_ref.shape

    def body(i, acc):
        r = pl.multiple_of(i * rows, rows)
        cc = c_ref[pl.ds(r, rows), :]
        cond = cc * _sigmoid(cc)
        p = w_ref[pl.ds(r, rows), :] * cond
        return acc + jnp.sum(p.reshape(rows // SUBLANES, SUBLANES, tn), axis=0)

    acc = lax.fori_loop(0, d // rows, body, jnp.zeros((SUBLANES, tn), F32))
    o_ref[...] = jnp.sum(acc, axis=0, keepdims=True) + b_ref[...]


def _ada_mod(c_col, ada_w, ada_b3):
    depth, d, n = ada_w.shape
    tn = _tile(n, 512)
    rows = _tile(d, 256)
    return pl.pallas_call(
        functools.partial(_ada_kernel, rows=rows),
        grid=(depth, n // tn),
        in_specs=[
            pl.BlockSpec((d, 1), lambda l, j: (0, 0)),
            pl.BlockSpec((None, d, tn), lambda l, j: (l, 0, j)),
            pl.BlockSpec((None, 1, tn), lambda l, j: (l, 0, j)),
        ],
        out_specs=pl.BlockSpec((None, 1, tn), lambda l, j: (l, 0, j)),
        out_shape=jax.ShapeDtypeStruct((depth, 1, n), F32),
        compiler_params=_params(("parallel", "parallel")),
        name="ada_mod",
    )(c_col, ada_w, ada_b3)


def _rope_kernel(p_ref, inv_ref, sgn_ref, c_ref, s_ref):
    ang = p_ref[...].astype(F32) * inv_ref[...]
    c_ref[...] = jnp.cos(ang)
    s_ref[...] = jnp.sin(ang) * sgn_ref[...]


def _rope_tables(pos_col, cfg):
    s = pos_col.shape[0]
    hd = cfg.attn_head_dim
    half = hd // 2
    inv = cfg.rope_theta ** (-jnp.arange(half, dtype=F32) / half)
    inv2 = jnp.concatenate([inv, inv]).reshape(1, hd)
    sgn = jnp.concatenate([-jnp.ones((half,), F32), jnp.ones((half,), F32)]).reshape(1, hd)
    ts = _tile(s, 1024)
    spec_t = pl.BlockSpec((ts, hd), lambda i: (i, 0))
    spec_c = pl.BlockSpec((1, hd), lambda i: (0, 0))
    return pl.pallas_call(
        _rope_kernel,
        grid=(s // ts,),
        in_specs=[pl.BlockSpec((ts, 1), lambda i: (i, 0)), spec_c, spec_c],
        out_specs=[spec_t, spec_t],
        out_shape=[jax.ShapeDtypeStruct((s, hd), F32)] * 2,
        compiler_params=_params(("parallel",)),
        name="rope_tables",
    )(pos_col, inv2, sgn)


def _norm_mod_kernel(x_ref, g_ref, sc_ref, sh_ref, o_ref, *rest, eps, dilations):
    x = x_ref[...]
    ms = jnp.mean(x * x, axis=-1, keepdims=True)
    y = x * lax.rsqrt(ms + eps) * g_ref[...]
    hb = (y * (1.0 + sc_ref[...]) + sh_ref[...]).astype(o_ref.dtype)
    o_ref[...] = hb
    ts = hb.shape[0]
    for p_ref, dil in zip(rest, dilations):
        n = ts // dil
        hp = _dot(_perm_matrix(ts, n, hb.dtype), hb).astype(p_ref.dtype)
        for r in range(dil):
            p_ref[r] = hp[r * n:(r + 1) * n, :]


def _norm_mod(x, g3, mod, layer, i_shift, i_scale, cfg, dilations=()):
    s, d = x.shape
    ts = _tile(s, 256)
    out_specs = [pl.BlockSpec((ts, d), lambda i: (i, 0))]
    out_shape = [jax.ShapeDtypeStruct((s, d), BF16)]
    for dil in dilations:
        assert ts % (dil * 2 * SUBLANES) == 0
        out_specs.append(pl.BlockSpec((dil, ts // dil, d), lambda i: (0, i, 0)))
        out_shape.append(jax.ShapeDtypeStruct((dil, s // dil, d), BF16))
    return pl.pallas_call(
        functools.partial(_norm_mod_kernel, eps=cfg.norm_eps, dilations=tuple(dilations)),
        grid=(s // ts,),
        in_specs=[
            pl.BlockSpec((ts, d), lambda i: (i, 0)),
            pl.BlockSpec((None, 1, d), lambda i: (layer, 0, 0)),
            pl.BlockSpec((None, 1, d), lambda i: (layer, 0, i_scale)),
            pl.BlockSpec((None, 1, d), lambda i: (layer, 0, i_shift)),
        ],
        out_specs=out_specs,
        out_shape=out_shape,
        compiler_params=_params(("parallel",)),
        name="norm_mod",
    )(x, g3, mod, mod)


def _final_norm_kernel(x_ref, g_ref, o_ref, *, eps):
    x = x_ref[...]
    ms = jnp.mean(x * x, axis=-1, keepdims=True)
    o_ref[...] = x * lax.rsqrt(ms + eps) * g_ref[...]


def _final_norm(x, g2, cfg):
    s, d = x.shape
    ts = _tile(s, 256)
    return pl.pallas_call(
        functools.partial(_final_norm_kernel, eps=cfg.norm_eps),
        grid=(s // ts,),
        in_specs=[pl.BlockSpec((ts, d), lambda i: (i, 0)), pl.BlockSpec((1, d), lambda i: (0, 0))],
        out_specs=pl.BlockSpec((ts, d), lambda i: (i, 0)),
        out_shape=jax.ShapeDtypeStruct((s, d), F32),
        compiler_params=_params(("parallel",)),
        name="final_norm",
    )(x, g2)


def _proj_kernel(x_ref, wt_ref, cos_ref, sin_ref, o_ref, *, q_scale, hd):
    acc = _dot_nt(x_ref[...], wt_ref[...])
    sec = pl.program_id(1)
    is_rope = sec < 2
    scale = jnp.where(sec == 0, q_scale, 1.0).astype(F32)
    cos = jnp.where(is_rope, cos_ref[...] * scale, 1.0)
    sin = jnp.where(is_rope, sin_ref[...] * scale, 0.0)
    for c in range(acc.shape[1] // hd):
        a = acc[:, c * hd:(c + 1) * hd]
        o_ref[:, c * hd:(c + 1) * hd] = (a * cos + pltpu.roll(a, hd // 2, 1) * sin).astype(o_ref.dtype)


def _attn_proj(h, wt3, layer, cos_t, sin_t, g, cfg):
    m, k = h.shape
    hd = cfg.attn_head_dim
    gw = cfg.group_width
    tm = _tile(m, 1024)
    tn = _tile(gw, 1024)
    per = gw // tn
    sect = cfg.attn_width // tn
    tbl0 = g * (m // tm)
    return pl.pallas_call(
        functools.partial(_proj_kernel, q_scale=LOG2_E * float(hd) ** -0.5, hd=hd),
        grid=(m // tm, 3, per),
        in_specs=[
            pl.BlockSpec((tm, k), lambda i, s, p: (i, 0)),
            pl.BlockSpec((None, tn, k), lambda i, s, p: (layer, s * sect + g * per + p, 0)),
            pl.BlockSpec((tm, hd), lambda i, s, p: (tbl0 + i, 0)),
            pl.BlockSpec((tm, hd), lambda i, s, p: (tbl0 + i, 0)),
        ],
        out_specs=pl.BlockSpec((tm, tn), lambda i, s, p: (i, s * per + p)),
        out_shape=jax.ShapeDtypeStruct((m, 3 * gw), BF16),
        compiler_params=_params(("parallel", "parallel", "parallel")),
        name=f"attn_proj_{g}",
    )(h, wt3, cos_t, sin_t)


def _mm_plain_nt_kernel(x_ref, wt_ref, o_ref):
    o_ref[...] = _dot_nt(x_ref[...], wt_ref[...]).astype(o_ref.dtype)


def _gla_proj(x, wt3, layer, n):
    m, k = x.shape
    assert wt3.shape[2] == k and n <= wt3.shape[1]
    tm, tn = _tile(m, 1024), _tile(n, 1024)
    return pl.pallas_call(
        _mm_plain_nt_kernel,
        grid=(m // tm, n // tn),
        in_specs=[pl.BlockSpec((tm, k), lambda i, j: (i, 0)),
                  pl.BlockSpec((None, tn, k), lambda i, j: (layer, j, 0))],
        out_specs=pl.BlockSpec((tm, tn), lambda i, j: (i, j)),
        out_shape=jax.ShapeDtypeStruct((m, n), BF16),
        compiler_params=_params(("parallel", "parallel")),
        name="gla_proj",
    )(x, wt3)


def _mm_relu2_kernel(x_ref, w_ref, o_ref):
    a = jnp.maximum(_dot(x_ref[...], w_ref[...]), 0.0)
    o_ref[...] = (a * a).astype(o_ref.dtype)


def _mm_resid_kernel(x_ref, w_ref, r_ref, g_ref, o_ref):
    o_ref[...] = r_ref[...] + g_ref[...] * _dot(x_ref[...], w_ref[...])


def _mm_resid_acc_kernel(x_ref, w_ref, r_ref, g_ref, o_ref, acc_ref):
    kk = pl.program_id(2)

    @pl.when(kk == 0)
    def _():
        acc_ref[...] = jnp.zeros_like(acc_ref)

    acc_ref[...] += _dot(x_ref[...], w_ref[...])

    @pl.when(kk == pl.num_programs(2) - 1)
    def _():
        o_ref[...] = r_ref[...] + g_ref[...] * acc_ref[...]


def _mlp_in(x, w3, layer):
    m, k = x.shape
    n = w3.shape[2]
    assert w3.shape[1] == k
    tm, tn = _tile(m, 1024), _tile(n, 1024)
    return pl.pallas_call(
        _mm_relu2_kernel,
        grid=(m // tm, n // tn),
        in_specs=[pl.BlockSpec((tm, k), lambda i, j: (i, 0)),
                  pl.BlockSpec((None, k, tn), lambda i, j: (layer, 0, j))],
        out_specs=pl.BlockSpec((tm, tn), lambda i, j: (i, j)),
        out_shape=jax.ShapeDtypeStruct((m, n), BF16),
        compiler_params=_params(("parallel", "parallel")),
        name="mlp_in",
    )(x, w3)


def _mm_resid(x, w3, resid, mod, layer, i_gate, *, tk, name):
    m, k = x.shape
    n = w3.shape[2]
    assert w3.shape[1] == k
    tm, tn = _tile(m, 1024), _tile(n, 1024)
    gate_blocks = n // tn
    tk = _tile(k, tk)
    if tk == k:
        return pl.pallas_call(
            _mm_resid_kernel,
            grid=(m // tm, n // tn),
            in_specs=[
                pl.BlockSpec((tm, k), lambda i, j: (i, 0)),
                pl.BlockSpec((None, k, tn), lambda i, j: (layer, 0, j)),
                pl.BlockSpec((tm, tn), lambda i, j: (i, j)),
                pl.BlockSpec((None, 1, tn), lambda i, j: (layer, 0, i_gate * gate_blocks + j)),
            ],
            out_specs=pl.BlockSpec((tm, tn), lambda i, j: (i, j)),
            out_shape=jax.ShapeDtypeStruct((m, n), F32),
            compiler_params=_params(("parallel", "parallel")),
            name=name,
        )(x, w3, resid, mod)
    return pl.pallas_call(
        _mm_resid_acc_kernel,
        grid=(m // tm, n // tn, k // tk),
        in_specs=[
            pl.BlockSpec((tm, tk), lambda i, j, kk: (i, kk)),
            pl.BlockSpec((None, tk, tn), lambda i, j, kk: (layer, kk, j)),
            pl.BlockSpec((tm, tn), lambda i, j, kk: (i, j)),
            pl.BlockSpec((None, 1, tn), lambda i, j, kk: (layer, 0, i_gate * gate_blocks + j)),
        ],
        out_specs=pl.BlockSpec((tm, tn), lambda i, j, kk: (i, j)),
        out_shape=jax.ShapeDtypeStruct((m, n), F32),
        scratch_shapes=[pltpu.VMEM((tm, tn), F32)],
        compiler_params=_params(("parallel", "parallel", "arbitrary")),
        name=name,
    )(x, w3, resid, mod)


def _merge_kernel(h_ref, og_ref, oa_ref, wga_ref, wgb_ref, bga_ref, bgb_ref, wa_ref, wb_ref, o_ref):
    h = h_ref[...]
    ga = _sigmoid(_dot(h, wga_ref[...]) + bga_ref[...])
    gb = _sigmoid(_dot(h, wgb_ref[...]) + bgb_ref[...])
    a = _dot(og_ref[...], wa_ref[...])
    b = _dot(oa_ref[...], wb_ref[...])
    o_ref[...] = (ga * a + gb * b).astype(o_ref.dtype)


def _merge(h, o_gla, o_attn, w_gate3, b_gate3, w_a3, w_b3, layer):
    m, d = h.shape
    tm, tn = _tile(m, 512), _tile(d, 512)
    nb = d // tn
    x_spec = lambda kdim: pl.BlockSpec((tm, kdim), lambda j, i: (i, 0))
    w_spec = lambda kdim, off: pl.BlockSpec((None, kdim, tn), lambda j, i: (layer, 0, j + off))
    b_spec = lambda off: pl.BlockSpec((None, 1, tn), lambda j, i: (layer, 0, j + off))
    return pl.pallas_call(
        _merge_kernel,
        grid=(nb, m // tm),
        in_specs=[
            x_spec(d), x_spec(o_gla.shape[1]), x_spec(o_attn.shape[1]),
            w_spec(d, 0), w_spec(d, nb), b_spec(0), b_spec(nb),
            w_spec(w_a3.shape[1], 0), w_spec(w_b3.shape[1], 0),
        ],
        out_specs=pl.BlockSpec((tm, tn), lambda j, i: (i, j)),
        out_shape=jax.ShapeDtypeStruct((m, d), BF16),
        compiler_params=_params(("parallel", "parallel")),
        name="gated_merge",
    )(h, o_gla, o_attn, w_gate3, w_gate3, b_gate3, b_gate3, w_a3, w_b3)


def _split_bf16(x, pieces):
    out = []
    for _ in range(pieces - 1):
        p = x.astype(BF16)
        out.append(p)
        x = x - p.astype(F32)
    out.append(x.astype(BF16))
    return out


def _gate_kernel(h_ref, wl_ref, wc_ref, ba_ref, b_ref, *, chunk, tau, rank):
    tm = h_ref.shape[0]
    x = _dot(h_ref[...], wl_ref[...])
    hi = x.astype(BF16).astype(F32)
    lane = lax.broadcasted_iota(jnp.int32, x.shape, 1)
    a = jnp.where(lane < 2 * rank, hi, x - hi).astype(BF16)
    logits = _dot(a, wc_ref[...]) + ba_ref[...]
    g = (jnp.minimum(logits, 0.0) - jnp.log(1.0 + jnp.exp(-jnp.abs(logits)))) * (LOG2_E / tau)
    ri = lax.broadcasted_iota(jnp.int32, (tm, tm), 0)
    ci = lax.broadcasted_iota(jnp.int32, (tm, tm), 1)
    tril = jnp.where(jnp.logical_and(ri >= ci, (ri ^ ci) < chunk), 1.0, 0.0).astype(BF16)
    p1, p2, p3 = _split_bf16(g, 3)
    b_ref[...] = _dot(tril, p1) + _dot(tril, p2) + _dot(tril, p3)


def _gla_gate(h, w_glr3, w_alpha_cat, b_alpha2, chunk, cfg):
    s, d = h.shape
    n = w_alpha_cat.shape[1]
    tm = _tile(s, max(256, chunk))
    assert tm % chunk == 0 and chunk & (chunk - 1) == 0 and 3 * cfg.gate_rank <= LANES
    return pl.pallas_call(
        functools.partial(_gate_kernel, chunk=chunk, tau=cfg.gate_tau, rank=cfg.gate_rank),
        grid=(s // tm,),
        in_specs=[
            pl.BlockSpec((tm, d), lambda i: (i, 0)),
            pl.BlockSpec((d, LANES), lambda i: (0, 0)),
            pl.BlockSpec((LANES, n), lambda i: (0, 0)),
            pl.BlockSpec((1, n), lambda i: (0, 0)),
        ],
        out_specs=pl.BlockSpec((tm, n), lambda i: (i, 0)),
        out_shape=jax.ShapeDtypeStruct((s, n), F32),
        compiler_params=_params(("parallel",)),
        name="gla_gate",
    )(h, w_glr3, w_alpha_cat, b_alpha2)


def _gla_block_kernel(q_ref, k_ref, v_ref, gr_ref, b_ref, gn_ref, o_ref, st_ref, *, heads, q_scale, eps):
    rows = q_ref.shape[0]
    dk = q_ref.shape[1] // heads
    dv = v_ref.shape[1] // heads
    sub = SUBLANES
    half = rows // 2

    @pl.when(pl.program_id(1) == 0)
    def _():
        st_ref[...] = jnp.zeros_like(st_ref)

    ri = lax.broadcasted_iota(jnp.int32, (half, half), 0)
    ci = lax.broadcasted_iota(jnp.int32, (half, half), 1)
    sub_i = lax.broadcasted_iota(jnp.int32, (sub, half), 0)
    lane_i = lax.broadcasted_iota(jnp.int32, (sub, half), 1)

    def cross_scores(q, k, b, g_blk):
        n = q.shape[0]
        span = 2 * g_blk
        zero = jnp.zeros((g_blk, dk), F32)
        q_parts, k_parts = [], []
        for s0 in range(0, n, span):
            ref = jnp.broadcast_to(b[s0 + g_blk - 1:s0 + g_blk, :], (g_blk, dk))
            lo, up = slice(s0, s0 + g_blk), slice(s0 + g_blk, s0 + span)
            q_parts += [zero, q[up] * jnp.exp2(b[up] - ref)]
            k_parts += [k[lo] * jnp.exp2(ref - b[lo]), zero]
        qg = jnp.concatenate(q_parts, axis=0).astype(BF16)
        kg = jnp.concatenate(k_parts, axis=0).astype(BF16)
        return _dot_nt(qg, kg)

    for hh in range(heads):
        ksl = slice(hh * dk, (hh + 1) * dk)
        vsl = slice(hh * dv, (hh + 1) * dv)

        q = q_ref[:, ksl].astype(F32) * q_scale
        k = k_ref[:, ksl].astype(F32)
        vb = v_ref[:, vsl]
        b = b_ref[:, ksl]
        b_last = b[rows - 1:rows, :]

        st = st_ref[hh]
        qe = (q * jnp.exp2(b)).astype(BF16)
        o = _dot_nt(qe, st.astype(BF16))

        b_mid = jnp.broadcast_to(b[half - 1:half, :], (half, dk))
        x_lo = _dot_nt((q[half:] * jnp.exp2(b[half:] - b_mid)).astype(BF16),
                       (k[:half] * jnp.exp2(b_mid - b[:half])).astype(BF16))
        a_halves = []
        for hf in range(2):
            rs = slice(hf * half, (hf + 1) * half)
            qh, kh, bh = q[rs], k[rs], b[rs]
            tiles = []
            for lo in range(0, half, sub):
                qi, ki, bi = qh[lo:lo + sub, :], kh[lo:lo + sub, :], bh[lo:lo + sub, :]
                acc = jnp.zeros((sub, half), F32)
                for jj in range(sub):
                    t = qi * ki[jj:jj + 1, :] * jnp.exp2(bi - bi[jj:jj + 1, :])
                    col = jnp.sum(t, axis=1, keepdims=True)
                    keep = jnp.logical_and(lane_i == lo + jj, sub_i >= jj)
                    acc = jnp.where(keep, col, acc)
                tiles.append(acc)
            a = jnp.concatenate(tiles, axis=0) + cross_scores(qh, kh, bh, half // 2)
            g_blk = half // 4
            while g_blk >= sub:
                same = (ri ^ ci) < 2 * g_blk
                a = a + jnp.where(same, cross_scores(qh, kh, bh, g_blk), 0.0)
                g_blk //= 2
            a_halves.append(a)
        top = jnp.concatenate([a_halves[0], jnp.zeros((half, half), F32)], axis=1)
        bot = jnp.concatenate([x_lo, a_halves[1]], axis=1)
        s = jnp.concatenate([top, bot], axis=0).astype(BF16)
        o = o + _dot(s, vb)

        kd = (k * jnp.exp2(b_last - b)).astype(BF16)
        st_ref[hh] = st * jnp.exp2(b_last) + _dot_tn(vb, kd)

        ms = jnp.mean(o * o, axis=-1, keepdims=True)
        y = o * lax.rsqrt(ms + eps) * gn_ref[...]
        gr = gr_ref[:, vsl].astype(F32)
        o_ref[:, vsl] = (y * (gr * _sigmoid(gr))).astype(o_ref.dtype)


def _gla_block(proj, b_cum, gla_norm2, rows, cfg, heads_per_step=2):
    s = proj.shape[0]
    nh, dk, dv = cfg.gla_heads, cfg.dk_head, cfg.dv_head
    hp = heads_per_step
    assert s % rows == 0 and nh % hp == 0 and rows & (rows - 1) == 0 and rows >= 8 * SUBLANES
    assert cfg.gla_dk % (hp * dk) == 0 and (2 * cfg.gla_dk) % (hp * dv) == 0
    k_off = cfg.gla_dk // (hp * dk)
    v_off = (2 * cfg.gla_dk) // (hp * dv)
    r_off = (2 * cfg.gla_dk + cfg.gla_dv) // (hp * dv)
    return pl.pallas_call(
        functools.partial(_gla_block_kernel, heads=hp, q_scale=float(dk) ** -0.5, eps=cfg.norm_eps),
        grid=(nh // hp, s // rows),
        in_specs=[
            pl.BlockSpec((rows, hp * dk), lambda h, t: (t, h)),
            pl.BlockSpec((rows, hp * dk), lambda h, t: (t, k_off + h)),
            pl.BlockSpec((rows, hp * dv), lambda h, t: (t, v_off + h)),
            pl.BlockSpec((rows, hp * dv), lambda h, t: (t, r_off + h)),
            pl.BlockSpec((rows, hp * dk), lambda h, t: (t, h)),
            pl.BlockSpec((1, dv), lambda h, t: (0, 0)),
        ],
        out_specs=pl.BlockSpec((rows, hp * dv), lambda h, t: (t, h)),
        out_shape=jax.ShapeDtypeStruct((s, cfg.gla_dv), BF16),
        scratch_shapes=[pltpu.VMEM((hp, dv, dk), F32)],
        compiler_params=_params(("parallel", "arbitrary")),
        name="gla",
    )(proj, proj, proj, proj, b_cum, gla_norm2)


def _attn_kernel(q_ref, kp_ref, kc_ref, vp_ref, vc_ref, o_ref, lse_ref, s_scr, p_scr, *, heads, hd, blk):
    nq = q_ref.shape[0] // blk
    n = pl.program_id(1)
    qi = lax.broadcasted_iota(jnp.int32, (blk, 2 * blk), 0)
    kj = lax.broadcasted_iota(jnp.int32, (blk, 2 * blk), 1)
    dist = blk + qi - kj
    band = jnp.logical_and(dist >= 0, dist <= blk)
    first = jnp.logical_and(band, jnp.logical_or(kj >= blk, n > 0))
    lane = lax.broadcasted_iota(jnp.int32, (blk, LANES), 1)
    for b in range(nq):
        rows = slice(b * blk, (b + 1) * blk)
        keys = slice((b - 1) * blk, (b + 1) * blk)
        mask = first if b == 0 else band
        slot = b % 2
        for hh in range(heads):
            sl = slice(hh * hd, (hh + 1) * hd)
            q = q_ref[rows, sl]
            if b == 0:
                s = jnp.concatenate([_dot_nt(q, kp_ref[:, sl]), _dot_nt(q, kc_ref[rows, sl])], axis=1)
            else:
                s = _dot_nt(q, kc_ref[keys, sl])
            s_scr[slot, hh] = jnp.where(mask, s, -jnp.inf)
        lse_tile = jnp.zeros((blk, LANES), F32)
        for hh in range(heads):
            s = s_scr[slot, hh]
            m = jnp.max(s, axis=1, keepdims=True)
            p = jnp.exp2(s - m)
            den = jnp.sum(p, axis=1, keepdims=True)
            p_scr[slot, hh] = (p * (1.0 / den)).astype(BF16)
            lse_tile = jnp.where(lane == hh, m * LN_2 + jnp.log(den), lse_tile)
        lse_ref[rows, :] = lse_tile
        for hh in range(heads):
            sl = slice(hh * hd, (hh + 1) * hd)
            p = p_scr[slot, hh]
            if b == 0:
                o = _dot(p[:, :blk], vp_ref[:, sl]) + _dot(p[:, blk:], vc_ref[rows, sl])
            else:
                o = _dot(p, vc_ref[keys, sl])
            o_ref[rows, sl] = o.astype(o_ref.dtype)


def _attn_group(qkv, window, dilation, cfg, name, blocks_per_step=4):
    dil, u, w3 = qkv.shape
    gw = cfg.group_width
    heads = cfg.attn_heads_per_group
    blk = window // dilation
    assert dil == dilation and w3 == 3 * gw and u % blk == 0 and heads <= LANES
    nb = u // blk
    nq = blocks_per_step
    while nb % nq:
        nq //= 2
    rows = nq * blk

    def cur(part):
        return pl.BlockSpec((None, rows, gw), lambda r, n: (r, n, part))

    def prev(part):
        return pl.BlockSpec((None, blk, gw), lambda r, n: (r, jnp.maximum(n * nq - 1, 0), part))

    return pl.pallas_call(
        functools.partial(_attn_kernel, heads=heads, hd=cfg.attn_head_dim, blk=blk),
        grid=(dilation, nb // nq),
        in_specs=[cur(0), prev(1), cur(1), prev(2), cur(2)],
        out_specs=[pl.BlockSpec((None, rows, gw), lambda r, n: (r, n, 0)),
                   pl.BlockSpec((None, rows, LANES), lambda r, n: (r, n, 0))],
        out_shape=[jax.ShapeDtypeStruct((dilation, u, gw), BF16),
                   jax.ShapeDtypeStruct((dilation, u, LANES), F32)],
        scratch_shapes=[pltpu.VMEM((2, heads, blk, 2 * blk), F32), pltpu.VMEM((2, heads, blk, 2 * blk), BF16)],
        compiler_params=_params(("parallel", "arbitrary")),
        name=name,
    )(qkv, qkv, qkv, qkv, qkv)


def _combine_kernel(*refs, dilations, hd):
    ng = len(dilations)
    o_refs, l_refs, out_ref = refs[:ng], refs[ng:2 * ng], refs[2 * ng]
    ts, gw = out_ref.shape
    perms = [_perm_matrix(ts, dil, BF16) if dil > 1 else None for dil in dilations]
    ls = []
    for gi in range(ng):
        l_g = l_refs[gi][...].reshape(ts, LANES)
        if perms[gi] is not None:
            l_g = functools.reduce(lambda a, b: a + b, [_dot(perms[gi], p) for p in _split_bf16(l_g, 3)])
        ls.append(l_g)
    m = functools.reduce(jnp.maximum, ls)
    es = [jnp.exp(l - m) for l in ls]
    inv = 1.0 / functools.reduce(lambda a, b: a + b, es)
    ws = [e * inv for e in es]
    width = 2 * hd if gw % (2 * hd) == 0 else hd
    for c0 in range(0, gw, width):
        o_c = []
        for gi in range(ng):
            o_g = o_refs[gi][:, :, c0:c0 + width].reshape(ts, width)
            o_c.append(_dot(perms[gi], o_g) if perms[gi] is not None else o_g.astype(F32))
        for c in range(c0, c0 + width, hd):
            hh = c // hd
            acc = None
            for gi in range(ng):
                term = ws[gi][:, hh:hh + 1] * o_c[gi][:, c - c0:c - c0 + hd]
                acc = term if acc is None else acc + term
            out_ref[:, c:c + hd] = acc.astype(out_ref.dtype)


def _combine(outs, lses, dilations, cfg):
    gw = cfg.group_width
    s = outs[0].shape[0] * outs[0].shape[1]
    ts = _tile(s, 512)
    in_specs = []
    for dil in dilations:
        assert ts % (dil * 2 * SUBLANES) == 0
        in_specs.append(pl.BlockSpec((dil, ts // dil, gw), lambda i: (0, i, 0)))
    for dil in dilations:
        in_specs.append(pl.BlockSpec((dil, ts // dil, LANES), lambda i: (0, i, 0)))
    return pl.pallas_call(
        functools.partial(_combine_kernel, dilations=tuple(dilations), hd=cfg.attn_head_dim),
        grid=(s // ts,),
        in_specs=in_specs,
        out_specs=pl.BlockSpec((ts, gw), lambda i: (i, 0)),
        out_shape=jax.ShapeDtypeStruct((s, gw), BF16),
        compiler_params=_params(("parallel",)),
        name="attn_combine",
    )(*outs, *lses)


def _forward(cfg, x, c, positions, ada_w, ada_b, norm_mix, norm_mlp, w_in, w_alpha_up, b_alpha,
             gla_norm, w_branch_gla, w_branch_attn, w_gate, b_gate, w_out, w_mlp_in, w_mlp_out,
             norm_final):
    bsz, s, d = x.shape
    assert bsz == 1 and d == cfg.d_model
    depth = cfg.depth
    x2 = x.reshape(s, d)
    mod = _ada_mod(c.reshape(d, 1), ada_w, ada_b.reshape(depth, 1, -1))
    dils = [dil for _, dil in cfg.dilated_groups]
    perm_dils = sorted({dil for dil in dils if dil > 1})
    pos = positions.reshape(s)
    pos_all = jnp.concatenate([pos.reshape(s // dil, dil).T.reshape(s) for dil in dils]).reshape(-1, 1)
    cos_t, sin_t = _rope_tables(pos_all, cfg)
    g_mix = norm_mix.reshape(depth, 1, d)
    g_mlp = norm_mlp.reshape(depth, 1, d)
    gw, rank = cfg.gla_width, cfg.gate_rank
    w_in_t = jnp.swapaxes(w_in, 1, 2).astype(BF16)
    w_attn_t = w_in_t[:, gw + rank:]
    w_gate_b = w_gate.astype(BF16)
    b_gate3 = b_gate.reshape(depth, 1, -1)
    w_bgla_b = w_branch_gla.astype(BF16)
    w_battn_b = w_branch_attn.astype(BF16)
    w_out_b = w_out.astype(BF16)
    w_mlp_in_b = w_mlp_in.astype(BF16)
    w_mlp_out_b = w_mlp_out.astype(BF16)

    for l in range(depth):
        w_glr = w_in[l, :, gw:gw + rank]
        w_glr3 = jnp.pad(jnp.concatenate([w_glr] * 3, axis=1), ((0, 0), (0, LANES - 3 * rank))).astype(BF16)
        wa_hi, wa_lo = _split_bf16(w_alpha_up[l], 2)
        w_alpha_cat = jnp.pad(jnp.concatenate([wa_hi, wa_lo, wa_hi], axis=0), ((0, LANES - 3 * rank), (0, 0)))

        hs = _norm_mod(x2, g_mix, mod, l, 0, 1, cfg, dilations=perm_dils)
        h = hs[0]
        h_by_dil = {1: h, **{dil: hp.reshape(s, d) for dil, hp in zip(perm_dils, hs[1:])}}
        proj = _gla_proj(h, w_in_t, l, gw)
        b_cum = _gla_gate(h, w_glr3, w_alpha_cat, b_alpha[l].reshape(1, -1), GLA_BLOCK, cfg)
        o_gla = _gla_block(proj, b_cum, gla_norm[l].reshape(1, -1), GLA_BLOCK, cfg)
        outs, lses = [], []
        for g, (window, dil) in enumerate(cfg.dilated_groups):
            qkv = _attn_proj(h_by_dil[dil], w_attn_t, l, cos_t, sin_t, g, cfg)
            o_g, lse_g = _attn_group(qkv.reshape(dil, s // dil, -1), window, dil, cfg, f"dilated_attn_{g}")
            outs.append(o_g)
            lses.append(lse_g)
        o_attn = _combine(outs, lses, dils, cfg)
        merged = _merge(h, o_gla, o_attn, w_gate_b, b_gate3, w_bgla_b, w_battn_b, l)
        x2 = _mm_resid(merged, w_out_b, x2, mod, l, 2, tk=4096, name="out_proj")

        h = _norm_mod(x2, g_mlp, mod, l, 3, 4, cfg)[0]
        a = _mlp_in(h, w_mlp_in_b, l)
        x2 = _mm_resid(a, w_mlp_out_b, x2, mod, l, 5, tk=2048, name="mlp_out")

    return _final_norm(x2, norm_final.reshape(1, d), cfg).reshape(bsz, s, d)


def kernel(x, c, positions, ada_w, ada_b, norm_mix, norm_mlp, w_in, w_alpha_up, b_alpha, gla_norm,
           w_branch_gla, w_branch_attn, w_gate, b_gate, w_out, w_mlp_in, w_mlp_out, norm_final):
    return _forward(Config(), x, c, positions, ada_w, ada_b, norm_mix, norm_mlp, w_in, w_alpha_up,
                    b_alpha, gla_norm, w_branch_gla, w_branch_attn, w_gate, b_gate, w_out, w_mlp_in,
                    w_mlp_out, norm_final)
```

```python
import functools
from typing import NamedTuple

import jax
import jax.numpy as jnp
from jax import lax
from jax.experimental import pallas as pl
from jax.experimental.pallas import tpu as pltpu

F32 = jnp.float32
BF16 = jnp.bfloat16
LANES = 128
SUBLANES = 8
VMEM_LIMIT = 56 * 1024 * 1024
MM_TILE = 1024
MAX_CONTRACTION = 4096
MERGE_TILE = 512
ROW_TILE = 256
GLA_BLOCK = 256
LOG2_E = 1.4426950408889634
LN_2 = 0.6931471805599453


class Config(NamedTuple):
    d_model: int = 4096
    depth: int = 2
    gla_heads: int = 4
    gate_rank: int = 16
    gate_tau: float = 16.0
    attn_head_dim: int = 128
    attn_heads_per_group: int = 16
    dilated_groups: tuple = ((128, 1), (512, 4), (2048, 16))
    rope_theta: float = 10000.0
    norm_eps: float = 1e-6

    @property
    def gla_dk(self):
        return self.d_model // 2

    @property
    def gla_dv(self):
        return self.d_model

    @property
    def dk_head(self):
        return self.gla_dk // self.gla_heads

    @property
    def dv_head(self):
        return self.gla_dv // self.gla_heads

    @property
    def group_width(self):
        return self.attn_heads_per_group * self.attn_head_dim

    @property
    def attn_width(self):
        return len(self.dilated_groups) * self.group_width

    @property
    def gla_width(self):
        return 2 * self.gla_dk + 2 * self.gla_dv


def _params(sem):
    return pltpu.CompilerParams(dimension_semantics=sem, vmem_limit_bytes=VMEM_LIMIT)


def _sigmoid(x):
    return 1.0 / (1.0 + jnp.exp(-x))


def _dot(a, b):
    return jnp.dot(a, b, preferred_element_type=F32)


def _dot_nt(a, b):
    return lax.dot_general(a, b, (((1,), (1,)), ((), ())), preferred_element_type=F32)


def _dot_tn(a, b):
    return lax.dot_general(a, b, (((0,), (0,)), ((), ())), preferred_element_type=F32)


def _tile(n, pref):
    t = min(n, pref)
    assert n % t == 0, (n, pref)
    return t


def _perm_matrix(size, inner, dtype):
    outer = size // inner
    assert inner * outer == size and inner & (inner - 1) == 0 and outer & (outer - 1) == 0
    i = lax.broadcasted_iota(jnp.int32, (size, size), 0)
    j = lax.broadcasted_iota(jnp.int32, (size, size), 1)
    src = (i & (inner - 1)) * outer + (i >> (inner.bit_length() - 1))
    return (j == src).astype(dtype)


def _ada_kernel(c_ref, w_ref, b_ref, o_ref, *, rows):
    d, tn = w_ref.shape

    def body(i, acc):
        r = pl.multiple_of(i * rows, rows)
        cc = c_ref[pl.ds(r, rows), :]
        cond = cc * _sigmoid(cc)
        p = w_ref[pl.ds(r, rows), :] * cond
        return acc + jnp.sum(p.reshape(rows // SUBLANES, SUBLANES, tn), axis=0)

    acc = lax.fori_loop(0, d // rows, body, jnp.zeros((SUBLANES, tn), F32))
    o_ref[...] = jnp.sum(acc, axis=0, keepdims=True) + b_ref[...]


def _ada_mod(c_col, ada_w, ada_b3):
    depth, d, n = ada_w.shape
    tn = _tile(n, MERGE_TILE)
    rows = _tile(d, ROW_TILE)
    return pl.pallas_call(
        functools.partial(_ada_kernel, rows=rows),
        grid=(depth, n // tn),
        in_specs=[
            pl.BlockSpec((d, 1), lambda l, j: (0, 0)),
            pl.BlockSpec((None, d, tn), lambda l, j: (l, 0, j)),
            pl.BlockSpec((None, 1, tn), lambda l, j: (l, 0, j)),
        ],
        out_specs=pl.BlockSpec((None, 1, tn), lambda l, j: (l, 0, j)),
        out_shape=jax.ShapeDtypeStruct((depth, 1, n), F32),
        compiler_params=_params(("parallel", "parallel")),
        name="ada_mod",
    )(c_col, ada_w, ada_b3)


def _rope_kernel(p_ref, inv_ref, sgn_ref, c_ref, s_ref):
    ang = p_ref[...].astype(F32) * inv_ref[...]
    c_ref[...] = jnp.cos(ang)
    s_ref[...] = jnp.sin(ang) * sgn_ref[...]


def _rope_tables(pos_col, cfg):
    s = pos_col.shape[0]
    hd = cfg.attn_head_dim
    half = hd // 2
    inv = cfg.rope_theta ** (-jnp.arange(half, dtype=F32) / half)
    inv2 = jnp.concatenate([inv, inv]).reshape(1, hd)
    sgn = jnp.concatenate([-jnp.ones((half,), F32), jnp.ones((half,), F32)]).reshape(1, hd)
    ts = _tile(s, MM_TILE)
    spec_t = pl.BlockSpec((ts, hd), lambda i: (i, 0))
    spec_c = pl.BlockSpec((1, hd), lambda i: (0, 0))
    return pl.pallas_call(
        _rope_kernel,
        grid=(s // ts,),
        in_specs=[pl.BlockSpec((ts, 1), lambda i: (i, 0)), spec_c, spec_c],
        out_specs=[spec_t, spec_t],
        out_shape=[jax.ShapeDtypeStruct((s, hd), F32)] * 2,
        compiler_params=_params(("parallel",)),
        name="rope_tables",
    )(pos_col, inv2, sgn)


def _norm_mod_kernel(x_ref, g_ref, sc_ref, sh_ref, o_ref, *rest, eps, dilations):
    x = x_ref[...]
    ms = jnp.mean(x * x, axis=-1, keepdims=True)
    y = x * lax.rsqrt(ms + eps) * g_ref[...]
    hb = (y * (1.0 + sc_ref[...]) + sh_ref[...]).astype(o_ref.dtype)
    o_ref[...] = hb
    ts = hb.shape[0]
    for p_ref, dil in zip(rest, dilations):
        n = ts // dil
        hp = _dot(_perm_matrix(ts, n, hb.dtype), hb).astype(p_ref.dtype)
        for r in range(dil):
            p_ref[r] = hp[r * n:(r + 1) * n, :]


def _norm_mod(x, g3, mod, layer, i_shift, i_scale, cfg, dilations=()):
    s, d = x.shape
    ts = _tile(s, ROW_TILE)
    out_specs = [pl.BlockSpec((ts, d), lambda i: (i, 0))]
    out_shape = [jax.ShapeDtypeStruct((s, d), BF16)]
    for dil in dilations:
        assert ts % (dil * 2 * SUBLANES) == 0
        out_specs.append(pl.BlockSpec((dil, ts // dil, d), lambda i: (0, i, 0)))
        out_shape.append(jax.ShapeDtypeStruct((dil, s // dil, d), BF16))
    return pl.pallas_call(
        functools.partial(_norm_mod_kernel, eps=cfg.norm_eps, dilations=tuple(dilations)),
        grid=(s // ts,),
        in_specs=[
            pl.BlockSpec((ts, d), lambda i: (i, 0)),
            pl.BlockSpec((None, 1, d), lambda i: (layer, 0, 0)),
            pl.BlockSpec((None, 1, d), lambda i: (layer, 0, i_scale)),
            pl.BlockSpec((None, 1, d), lambda i: (layer, 0, i_shift)),
        ],
        out_specs=out_specs,
        out_shape=out_shape,
        compiler_params=_params(("parallel",)),
        name="norm_mod",
    )(x, g3, mod, mod)


def _final_norm_kernel(x_ref, g_ref, o_ref, *, eps):
    x = x_ref[...]
    ms = jnp.mean(x * x, axis=-1, keepdims=True)
    o_ref[...] = x * lax.rsqrt(ms + eps) * g_ref[...]


def _final_norm(x, g2, cfg):
    s, d = x.shape
    ts = _tile(s, ROW_TILE)
    return pl.pallas_call(
        functools.partial(_final_norm_kernel, eps=cfg.norm_eps),
        grid=(s // ts,),
        in_specs=[pl.BlockSpec((ts, d), lambda i: (i, 0)), pl.BlockSpec((1, d), lambda i: (0, 0))],
        out_specs=pl.BlockSpec((ts, d), lambda i: (i, 0)),
        out_shape=jax.ShapeDtypeStruct((s, d), F32),
        compiler_params=_params(("parallel",)),
        name="final_norm",
    )(x, g2)


def _proj_kernel(x_ref, wt_ref, cos_ref, sin_ref, o_ref, *, q_scale, hd):
    acc = _dot_nt(x_ref[...], wt_ref[...])
    sec = pl.program_id(1)
    is_rope = sec < 2
    scale = jnp.where(sec == 0, q_scale, 1.0).astype(F32)
    cos = jnp.where(is_rope, cos_ref[...] * scale, 1.0)
    sin = jnp.where(is_rope, sin_ref[...] * scale, 0.0)
    for c in range(acc.shape[1] // hd):
        a = acc[:, c * hd:(c + 1) * hd]
        o_ref[:, c * hd:(c + 1) * hd] = (a * cos + pltpu.roll(a, hd // 2, 1) * sin).astype(o_ref.dtype)


def _attn_proj(h, wt3, layer, cos_t, sin_t, g, cfg):
    m, k = h.shape
    hd = cfg.attn_head_dim
    gw = cfg.group_width
    tm = _tile(m, MM_TILE)
    tn = _tile(gw, MM_TILE)
    per = gw // tn
    sect = cfg.attn_width // tn
    tbl0 = g * (m // tm)
    return pl.pallas_call(
        functools.partial(_proj_kernel, q_scale=LOG2_E * float(hd) ** -0.5, hd=hd),
        grid=(m // tm, 3, per),
        in_specs=[
            pl.BlockSpec((tm, k), lambda i, s, p: (i, 0)),
            pl.BlockSpec((None, tn, k), lambda i, s, p: (layer, s * sect + g * per + p, 0)),
            pl.BlockSpec((tm, hd), lambda i, s, p: (tbl0 + i, 0)),
            pl.BlockSpec((tm, hd), lambda i, s, p: (tbl0 + i, 0)),
        ],
        out_specs=pl.BlockSpec((tm, tn), lambda i, s, p: (i, s * per + p)),
        out_shape=jax.ShapeDtypeStruct((m, 3 * gw), BF16),
        compiler_params=_params(("parallel", "parallel", "parallel")),
        name=f"attn_proj_{g}",
    )(h, wt3, cos_t, sin_t)


def _mm_plain_nt_kernel(x_ref, wt_ref, o_ref):
    o_ref[...] = _dot_nt(x_ref[...], wt_ref[...]).astype(o_ref.dtype)


def _gla_proj(x, wt3, layer, n):
    m, k = x.shape
    assert wt3.shape[2] == k and n <= wt3.shape[1]
    tm, tn = _tile(m, MM_TILE), _tile(n, MM_TILE)
    return pl.pallas_call(
        _mm_plain_nt_kernel,
        grid=(m // tm, n // tn),
        in_specs=[pl.BlockSpec((tm, k), lambda i, j: (i, 0)),
                  pl.BlockSpec((None, tn, k), lambda i, j: (layer, j, 0))],
        out_specs=pl.BlockSpec((tm, tn), lambda i, j: (i, j)),
        out_shape=jax.ShapeDtypeStruct((m, n), BF16),
        compiler_params=_params(("parallel", "parallel")),
        name="gla_proj",
    )(x, wt3)


def _mm_relu2_kernel(x_ref, w_ref, o_ref):
    a = jnp.maximum(_dot(x_ref[...], w_ref[...]), 0.0)
    o_ref[...] = (a * a).astype(o_ref.dtype)


def _mm_resid_kernel(x_ref, w_ref, r_ref, g_ref, o_ref):
    o_ref[...] = r_ref[...] + g_ref[...] * _dot(x_ref[...], w_ref[...])


def _mm_resid_acc_kernel(x_ref, w_ref, r_ref, g_ref, o_ref, acc_ref):
    kk = pl.program_id(2)

    @pl.when(kk == 0)
    def _():
        acc_ref[...] = jnp.zeros_like(acc_ref)

    acc_ref[...] += _dot(x_ref[...], w_ref[...])

    @pl.when(kk == pl.num_programs(2) - 1)
    def _():
        o_ref[...] = r_ref[...] + g_ref[...] * acc_ref[...]


def _mlp_in(x, w3, layer):
    m, k = x.shape
    n = w3.shape[2]
    assert w3.shape[1] == k
    tm, tn = _tile(m, MM_TILE), _tile(n, MM_TILE)
    return pl.pallas_call(
        _mm_relu2_kernel,
        grid=(m // tm, n // tn),
        in_specs=[pl.BlockSpec((tm, k), lambda i, j: (i, 0)),
                  pl.BlockSpec((None, k, tn), lambda i, j: (layer, 0, j))],
        out_specs=pl.BlockSpec((tm, tn), lambda i, j: (i, j)),
        out_shape=jax.ShapeDtypeStruct((m, n), BF16),
        compiler_params=_params(("parallel", "parallel")),
        name="mlp_in",
    )(x, w3)


def _mm_resid(x, w3, resid, mod, layer, i_gate, *, tk, name):
    m, k = x.shape
    n = w3.shape[2]
    assert w3.shape[1] == k
    tm, tn = _tile(m, MM_TILE), _tile(n, MM_TILE)
    gate_blocks = n // tn
    tk = _tile(k, tk)
    if tk == k:
        return pl.pallas_call(
            _mm_resid_kernel,
            grid=(m // tm, n // tn),
            in_specs=[
                pl.BlockSpec((tm, k), lambda i, j: (i, 0)),
                pl.BlockSpec((None, k, tn), lambda i, j: (layer, 0, j)),
                pl.BlockSpec((tm, tn), lambda i, j: (i, j)),
                pl.BlockSpec((None, 1, tn), lambda i, j: (layer, 0, i_gate * gate_blocks + j)),
            ],
            out_specs=pl.BlockSpec((tm, tn), lambda i, j: (i, j)),
            out_shape=jax.ShapeDtypeStruct((m, n), F32),
            compiler_params=_params(("parallel", "parallel")),
            name=name,
        )(x, w3, resid, mod)
    return pl.pallas_call(
        _mm_resid_acc_kernel,
        grid=(m // tm, n // tn, k // tk),
        in_specs=[
            pl.BlockSpec((tm, tk), lambda i, j, kk: (i, kk)),
            pl.BlockSpec((None, tk, tn), lambda i, j, kk: (layer, kk, j)),
            pl.BlockSpec((tm, tn), lambda i, j, kk: (i, j)),
            pl.BlockSpec((None, 1, tn), lambda i, j, kk: (layer, 0, i_gate * gate_blocks + j)),
        ],
        out_specs=pl.BlockSpec((tm, tn), lambda i, j, kk: (i, j)),
        out_shape=jax.ShapeDtypeStruct((m, n), F32),
        scratch_shapes=[pltpu.VMEM((tm, tn), F32)],
        compiler_params=_params(("parallel", "parallel", "arbitrary")),
        name=name,
    )(x, w3, resid, mod)


def _merge_kernel(h_ref, og_ref, oa_ref, wga_ref, wgb_ref, bga_ref, bgb_ref, wa_ref, wb_ref, o_ref):
    h = h_ref[...]
    ga = _sigmoid(_dot(h, wga_ref[...]) + bga_ref[...])
    gb = _sigmoid(_dot(h, wgb_ref[...]) + bgb_ref[...])
    a = _dot(og_ref[...], wa_ref[...])
    b = _dot(oa_ref[...], wb_ref[...])
    o_ref[...] = (ga * a + gb * b).astype(o_ref.dtype)


def _merge(h, o_gla, o_attn, w_gate3, b_gate3, w_a3, w_b3, layer):
    m, d = h.shape
    tm, tn = _tile(m, MERGE_TILE), _tile(d, MERGE_TILE)
    nb = d // tn
    x_spec = lambda kdim: pl.BlockSpec((tm, kdim), lambda j, i: (i, 0))
    w_spec = lambda kdim, off: pl.BlockSpec((None, kdim, tn), lambda j, i: (layer, 0, j + off))
    b_spec = lambda off: pl.BlockSpec((None, 1, tn), lambda j, i: (layer, 0, j + off))
    return pl.pallas_call(
        _merge_kernel,
        grid=(nb, m // tm),
        in_specs=[
            x_spec(d), x_spec(o_gla.shape[1]), x_spec(o_attn.shape[1]),
            w_spec(d, 0), w_spec(d, nb), b_spec(0), b_spec(nb),
            w_spec(w_a3.shape[1], 0), w_spec(w_b3.shape[1], 0),
        ],
        out_specs=pl.BlockSpec((tm, tn), lambda j, i: (i, j)),
        out_shape=jax.ShapeDtypeStruct((m, d), BF16),
        compiler_params=_params(("parallel", "parallel")),
        name="gated_merge",
    )(h, o_gla, o_attn, w_gate3, w_gate3, b_gate3, b_gate3, w_a3, w_b3)


def _split_bf16(x, pieces):
    out = []
    for _ in range(pieces - 1):
        p = x.astype(BF16)
        out.append(p)
        x = x - p.astype(F32)
    out.append(x.astype(BF16))
    return out


def _gate_kernel(h_ref, wl_ref, wc_ref, ba_ref, b_ref, *, chunk, tau, rank):
    tm = h_ref.shape[0]
    x = _dot(h_ref[...], wl_ref[...])
    hi = x.astype(BF16).astype(F32)
    lane = lax.broadcasted_iota(jnp.int32, x.shape, 1)
    a = jnp.where(lane < 2 * rank, hi, x - hi).astype(BF16)
    logits = _dot(a, wc_ref[...]) + ba_ref[...]
    g = (jnp.minimum(logits, 0.0) - jnp.log(1.0 + jnp.exp(-jnp.abs(logits)))) * (LOG2_E / tau)
    ri = lax.broadcasted_iota(jnp.int32, (tm, tm), 0)
    ci = lax.broadcasted_iota(jnp.int32, (tm, tm), 1)
    tril = jnp.where(jnp.logical_and(ri >= ci, (ri ^ ci) < chunk), 1.0, 0.0).astype(BF16)
    p1, p2, p3 = _split_bf16(g, 3)
    b_ref[...] = _dot(tril, p1) + _dot(tril, p2) + _dot(tril, p3)


def _gla_gate(h, w_glr3, w_alpha_cat, b_alpha2, chunk, cfg):
    s, d = h.shape
    n = w_alpha_cat.shape[1]
    tm = _tile(s, max(ROW_TILE, chunk))
    assert tm % chunk == 0 and chunk & (chunk - 1) == 0 and 3 * cfg.gate_rank <= LANES
    return pl.pallas_call(
        functools.partial(_gate_kernel, chunk=chunk, tau=cfg.gate_tau, rank=cfg.gate_rank),
        grid=(s // tm,),
        in_specs=[
            pl.BlockSpec((tm, d), lambda i: (i, 0)),
            pl.BlockSpec((d, LANES), lambda i: (0, 0)),
            pl.BlockSpec((LANES, n), lambda i: (0, 0)),
            pl.BlockSpec((1, n), lambda i: (0, 0)),
        ],
        out_specs=pl.BlockSpec((tm, n), lambda i: (i, 0)),
        out_shape=jax.ShapeDtypeStruct((s, n), F32),
        compiler_params=_params(("parallel",)),
        name="gla_gate",
    )(h, w_glr3, w_alpha_cat, b_alpha2)


def _gla_block_kernel(q_ref, k_ref, v_ref, gr_ref, b_ref, gn_ref, o_ref, st_ref, *, heads, q_scale, eps):
    rows = q_ref.shape[0]
    dk = q_ref.shape[1] // heads
    dv = v_ref.shape[1] // heads
    sub = SUBLANES
    half = rows // 2

    @pl.when(pl.program_id(1) == 0)
    def _():
        st_ref[...] = jnp.zeros_like(st_ref)

    ri = lax.broadcasted_iota(jnp.int32, (half, half), 0)
    ci = lax.broadcasted_iota(jnp.int32, (half, half), 1)
    sub_i = lax.broadcasted_iota(jnp.int32, (sub, half), 0)
    lane_i = lax.broadcasted_iota(jnp.int32, (sub, half), 1)

    def cross_scores(q, k, b, g_blk):
        n = q.shape[0]
        span = 2 * g_blk
        zero = jnp.zeros((g_blk, dk), F32)
        q_parts, k_parts = [], []
        for s0 in range(0, n, span):
            ref = jnp.broadcast_to(b[s0 + g_blk - 1:s0 + g_blk, :], (g_blk, dk))
            lo, up = slice(s0, s0 + g_blk), slice(s0 + g_blk, s0 + span)
            q_parts += [zero, q[up] * jnp.exp2(b[up] - ref)]
            k_parts += [k[lo] * jnp.exp2(ref - b[lo]), zero]
        qg = jnp.concatenate(q_parts, axis=0).astype(BF16)
        kg = jnp.concatenate(k_parts, axis=0).astype(BF16)
        return _dot_nt(qg, kg)

    for hh in range(heads):
        ksl = slice(hh * dk, (hh + 1) * dk)
        vsl = slice(hh * dv, (hh + 1) * dv)

        q = q_ref[:, ksl].astype(F32) * q_scale
        k = k_ref[:, ksl].astype(F32)
        vb = v_ref[:, vsl]
        b = b_ref[:, ksl]
        b_last = b[rows - 1:rows, :]

        st = st_ref[hh]
        qe = (q * jnp.exp2(b)).astype(BF16)
        o = _dot_nt(qe, st.astype(BF16))

        b_mid = jnp.broadcast_to(b[half - 1:half, :], (half, dk))
        x_lo = _dot_nt((q[half:] * jnp.exp2(b[half:] - b_mid)).astype(BF16),
                       (k[:half] * jnp.exp2(b_mid - b[:half])).astype(BF16))
        a_halves = []
        for hf in range(2):
            rs = slice(hf * half, (hf + 1) * half)
            qh, kh, bh = q[rs], k[rs], b[rs]
            tiles = []
            for lo in range(0, half, sub):
                qi, ki, bi = qh[lo:lo + sub, :], kh[lo:lo + sub, :], bh[lo:lo + sub, :]
                acc = jnp.zeros((sub, half), F32)
                for jj in range(sub):
                    t = qi * ki[jj:jj + 1, :] * jnp.exp2(bi - bi[jj:jj + 1, :])
                    col = jnp.sum(t, axis=1, keepdims=True)
                    keep = jnp.logical_and(lane_i == lo + jj, sub_i >= jj)
                    acc = jnp.where(keep, col, acc)
                tiles.append(acc)
            a = jnp.concatenate(tiles, axis=0) + cross_scores(qh, kh, bh, half // 2)
            g_blk = half // 4
            while g_blk >= sub:
                same = (ri ^ ci) < 2 * g_blk
                a = a + jnp.where(same, cross_scores(qh, kh, bh, g_blk), 0.0)
                g_blk //= 2
            a_halves.append(a)
        top = jnp.concatenate([a_halves[0], jnp.zeros((half, half), F32)], axis=1)
        bot = jnp.concatenate([x_lo, a_halves[1]], axis=1)
        s = jnp.concatenate([top, bot], axis=0).astype(BF16)
        o = o + _dot(s, vb)

        kd = (k * jnp.exp2(b_last - b)).astype(BF16)
        st_ref[hh] = st * jnp.exp2(b_last) + _dot_tn(vb, kd)

        ms = jnp.mean(o * o, axis=-1, keepdims=True)
        y = o * lax.rsqrt(ms + eps) * gn_ref[...]
        gr = gr_ref[:, vsl].astype(F32)
        o_ref[:, vsl] = (y * (gr * _sigmoid(gr))).astype(o_ref.dtype)


def _gla_block(proj, b_cum, gla_norm2, rows, cfg, heads_per_step=2):
    s = proj.shape[0]
    nh, dk, dv = cfg.gla_heads, cfg.dk_head, cfg.dv_head
    hp = heads_per_step
    assert s % rows == 0 and nh % hp == 0 and rows & (rows - 1) == 0 and rows >= 8 * SUBLANES
    assert cfg.gla_dk % (hp * dk) == 0 and (2 * cfg.gla_dk) % (hp * dv) == 0
    k_off = cfg.gla_dk // (hp * dk)
    v_off = (2 * cfg.gla_dk) // (hp * dv)
    r_off = (2 * cfg.gla_dk + cfg.gla_dv) // (hp * dv)
    return pl.pallas_call(
        functools.partial(_gla_block_kernel, heads=hp, q_scale=float(dk) ** -0.5, eps=cfg.norm_eps),
        grid=(nh // hp, s // rows),
        in_specs=[
            pl.BlockSpec((rows, hp * dk), lambda h, t: (t, h)),
            pl.BlockSpec((rows, hp * dk), lambda h, t: (t, k_off + h)),
            pl.BlockSpec((rows, hp * dv), lambda h, t: (t, v_off + h)),
            pl.BlockSpec((rows, hp * dv), lambda h, t: (t, r_off + h)),
            pl.BlockSpec((rows, hp * dk), lambda h, t: (t, h)),
            pl.BlockSpec((1, dv), lambda h, t: (0, 0)),
        ],
        out_specs=pl.BlockSpec((rows, hp * dv), lambda h, t: (t, h)),
        out_shape=jax.ShapeDtypeStruct((s, cfg.gla_dv), BF16),
        scratch_shapes=[pltpu.VMEM((hp, dv, dk), F32)],
        compiler_params=_params(("parallel", "arbitrary")),
        name="gla",
    )(proj, proj, proj, proj, b_cum, gla_norm2)


def _attn_kernel(q_ref, kp_ref, kc_ref, vp_ref, vc_ref, o_ref, lse_ref, s_scr, p_scr, *, heads, hd, blk):
    nq = q_ref.shape[0] // blk
    n = pl.program_id(1)
    qi = lax.broadcasted_iota(jnp.int32, (blk, 2 * blk), 0)
    kj = lax.broadcasted_iota(jnp.int32, (blk, 2 * blk), 1)
    dist = blk + qi - kj
    band = jnp.logical_and(dist >= 0, dist <= blk)
    first = jnp.logical_and(band, jnp.logical_or(kj >= blk, n > 0))
    lane = lax.broadcasted_iota(jnp.int32, (blk, LANES), 1)
    for b in range(nq):
        rows = slice(b * blk, (b + 1) * blk)
        keys = slice((b - 1) * blk, (b + 1) * blk)
        mask = first if b == 0 else band
        slot = b % 2
        for hh in range(heads):
            sl = slice(hh * hd, (hh + 1) * hd)
            q = q_ref[rows, sl]
            if b == 0:
                s = jnp.concatenate([_dot_nt(q, kp_ref[:, sl]), _dot_nt(q, kc_ref[rows, sl])], axis=1)
            else:
                s = _dot_nt(q, kc_ref[keys, sl])
            s_scr[slot, hh] = jnp.where(mask, s, -jnp.inf)
        lse_tile = jnp.zeros((blk, LANES), F32)
        for hh in range(heads):
            s = s_scr[slot, hh]
            m = jnp.max(s, axis=1, keepdims=True)
            p = jnp.exp2(s - m)
            den = jnp.sum(p, axis=1, keepdims=True)
            p_scr[slot, hh] = (p * (1.0 / den)).astype(BF16)
            lse_tile = jnp.where(lane == hh, m * LN_2 + jnp.log(den), lse_tile)
        lse_ref[rows, :] = lse_tile
        for hh in range(heads):
            sl = slice(hh * hd, (hh + 1) * hd)
            p = p_scr[slot, hh]
            if b == 0:
                o = _dot(p[:, :blk], vp_ref[:, sl]) + _dot(p[:, blk:], vc_ref[rows, sl])
            else:
                o = _dot(p, vc_ref[keys, sl])
            o_ref[rows, sl] = o.astype(o_ref.dtype)


def _attn_group(qkv, window, dilation, cfg, name, blocks_per_step=4):
    dil, u, w3 = qkv.shape
    gw = cfg.group_width
    heads = cfg.attn_heads_per_group
    blk = window // dilation
    assert dil == dilation and w3 == 3 * gw and u % blk == 0 and heads <= LANES
    nb = u // blk
    nq = blocks_per_step
    while nb % nq:
        nq //= 2
    rows = nq * blk

    def cur(part):
        return pl.BlockSpec((None, rows, gw), lambda r, n: (r, n, part))

    def prev(part):
        return pl.BlockSpec((None, blk, gw), lambda r, n: (r, jnp.maximum(n * nq - 1, 0), part))

    return pl.pallas_call(
        functools.partial(_attn_kernel, heads=heads, hd=cfg.attn_head_dim, blk=blk),
        grid=(dilation, nb // nq),
        in_specs=[cur(0), prev(1), cur(1), prev(2), cur(2)],
        out_specs=[pl.BlockSpec((None, rows, gw), lambda r, n: (r, n, 0)),
                   pl.BlockSpec((None, rows, LANES), lambda r, n: (r, n, 0))],
        out_shape=[jax.ShapeDtypeStruct((dilation, u, gw), BF16),
                   jax.ShapeDtypeStruct((dilation, u, LANES), F32)],
        scratch_shapes=[pltpu.VMEM((2, heads, blk, 2 * blk), F32), pltpu.VMEM((2, heads, blk, 2 * blk), BF16)],
        compiler_params=_params(("parallel", "arbitrary")),
        name=name,
    )(qkv, qkv, qkv, qkv, qkv)


def _combine_kernel(*refs, dilations, hd):
    ng = len(dilations)
    o_refs, l_refs, out_ref = refs[:ng], refs[ng:2 * ng], refs[2 * ng]
    ts, gw = out_ref.shape
    perms = [_perm_matrix(ts, dil, BF16) if dil > 1 else None for dil in dilations]
    ls = []
    for gi in range(ng):
        l_g = l_refs[gi][...].reshape(ts, LANES)
        if perms[gi] is not None:
            l_g = functools.reduce(lambda a, b: a + b, [_dot(perms[gi], p) for p in _split_bf16(l_g, 3)])
        ls.append(l_g)
    m = functools.reduce(jnp.maximum, ls)
    es = [jnp.exp(l - m) for l in ls]
    inv = 1.0 / functools.reduce(lambda a, b: a + b, es)
    ws = [e * inv for e in es]
    width = 2 * hd if gw % (2 * hd) == 0 else hd
    for c0 in range(0, gw, width):
        o_c = []
        for gi in range(ng):
            o_g = o_refs[gi][:, :, c0:c0 + width].reshape(ts, width)
            o_c.append(_dot(perms[gi], o_g) if perms[gi] is not None else o_g.astype(F32))
        for c in range(c0, c0 + width, hd):
            hh = c // hd
            acc = None
            for gi in range(ng):
                term = ws[gi][:, hh:hh + 1] * o_c[gi][:, c - c0:c - c0 + hd]
                acc = term if acc is None else acc + term
            out_ref[:, c:c + hd] = acc.astype(out_ref.dtype)


def _combine(outs, lses, dilations, cfg):
    gw = cfg.group_width
    s = outs[0].shape[0] * outs[0].shape[1]
    ts = _tile(s, MERGE_TILE)
    in_specs = []
    for dil in dilations:
        assert ts % (dil * 2 * SUBLANES) == 0
        in_specs.append(pl.BlockSpec((dil, ts // dil, gw), lambda i: (0, i, 0)))
    for dil in dilations:
        in_specs.append(pl.BlockSpec((dil, ts // dil, LANES), lambda i: (0, i, 0)))
    return pl.pallas_call(
        functools.partial(_combine_kernel, dilations=tuple(dilations), hd=cfg.attn_head_dim),
        grid=(s // ts,),
        in_specs=in_specs,
        out_specs=pl.BlockSpec((ts, gw), lambda i: (i, 0)),
        out_shape=jax.ShapeDtypeStruct((s, gw), BF16),
        compiler_params=_params(("parallel",)),
        name="attn_combine",
    )(*outs, *lses)


def _forward(cfg, x, c, positions, ada_w, ada_b, norm_mix, norm_mlp, w_in, w_alpha_up, b_alpha,
             gla_norm, w_branch_gla, w_branch_attn, w_gate, b_gate, w_out, w_mlp_in, w_mlp_out,
             norm_final):
    bsz, s, d = x.shape
    assert bsz == 1 and d == cfg.d_model
    depth = cfg.depth
    x2 = x.reshape(s, d)
    mod = _ada_mod(c.reshape(d, 1), ada_w, ada_b.reshape(depth, 1, -1))
    dils = [dil for _, dil in cfg.dilated_groups]
    perm_dils = sorted({dil for dil in dils if dil > 1})
    pos = positions.reshape(s)
    pos_all = jnp.concatenate([pos.reshape(s // dil, dil).T.reshape(s) for dil in dils]).reshape(-1, 1)
    cos_t, sin_t = _rope_tables(pos_all, cfg)
    g_mix = norm_mix.reshape(depth, 1, d)
    g_mlp = norm_mlp.reshape(depth, 1, d)
    gw, rank = cfg.gla_width, cfg.gate_rank
    w_in_t = jnp.swapaxes(w_in, 1, 2).astype(BF16)
    w_attn_t = w_in_t[:, gw + rank:]
    w_gate_b = w_gate.astype(BF16)
    b_gate3 = b_gate.reshape(depth, 1, -1)
    w_bgla_b = w_branch_gla.astype(BF16)
    w_battn_b = w_branch_attn.astype(BF16)
    w_out_b = w_out.astype(BF16)
    w_mlp_in_b = w_mlp_in.astype(BF16)
    w_mlp_out_b = w_mlp_out.astype(BF16)

    for l in range(depth):
        w_glr = w_in[l, :, gw:gw + rank]
        w_glr3 = jnp.pad(jnp.concatenate([w_glr] * 3, axis=1), ((0, 0), (0, LANES - 3 * rank))).astype(BF16)
        wa_hi, wa_lo = _split_bf16(w_alpha_up[l], 2)
        w_alpha_cat = jnp.pad(jnp.concatenate([wa_hi, wa_lo, wa_hi], axis=0), ((0, LANES - 3 * rank), (0, 0)))

        hs = _norm_mod(x2, g_mix, mod, l, 0, 1, cfg, dilations=perm_dils)
        h = hs[0]
        h_by_dil = {1: h, **{dil: hp.reshape(s, d) for dil, hp in zip(perm_dils, hs[1:])}}
        proj = _gla_proj(h, w_in_t, l, gw)
        b_cum = _gla_gate(h, w_glr3, w_alpha_cat, b_alpha[l].reshape(1, -1), GLA_BLOCK, cfg)
        o_gla = _gla_block(proj, b_cum, gla_norm[l].reshape(1, -1), GLA_BLOCK, cfg)
        outs, lses = [], []
        for g, (window, dil) in enumerate(cfg.dilated_groups):
            qkv = _attn_proj(h_by_dil[dil], w_attn_t, l, cos_t, sin_t, g, cfg)
            o_g, lse_g = _attn_group(qkv.reshape(dil, s // dil, -1), window, dil, cfg, f"dilated_attn_{g}")
            outs.append(o_g)
            lses.append(lse_g)
        o_attn = _combine(outs, lses, dils, cfg)
        merged = _merge(h, o_gla, o_attn, w_gate_b, b_gate3, w_bgla_b, w_battn_b, l)
        x2 = _mm_resid(merged, w_out_b, x2, mod, l, 2, tk=MAX_CONTRACTION, name="out_proj")

        h = _norm_mod(x2, g_mlp, mod, l, 3, 4, cfg)[0]
        a = _mlp_in(h, w_mlp_in_b, l)
        x2 = _mm_resid(a, w_mlp_out_b, x2, mod, l, 5, tk=MAX_CONTRACTION // 2, name="mlp_out")

    return _final_norm(x2, norm_final.reshape(1, d), cfg).reshape(bsz, s, d)


def kernel(x, c, positions, ada_w, ada_b, norm_mix, norm_mlp, w_in, w_alpha_up, b_alpha, gla_norm,
           w_branch_gla, w_branch_attn, w_gate, b_gate, w_out, w_mlp_in, w_mlp_out, norm_final):
    return _forward(Config(), x, c, positions, ada_w, ada_b, norm_mix, norm_mlp, w_in, w_alpha_up,
                    b_alpha, gla_norm, w_branch_gla, w_branch_attn, w_gate, b_gate, w_out, w_mlp_in,
                    w_mlp_out, norm_final)
```

```python
import functools
from typing import NamedTuple

import jax
import jax.numpy as jnp
from jax import lax
from jax.experimental import pallas as pl
from jax.experimental.pallas import tpu as pltpu

F32 = jnp.float32
BF16 = jnp.bfloat16
LANES = 128
SUBLANES = 8
VMEM_LIMIT = 56 * 1024 * 1024
MM_TILE = 1024
MAX_CONTRACTION = 4096
MERGE_TILE = 512
ROW_TILE = 256
GLA_BLOCK = 256
LOG2_E = 1.4426950408889634
LN_2 = 0.6931471805599453


class Config(NamedTuple):
    d_model: int = 4096
    depth: int = 2
    gla_heads: int = 4
    gate_rank: int = 16
    gate_tau: float = 16.0
    attn_head_dim: int = 128
    attn_heads_per_group: int = 16
    dilated_groups: tuple = ((128, 1), (512, 4), (2048, 16))
    rope_theta: float = 10000.0
    norm_eps: float = 1e-6

    @property
    def gla_dk(self):
        return self.d_model // 2

    @property
    def gla_dv(self):
        return self.d_model

    @property
    def dk_head(self):
        return self.gla_dk // self.gla_heads

    @property
    def dv_head(self):
        return self.gla_dv // self.gla_heads

    @property
    def group_width(self):
        return self.attn_heads_per_group * self.attn_head_dim

    @property
    def attn_width(self):
        return len(self.dilated_groups) * self.group_width

    @property
    def gla_width(self):
        return 2 * self.gla_dk + 2 * self.gla_dv


def _params(sem):
    return pltpu.CompilerParams(dimension_semantics=sem, vmem_limit_bytes=VMEM_LIMIT)


def _sigmoid(x):
    return 1.0 / (1.0 + jnp.exp(-x))


def _dot(a, b):
    return jnp.dot(a, b, preferred_element_type=F32)


def _dot_nt(a, b):
    return lax.dot_general(a, b, (((1,), (1,)), ((), ())), preferred_element_type=F32)


def _dot_tn(a, b):
    return lax.dot_general(a, b, (((0,), (0,)), ((), ())), preferred_element_type=F32)


def _tile(n, pref):
    t = min(n, pref)
    assert n % t == 0, (n, pref)
    return t


def _perm_matrix(size, inner, dtype):
    outer = size // inner
    assert inner * outer == size and inner & (inner - 1) == 0 and outer & (outer - 1) == 0
    i = lax.broadcasted_iota(jnp.int32, (size, size), 0)
    j = lax.broadcasted_iota(jnp.int32, (size, size), 1)
    src = (i & (inner - 1)) * outer + (i >> (inner.bit_length() - 1))
    return (j == src).astype(dtype)


def _ada_kernel(c_ref, w_ref, b_ref, o_ref, *, rows):
    d, tn = w_ref.shape

    def body(i, acc):
        r = pl.multiple_of(i * rows, rows)
        cc = c_ref[pl.ds(r, rows), :]
        cond = cc * _sigmoid(cc)
        p = w_ref[pl.ds(r, rows), :] * cond
        return acc + jnp.sum(p.reshape(rows // SUBLANES, SUBLANES, tn), axis=0)

    acc = lax.fori_loop(0, d // rows, body, jnp.zeros((SUBLANES, tn), F32))
    o_ref[...] = jnp.sum(acc, axis=0, keepdims=True) + b_ref[...]


def _ada_mod(c_col, ada_w, ada_b3):
    depth, d, n = ada_w.shape
    tn = _tile(n, MERGE_TILE)
    rows = _tile(d, ROW_TILE)
    return pl.pallas_call(
        functools.partial(_ada_kernel, rows=rows),
        grid=(depth, n // tn),
        in_specs=[
            pl.BlockSpec((d, 1), lambda l, j: (0, 0)),
            pl.BlockSpec((None, d, tn), lambda l, j: (l, 0, j)),
            pl.BlockSpec((None, 1, tn), lambda l, j: (l, 0, j)),
        ],
        out_specs=pl.BlockSpec((None, 1, tn), lambda l, j: (l, 0, j)),
        out_shape=jax.ShapeDtypeStruct((depth, 1, n), F32),
        compiler_params=_params(("parallel", "parallel")),
        name="ada_mod",
    )(c_col, ada_w, ada_b3)


def _rope_kernel(p_ref, inv_ref, sgn_ref, c_ref, s_ref):
    ang = p_ref[...].astype(F32) * inv_ref[...]
    c_ref[...] = jnp.cos(ang)
    s_ref[...] = jnp.sin(ang) * sgn_ref[...]


def _rope_tables(pos_col, cfg):
    s = pos_col.shape[0]
    hd = cfg.attn_head_dim
    half = hd // 2
    inv = cfg.rope_theta ** (-jnp.arange(half, dtype=F32) / half)
    inv2 = jnp.concatenate([inv, inv]).reshape(1, hd)
    sgn = jnp.concatenate([-jnp.ones((half,), F32), jnp.ones((half,), F32)]).reshape(1, hd)
    ts = _tile(s, MM_TILE)
    spec_t = pl.BlockSpec((ts, hd), lambda i: (i, 0))
    spec_c = pl.BlockSpec((1, hd), lambda i: (0, 0))
    return pl.pallas_call(
        _rope_kernel,
        grid=(s // ts,),
        in_specs=[pl.BlockSpec((ts, 1), lambda i: (i, 0)), spec_c, spec_c],
        out_specs=[spec_t, spec_t],
        out_shape=[jax.ShapeDtypeStruct((s, hd), F32)] * 2,
        compiler_params=_params(("parallel",)),
        name="rope_tables",
    )(pos_col, inv2, sgn)


def _norm_mod_kernel(x_ref, g_ref, sc_ref, sh_ref, o_ref, *rest, eps, dilations):
    x = x_ref[...]
    ms = jnp.mean(x * x, axis=-1, keepdims=True)
    y = x * lax.rsqrt(ms + eps) * g_ref[...]
    hb = (y * (1.0 + sc_ref[...]) + sh_ref[...]).astype(o_ref.dtype)
    o_ref[...] = hb
    ts = hb.shape[0]
    for p_ref, dil in zip(rest, dilations):
        n = ts // dil
        hp = _dot(_perm_matrix(ts, n, hb.dtype), hb).astype(p_ref.dtype)
        for r in range(dil):
            p_ref[r] = hp[r * n:(r + 1) * n, :]


def _norm_mod(x, g3, mod, layer, i_shift, i_scale, cfg, dilations=()):
    s, d = x.shape
    ts = _tile(s, ROW_TILE)
    out_specs = [pl.BlockSpec((ts, d), lambda i: (i, 0))]
    out_shape = [jax.ShapeDtypeStruct((s, d), BF16)]
    for dil in dilations:
        assert ts % (dil * 2 * SUBLANES) == 0
        out_specs.append(pl.BlockSpec((dil, ts // dil, d), lambda i: (0, i, 0)))
        out_shape.append(jax.ShapeDtypeStruct((dil, s // dil, d), BF16))
    return pl.pallas_call(
        functools.partial(_norm_mod_kernel, eps=cfg.norm_eps, dilations=tuple(dilations)),
        grid=(s // ts,),
        in_specs=[
            pl.BlockSpec((ts, d), lambda i: (i, 0)),
            pl.BlockSpec((None, 1, d), lambda i: (layer, 0, 0)),
            pl.BlockSpec((None, 1, d), lambda i: (layer, 0, i_scale)),
            pl.BlockSpec((None, 1, d), lambda i: (layer, 0, i_shift)),
        ],
        out_specs=out_specs,
        out_shape=out_shape,
        compiler_params=_params(("parallel",)),
        name="norm_mod",
    )(x, g3, mod, mod)


def _final_norm_kernel(x_ref, g_ref, o_ref, *, eps):
    x = x_ref[...]
    ms = jnp.mean(x * x, axis=-1, keepdims=True)
    o_ref[...] = x * lax.rsqrt(ms + eps) * g_ref[...]


def _final_norm(x, g2, cfg):
    s, d = x.shape
    ts = _tile(s, ROW_TILE)
    return pl.pallas_call(
        functools.partial(_final_norm_kernel, eps=cfg.norm_eps),
        grid=(s // ts,),
        in_specs=[pl.BlockSpec((ts, d), lambda i: (i, 0)), pl.BlockSpec((1, d), lambda i: (0, 0))],
        out_specs=pl.BlockSpec((ts, d), lambda i: (i, 0)),
        out_shape=jax.ShapeDtypeStruct((s, d), F32),
        compiler_params=_params(("parallel",)),
        name="final_norm",
    )(x, g2)


def _proj_kernel(x_ref, wt_ref, cos_ref, sin_ref, o_ref, *, q_scale, hd):
    acc = _dot_nt(x_ref[...], wt_ref[...])
    sec = pl.program_id(1)
    is_rope = sec < 2
    scale = jnp.where(sec == 0, q_scale, 1.0).astype(F32)
    cos = jnp.where(is_rope, cos_ref[...] * scale, 1.0)
    sin = jnp.where(is_rope, sin_ref[...] * scale, 0.0)
    for c in range(acc.shape[1] // hd):
        a = acc[:, c * hd:(c + 1) * hd]
        o_ref[:, c * hd:(c + 1) * hd] = (a * cos + pltpu.roll(a, hd // 2, 1) * sin).astype(o_ref.dtype)


def _attn_proj(h, wt3, layer, cos_t, sin_t, g, cfg):
    m, k = h.shape
    hd = cfg.attn_head_dim
    gw = cfg.group_width
    tm = _tile(m, MM_TILE)
    tn = _tile(gw, MM_TILE)
    per = gw // tn
    sect = cfg.attn_width // tn
    tbl0 = g * (m // tm)
    return pl.pallas_call(
        functools.partial(_proj_kernel, q_scale=LOG2_E * float(hd) ** -0.5, hd=hd),
        grid=(m // tm, 3, per),
        in_specs=[
            pl.BlockSpec((tm, k), lambda i, s, p: (i, 0)),
            pl.BlockSpec((None, tn, k), lambda i, s, p: (layer, s * sect + g * per + p, 0)),
            pl.BlockSpec((tm, hd), lambda i, s, p: (tbl0 + i, 0)),
            pl.BlockSpec((tm, hd), lambda i, s, p: (tbl0 + i, 0)),
        ],
        out_specs=pl.BlockSpec((tm, tn), lambda i, s, p: (i, s * per + p)),
        out_shape=jax.ShapeDtypeStruct((m, 3 * gw), BF16),
        compiler_params=_params(("parallel", "parallel", "parallel")),
        name=f"attn_proj_{g}",
    )(h, wt3, cos_t, sin_t)


def _mm_plain_nt_kernel(x_ref, wt_ref, o_ref):
    o_ref[...] = _dot_nt(x_ref[...], wt_ref[...]).astype(o_ref.dtype)


def _gla_proj(x, wt3, layer, n):
    m, k = x.shape
    assert wt3.shape[2] == k and n <= wt3.shape[1]
    tm, tn = _tile(m, MM_TILE), _tile(n, MM_TILE)
    return pl.pallas_call(
        _mm_plain_nt_kernel,
        grid=(m // tm, n // tn),
        in_specs=[pl.BlockSpec((tm, k), lambda i, j: (i, 0)),
                  pl.BlockSpec((None, tn, k), lambda i, j: (layer, j, 0))],
        out_specs=pl.BlockSpec((tm, tn), lambda i, j: (i, j)),
        out_shape=jax.ShapeDtypeStruct((m, n), BF16),
        compiler_params=_params(("parallel", "parallel")),
        name="gla_proj",
    )(x, wt3)


def _mm_relu2_kernel(x_ref, w_ref, o_ref):
    a = jnp.maximum(_dot(x_ref[...], w_ref[...]), 0.0)
    o_ref[...] = (a * a).astype(o_ref.dtype)


def _mm_resid_kernel(x_ref, w_ref, r_ref, g_ref, o_ref):
    o_ref[...] = r_ref[...] + g_ref[...] * _dot(x_ref[...], w_ref[...])


def _mm_resid_acc_kernel(x_ref, w_ref, r_ref, g_ref, o_ref, acc_ref):
    kk = pl.program_id(2)

    @pl.when(kk == 0)
    def _():
        acc_ref[...] = jnp.zeros_like(acc_ref)

    acc_ref[...] += _dot(x_ref[...], w_ref[...])

    @pl.when(kk == pl.num_programs(2) - 1)
    def _():
        o_ref[...] = r_ref[...] + g_ref[...] * acc_ref[...]


def _mlp_in(x, w3, layer):
    m, k = x.shape
    n = w3.shape[2]
    assert w3.shape[1] == k
    tm, tn = _tile(m, MM_TILE), _tile(n, MM_TILE)
    return pl.pallas_call(
        _mm_relu2_kernel,
        grid=(m // tm, n // tn),
        in_specs=[pl.BlockSpec((tm, k), lambda i, j: (i, 0)),
                  pl.BlockSpec((None, k, tn), lambda i, j: (layer, 0, j))],
        out_specs=pl.BlockSpec((tm, tn), lambda i, j: (i, j)),
        out_shape=jax.ShapeDtypeStruct((m, n), BF16),
        compiler_params=_params(("parallel", "parallel")),
        name="mlp_in",
    )(x, w3)


def _mm_resid(x, w3, resid, mod, layer, i_gate, *, tk, name):
    m, k = x.shape
    n = w3.shape[2]
    assert w3.shape[1] == k
    tm, tn = _tile(m, MM_TILE), _tile(n, MM_TILE)
    gate_blocks = n // tn
    tk = _tile(k, tk)
    if tk == k:
        return pl.pallas_call(
            _mm_resid_kernel,
            grid=(m // tm, n // tn),
            in_specs=[
                pl.BlockSpec((tm, k), lambda i, j: (i, 0)),
                pl.BlockSpec((None, k, tn), lambda i, j: (layer, 0, j)),
                pl.BlockSpec((tm, tn), lambda i, j: (i, j)),
                pl.BlockSpec((None, 1, tn), lambda i, j: (layer, 0, i_gate * gate_blocks + j)),
            ],
            out_specs=pl.BlockSpec((tm, tn), lambda i, j: (i, j)),
            out_shape=jax.ShapeDtypeStruct((m, n), F32),
            compiler_params=_params(("parallel", "parallel")),
            name=name,
        )(x, w3, resid, mod)
    return pl.pallas_call(
        _mm_resid_acc_kernel,
        grid=(m // tm, n // tn, k // tk),
        in_specs=[
            pl.BlockSpec((tm, tk), lambda i, j, kk: (i, kk)),
            pl.BlockSpec((None, tk, tn), lambda i, j, kk: (layer, kk, j)),
            pl.BlockSpec((tm, tn), lambda i, j, kk: (i, j)),
            pl.BlockSpec((None, 1, tn), lambda i, j, kk: (layer, 0, i_gate * gate_blocks + j)),
        ],
        out_specs=pl.BlockSpec((tm, tn), lambda i, j, kk: (i, j)),
        out_shape=jax.ShapeDtypeStruct((m, n), F32),
        scratch_shapes=[pltpu.VMEM((tm, tn), F32)],
        compiler_params=_params(("parallel", "parallel", "arbitrary")),
        name=name,
    )(x, w3, resid, mod)


def _merge_kernel(h_ref, og_ref, oa_ref, wga_ref, wgb_ref, bga_ref, bgb_ref, wa_ref, wb_ref, o_ref):
    h = h_ref[...]
    ga = _sigmoid(_dot(h, wga_ref[...]) + bga_ref[...])
    gb = _sigmoid(_dot(h, wgb_ref[...]) + bgb_ref[...])
    a = _dot(og_ref[...], wa_ref[...])
    b = _dot(oa_ref[...], wb_ref[...])
    o_ref[...] = (ga * a + gb * b).astype(o_ref.dtype)


def _merge(h, o_gla, o_attn, w_gate3, b_gate3, w_a3, w_b3, layer):
    m, d = h.shape
    tm, tn = _tile(m, MERGE_TILE), _tile(d, MERGE_TILE)
    nb = d // tn
    x_spec = lambda kdim: pl.BlockSpec((tm, kdim), lambda j, i: (i, 0))
    w_spec = lambda kdim, off: pl.BlockSpec((None, kdim, tn), lambda j, i: (layer, 0, j + off))
    b_spec = lambda off: pl.BlockSpec((None, 1, tn), lambda j, i: (layer, 0, j + off))
    return pl.pallas_call(
        _merge_kernel,
        grid=(nb, m // tm),
        in_specs=[
            x_spec(d), x_spec(o_gla.shape[1]), x_spec(o_attn.shape[1]),
            w_spec(d, 0), w_spec(d, nb), b_spec(0), b_spec(nb),
            w_spec(w_a3.shape[1], 0), w_spec(w_b3.shape[1], 0),
        ],
        out_specs=pl.BlockSpec((tm, tn), lambda j, i: (i, j)),
        out_shape=jax.ShapeDtypeStruct((m, d), BF16),
        compiler_params=_params(("parallel", "parallel")),
        name="gated_merge",
    )(h, o_gla, o_attn, w_gate3, w_gate3, b_gate3, b_gate3, w_a3, w_b3)


def _split_bf16(x, pieces):
    out = []
    for _ in range(pieces - 1):
        p = x.astype(BF16)
        out.append(p)
        x = x - p.astype(F32)
    out.append(x.astype(BF16))
    return out


def _gate_kernel(h_ref, wl_ref, wc_ref, ba_ref, b_ref, *, chunk, tau, rank):
    tm = h_ref.shape[0]
    x = _dot(h_ref[...], wl_ref[...])
    hi = x.astype(BF16).astype(F32)
    lane = lax.broadcasted_iota(jnp.int32, x.shape, 1)
    a = jnp.where(lane < 2 * rank, hi, x - hi).astype(BF16)
    logits = _dot(a, wc_ref[...]) + ba_ref[...]
    g = (jnp.minimum(logits, 0.0) - jnp.log(1.0 + jnp.exp(-jnp.abs(logits)))) * (LOG2_E / tau)
    ri = lax.broadcasted_iota(jnp.int32, (tm, tm), 0)
    ci = lax.broadcasted_iota(jnp.int32, (tm, tm), 1)
    tril = jnp.where(jnp.logical_and(ri >= ci, (ri ^ ci) < chunk), 1.0, 0.0).astype(BF16)
    p1, p2, p3 = _split_bf16(g, 3)
    b_ref[...] = _dot(tril, p1) + _dot(tril, p2) + _dot(tril, p3)


def _gla_gate(h, w_glr3, w_alpha_cat, b_alpha2, chunk, cfg):
    s, d = h.shape
    n = w_alpha_cat.shape[1]
    tm = _tile(s, max(ROW_TILE, chunk))
    assert tm % chunk == 0 and chunk & (chunk - 1) == 0 and 3 * cfg.gate_rank <= LANES
    return pl.pallas_call(
        functools.partial(_gate_kernel, chunk=chunk, tau=cfg.gate_tau, rank=cfg.gate_rank),
        grid=(s // tm,),
        in_specs=[
            pl.BlockSpec((tm, d), lambda i: (i, 0)),
            pl.BlockSpec((d, LANES), lambda i: (0, 0)),
            pl.BlockSpec((LANES, n), lambda i: (0, 0)),
            pl.BlockSpec((1, n), lambda i: (0, 0)),
        ],
        out_specs=pl.BlockSpec((tm, n), lambda i: (i, 0)),
        out_shape=jax.ShapeDtypeStruct((s, n), F32),
        compiler_params=_params(("parallel",)),
        name="gla_gate",
    )(h, w_glr3, w_alpha_cat, b_alpha2)


def _gla_block_kernel(q_ref, k_ref, v_ref, gr_ref, b_ref, gn_ref, o_ref, st_ref, *, heads, q_scale, eps):
    rows = q_ref.shape[0]
    dk = q_ref.shape[1] // heads
    dv = v_ref.shape[1] // heads
    sub = SUBLANES
    half = rows // 2

    @pl.when(pl.program_id(1) == 0)
    def _():
        st_ref[...] = jnp.zeros_like(st_ref)

    ri = lax.broadcasted_iota(jnp.int32, (half, half), 0)
    ci = lax.broadcasted_iota(jnp.int32, (half, half), 1)
    sub_i = lax.broadcasted_iota(jnp.int32, (sub, half), 0)
    lane_i = lax.broadcasted_iota(jnp.int32, (sub, half), 1)

    def cross_scores(q, k, b, g_blk):
        n = q.shape[0]
        span = 2 * g_blk
        zero = jnp.zeros((g_blk, dk), F32)
        q_parts, k_parts = [], []
        for s0 in range(0, n, span):
            ref = jnp.broadcast_to(b[s0 + g_blk - 1:s0 + g_blk, :], (g_blk, dk))
            lo, up = slice(s0, s0 + g_blk), slice(s0 + g_blk, s0 + span)
            q_parts += [zero, q[up] * jnp.exp2(b[up] - ref)]
            k_parts += [k[lo] * jnp.exp2(ref - b[lo]), zero]
        qg = jnp.concatenate(q_parts, axis=0).astype(BF16)
        kg = jnp.concatenate(k_parts, axis=0).astype(BF16)
        return _dot_nt(qg, kg)

    for hh in range(heads):
        ksl = slice(hh * dk, (hh + 1) * dk)
        vsl = slice(hh * dv, (hh + 1) * dv)

        q = q_ref[:, ksl].astype(F32) * q_scale
        k = k_ref[:, ksl].astype(F32)
        vb = v_ref[:, vsl]
        b = b_ref[:, ksl]
        b_last = b[rows - 1:rows, :]

        st = st_ref[hh]
        qe = (q * jnp.exp2(b)).astype(BF16)
        o = _dot_nt(qe, st.astype(BF16))

        b_mid = jnp.broadcast_to(b[half - 1:half, :], (half, dk))
        x_lo = _dot_nt((q[half:] * jnp.exp2(b[half:] - b_mid)).astype(BF16),
                       (k[:half] * jnp.exp2(b_mid - b[:half])).astype(BF16))
        a_halves = []
        for hf in range(2):
            rs = slice(hf * half, (hf + 1) * half)
            qh, kh, bh = q[rs], k[rs], b[rs]
            tiles = []
            for lo in range(0, half, sub):
                qi, ki, bi = qh[lo:lo + sub, :], kh[lo:lo + sub, :], bh[lo:lo + sub, :]
                acc = jnp.zeros((sub, half), F32)
                for jj in range(sub):
                    t = qi * ki[jj:jj + 1, :] * jnp.exp2(bi - bi[jj:jj + 1, :])
                    col = jnp.sum(t, axis=1, keepdims=True)
                    keep = jnp.logical_and(lane_i == lo + jj, sub_i >= jj)
                    acc = jnp.where(keep, col, acc)
                tiles.append(acc)
            a = jnp.concatenate(tiles, axis=0) + cross_scores(qh, kh, bh, half // 2)
            g_blk = half // 4
            while g_blk >= sub:
                same = (ri ^ ci) < 2 * g_blk
                a = a + jnp.where(same, cross_scores(qh, kh, bh, g_blk), 0.0)
                g_blk //= 2
            a_halves.append(a)
        top = jnp.concatenate([a_halves[0], jnp.zeros((half, half), F32)], axis=1)
        bot = jnp.concatenate([x_lo, a_halves[1]], axis=1)
        s = jnp.concatenate([top, bot], axis=0).astype(BF16)
        o = o + _dot(s, vb)

        kd = (k * jnp.exp2(b_last - b)).astype(BF16)
        st_ref[hh] = st * jnp.exp2(b_last) + _dot_tn(vb, kd)

        ms = jnp.mean(o * o, axis=-1, keepdims=True)
        y = o * lax.rsqrt(ms + eps) * gn_ref[...]
        gr = gr_ref[:, vsl].astype(F32)
        o_ref[:, vsl] = (y * (gr * _sigmoid(gr))).astype(o_ref.dtype)


def _gla_block(proj, b_cum, gla_norm2, rows, cfg, heads_per_step=4):
    s = proj.shape[0]
    nh, dk, dv = cfg.gla_heads, cfg.dk_head, cfg.dv_head
    hp = heads_per_step
    assert s % rows == 0 and nh % hp == 0 and rows & (rows - 1) == 0 and rows >= 8 * SUBLANES
    assert cfg.gla_dk % (hp * dk) == 0 and (2 * cfg.gla_dk) % (hp * dv) == 0
    k_off = cfg.gla_dk // (hp * dk)
    v_off = (2 * cfg.gla_dk) // (hp * dv)
    r_off = (2 * cfg.gla_dk + cfg.gla_dv) // (hp * dv)
    return pl.pallas_call(
        functools.partial(_gla_block_kernel, heads=hp, q_scale=float(dk) ** -0.5, eps=cfg.norm_eps),
        grid=(nh // hp, s // rows),
        in_specs=[
            pl.BlockSpec((rows, hp * dk), lambda h, t: (t, h)),
            pl.BlockSpec((rows, hp * dk), lambda h, t: (t, k_off + h)),
            pl.BlockSpec((rows, hp * dv), lambda h, t: (t, v_off + h)),
            pl.BlockSpec((rows, hp * dv), lambda h, t: (t, r_off + h)),
            pl.BlockSpec((rows, hp * dk), lambda h, t: (t, h)),
            pl.BlockSpec((1, dv), lambda h, t: (0, 0)),
        ],
        out_specs=pl.BlockSpec((rows, hp * dv), lambda h, t: (t, h)),
        out_shape=jax.ShapeDtypeStruct((s, cfg.gla_dv), BF16),
        scratch_shapes=[pltpu.VMEM((hp, dv, dk), F32)],
        compiler_params=_params(("parallel", "arbitrary")),
        name="gla",
    )(proj, proj, proj, proj, b_cum, gla_norm2)


def _attn_kernel(q_ref, kp_ref, kc_ref, vp_ref, vc_ref, o_ref, lse_ref, s_scr, p_scr, *, heads, hd, blk):
    nq = q_ref.shape[0] // blk
    n = pl.program_id(1)
    qi = lax.broadcasted_iota(jnp.int32, (blk, 2 * blk), 0)
    kj = lax.broadcasted_iota(jnp.int32, (blk, 2 * blk), 1)
    dist = blk + qi - kj
    band = jnp.logical_and(dist >= 0, dist <= blk)
    first = jnp.logical_and(band, jnp.logical_or(kj >= blk, n > 0))
    lane = lax.broadcasted_iota(jnp.int32, (blk, LANES), 1)
    for b in range(nq):
        rows = slice(b * blk, (b + 1) * blk)
        keys = slice((b - 1) * blk, (b + 1) * blk)
        mask = first if b == 0 else band
        slot = b % 2
        for hh in range(heads):
            sl = slice(hh * hd, (hh + 1) * hd)
            q = q_ref[rows, sl]
            if b == 0:
                s = jnp.concatenate([_dot_nt(q, kp_ref[:, sl]), _dot_nt(q, kc_ref[rows, sl])], axis=1)
            else:
                s = _dot_nt(q, kc_ref[keys, sl])
            s_scr[slot, hh] = jnp.where(mask, s, -jnp.inf)
        lse_tile = jnp.zeros((blk, LANES), F32)
        for hh in range(heads):
            s = s_scr[slot, hh]
            m = jnp.max(s, axis=1, keepdims=True)
            p = jnp.exp2(s - m)
            den = jnp.sum(p, axis=1, keepdims=True)
            p_scr[slot, hh] = (p * (1.0 / den)).astype(BF16)
            lse_tile = jnp.where(lane == hh, m * LN_2 + jnp.log(den), lse_tile)
        lse_ref[rows, :] = lse_tile
        for hh in range(heads):
            sl = slice(hh * hd, (hh + 1) * hd)
            p = p_scr[slot, hh]
            if b == 0:
                o = _dot(p[:, :blk], vp_ref[:, sl]) + _dot(p[:, blk:], vc_ref[rows, sl])
            else:
                o = _dot(p, vc_ref[keys, sl])
            o_ref[rows, sl] = o.astype(o_ref.dtype)


def _attn_group(qkv, window, dilation, cfg, name, blocks_per_step=8):
    dil, u, w3 = qkv.shape
    gw = cfg.group_width
    heads = cfg.attn_heads_per_group
    blk = window // dilation
    assert dil == dilation and w3 == 3 * gw and u % blk == 0 and heads <= LANES
    nb = u // blk
    nq = blocks_per_step
    while nb % nq:
        nq //= 2
    rows = nq * blk

    def cur(part):
        return pl.BlockSpec((None, rows, gw), lambda r, n: (r, n, part))

    def prev(part):
        return pl.BlockSpec((None, blk, gw), lambda r, n: (r, jnp.maximum(n * nq - 1, 0), part))

    return pl.pallas_call(
        functools.partial(_attn_kernel, heads=heads, hd=cfg.attn_head_dim, blk=blk),
        grid=(dilation, nb // nq),
        in_specs=[cur(0), prev(1), cur(1), prev(2), cur(2)],
        out_specs=[pl.BlockSpec((None, rows, gw), lambda r, n: (r, n, 0)),
                   pl.BlockSpec((None, rows, LANES), lambda r, n: (r, n, 0))],
        out_shape=[jax.ShapeDtypeStruct((dilation, u, gw), BF16),
                   jax.ShapeDtypeStruct((dilation, u, LANES), F32)],
        scratch_shapes=[pltpu.VMEM((2, heads, blk, 2 * blk), F32), pltpu.VMEM((2, heads, blk, 2 * blk), BF16)],
        compiler_params=_params(("parallel", "arbitrary")),
        name=name,
    )(qkv, qkv, qkv, qkv, qkv)


def _combine_kernel(*refs, dilations, hd):
    ng = len(dilations)
    o_refs, l_refs, out_ref = refs[:ng], refs[ng:2 * ng], refs[2 * ng]
    ts, gw = out_ref.shape
    perms = [_perm_matrix(ts, dil, BF16) if dil > 1 else None for dil in dilations]
    ls = []
    for gi in range(ng):
        l_g = l_refs[gi][...].reshape(ts, LANES)
        if perms[gi] is not None:
            l_g = functools.reduce(lambda a, b: a + b, [_dot(perms[gi], p) for p in _split_bf16(l_g, 3)])
        ls.append(l_g)
    m = functools.reduce(jnp.maximum, ls)
    es = [jnp.exp(l - m) for l in ls]
    inv = 1.0 / functools.reduce(lambda a, b: a + b, es)
    ws = [e * inv for e in es]
    width = 2 * hd if gw % (2 * hd) == 0 else hd
    for c0 in range(0, gw, width):
        o_c = []
        for gi in range(ng):
            o_g = o_refs[gi][:, :, c0:c0 + width].reshape(ts, width)
            o_c.append(_dot(perms[gi], o_g) if perms[gi] is not None else o_g.astype(F32))
        for c in range(c0, c0 + width, hd):
            hh = c // hd
            acc = None
            for gi in range(ng):
                term = ws[gi][:, hh:hh + 1] * o_c[gi][:, c - c0:c - c0 + hd]
                acc = term if acc is None else acc + term
            out_ref[:, c:c + hd] = acc.astype(out_ref.dtype)


def _combine(outs, lses, dilations, cfg):
    gw = cfg.group_width
    s = outs[0].shape[0] * outs[0].shape[1]
    ts = _tile(s, MERGE_TILE)
    in_specs = []
    for dil in dilations:
        assert ts % (dil * 2 * SUBLANES) == 0
        in_specs.append(pl.BlockSpec((dil, ts // dil, gw), lambda i: (0, i, 0)))
    for dil in dilations:
        in_specs.append(pl.BlockSpec((dil, ts // dil, LANES), lambda i: (0, i, 0)))
    return pl.pallas_call(
        functools.partial(_combine_kernel, dilations=tuple(dilations), hd=cfg.attn_head_dim),
        grid=(s // ts,),
        in_specs=in_specs,
        out_specs=pl.BlockSpec((ts, gw), lambda i: (i, 0)),
        out_shape=jax.ShapeDtypeStruct((s, gw), BF16),
        compiler_params=_params(("parallel",)),
        name="attn_combine",
    )(*outs, *lses)


def _forward(cfg, x, c, positions, ada_w, ada_b, norm_mix, norm_mlp, w_in, w_alpha_up, b_alpha,
             gla_norm, w_branch_gla, w_branch_attn, w_gate, b_gate, w_out, w_mlp_in, w_mlp_out,
             norm_final):
    bsz, s, d = x.shape
    assert bsz == 1 and d == cfg.d_model
    depth = cfg.depth
    x2 = x.reshape(s, d)
    mod = _ada_mod(c.reshape(d, 1), ada_w, ada_b.reshape(depth, 1, -1))
    dils = [dil for _, dil in cfg.dilated_groups]
    perm_dils = sorted({dil for dil in dils if dil > 1})
    pos = positions.reshape(s)
    pos_all = jnp.concatenate([pos.reshape(s // dil, dil).T.reshape(s) for dil in dils]).reshape(-1, 1)
    cos_t, sin_t = _rope_tables(pos_all, cfg)
    g_mix = norm_mix.reshape(depth, 1, d)
    g_mlp = norm_mlp.reshape(depth, 1, d)
    gw, rank = cfg.gla_width, cfg.gate_rank
    w_in_t = jnp.swapaxes(w_in, 1, 2).astype(BF16)
    w_attn_t = w_in_t[:, gw + rank:]
    w_gate_b = w_gate.astype(BF16)
    b_gate3 = b_gate.reshape(depth, 1, -1)
    w_bgla_b = w_branch_gla.astype(BF16)
    w_battn_b = w_branch_attn.astype(BF16)
    w_out_b = w_out.astype(BF16)
    w_mlp_in_b = w_mlp_in.astype(BF16)
    w_mlp_out_b = w_mlp_out.astype(BF16)

    for l in range(depth):
        w_glr = w_in[l, :, gw:gw + rank]
        w_glr3 = jnp.pad(jnp.concatenate([w_glr] * 3, axis=1), ((0, 0), (0, LANES - 3 * rank))).astype(BF16)
        wa_hi, wa_lo = _split_bf16(w_alpha_up[l], 2)
        w_alpha_cat = jnp.pad(jnp.concatenate([wa_hi, wa_lo, wa_hi], axis=0), ((0, LANES - 3 * rank), (0, 0)))

        hs = _norm_mod(x2, g_mix, mod, l, 0, 1, cfg, dilations=perm_dils)
        h = hs[0]
        h_by_dil = {1: h, **{dil: hp.reshape(s, d) for dil, hp in zip(perm_dils, hs[1:])}}
        proj = _gla_proj(h, w_in_t, l, gw)
        b_cum = _gla_gate(h, w_glr3, w_alpha_cat, b_alpha[l].reshape(1, -1), GLA_BLOCK, cfg)
        o_gla = _gla_block(proj, b_cum, gla_norm[l].reshape(1, -1), GLA_BLOCK, cfg)
        outs, lses = [], []
        for g, (window, dil) in enumerate(cfg.dilated_groups):
            qkv = _attn_proj(h_by_dil[dil], w_attn_t, l, cos_t, sin_t, g, cfg)
            o_g, lse_g = _attn_group(qkv.reshape(dil, s // dil, -1), window, dil, cfg, f"dilated_attn_{g}")
            outs.append(o_g)
            lses.append(lse_g)
        o_attn = _combine(outs, lses, dils, cfg)
        merged = _merge(h, o_gla, o_attn, w_gate_b, b_gate3, w_bgla_b, w_battn_b, l)
        x2 = _mm_resid(merged, w_out_b, x2, mod, l, 2, tk=MAX_CONTRACTION, name="out_proj")

        h = _norm_mod(x2, g_mlp, mod, l, 3, 4, cfg)[0]
        a = _mlp_in(h, w_mlp_in_b, l)
        x2 = _mm_resid(a, w_mlp_out_b, x2, mod, l, 5, tk=MAX_CONTRACTION // 2, name="mlp_out")

    return _final_norm(x2, norm_final.reshape(1, d), cfg).reshape(bsz, s, d)


def kernel(x, c, positions, ada_w, ada_b, norm_mix, norm_mlp, w_in, w_alpha_up, b_alpha, gla_norm,
           w_branch_gla, w_branch_attn, w_gate, b_gate, w_out, w_mlp_in, w_mlp_out, norm_final):
    return _forward(Config(), x, c, positions, ada_w, ada_b, norm_mix, norm_mlp, w_in, w_alpha_up,
                    b_alpha, gla_norm, w_branch_gla, w_branch_attn, w_gate, b_gate, w_out, w_mlp_in,
                    w_mlp_out, norm_final)
```

```python
import functools
from typing import NamedTuple

import jax
import jax.numpy as jnp
from jax import lax
from jax.experimental import pallas as pl
from jax.experimental.pallas import tpu as pltpu

F32 = jnp.float32
BF16 = jnp.bfloat16
LANES = 128
SUBLANES = 8
VMEM_LIMIT = 56 * 1024 * 1024
MM_TILE = 1024
MAX_CONTRACTION = 4096
MERGE_TILE = 512
ROW_TILE = 256
GLA_BLOCK = 256
LOG2_E = 1.4426950408889634
LN_2 = 0.6931471805599453


class Config(NamedTuple):
    d_model: int = 4096
    depth: int = 2
    gla_heads: int = 4
    gate_rank: int = 16
    gate_tau: float = 16.0
    attn_head_dim: int = 128
    attn_heads_per_group: int = 16
    dilated_groups: tuple = ((128, 1), (512, 4), (2048, 16))
    rope_theta: float = 10000.0
    norm_eps: float = 1e-6

    @property
    def gla_dk(self):
        return self.d_model // 2

    @property
    def gla_dv(self):
        return self.d_model

    @property
    def dk_head(self):
        return self.gla_dk // self.gla_heads

    @property
    def dv_head(self):
        return self.gla_dv // self.gla_heads

    @property
    def group_width(self):
        return self.attn_heads_per_group * self.attn_head_dim

    @property
    def attn_width(self):
        return len(self.dilated_groups) * self.group_width

    @property
    def gla_width(self):
        return 2 * self.gla_dk + 2 * self.gla_dv


def _params(sem):
    return pltpu.CompilerParams(dimension_semantics=sem, vmem_limit_bytes=VMEM_LIMIT)


def _sigmoid(x):
    return 1.0 / (1.0 + jnp.exp(-x))


def _dot(a, b):
    return jnp.dot(a, b, preferred_element_type=F32)


def _dot_nt(a, b):
    return lax.dot_general(a, b, (((1,), (1,)), ((), ())), preferred_element_type=F32)


def _dot_tn(a, b):
    return lax.dot_general(a, b, (((0,), (0,)), ((), ())), preferred_element_type=F32)


def _tile(n, pref):
    t = min(n, pref)
    assert n % t == 0, (n, pref)
    return t


def _perm_matrix(size, inner, dtype):
    outer = size // inner
    assert inner * outer == size and inner & (inner - 1) == 0 and outer & (outer - 1) == 0
    i = lax.broadcasted_iota(jnp.int32, (size, size), 0)
    j = lax.broadcasted_iota(jnp.int32, (size, size), 1)
    src = (i & (inner - 1)) * outer + (i >> (inner.bit_length() - 1))
    return (j == src).astype(dtype)


def _ada_kernel(c_ref, w_ref, b_ref, o_ref, *, rows):
    d, tn = w_ref.shape

    def body(i, acc):
        r = pl.multiple_of(i * rows, rows)
        cc = c_ref[pl.ds(r, rows), :]
        cond = cc * _sigmoid(cc)
        p = w_ref[pl.ds(r, rows), :] * cond
        return acc + jnp.sum(p.reshape(rows // SUBLANES, SUBLANES, tn), axis=0)

    acc = lax.fori_loop(0, d // rows, body, jnp.zeros((SUBLANES, tn), F32))
    o_ref[...] = jnp.sum(acc, axis=0, keepdims=True) + b_ref[...]


def _ada_mod(c_col, ada_w, ada_b3):
    depth, d, n = ada_w.shape
    tn = _tile(n, MM_TILE)
    rows = _tile(d, ROW_TILE)
    return pl.pallas_call(
        functools.partial(_ada_kernel, rows=rows),
        grid=(depth, n // tn),
        in_specs=[
            pl.BlockSpec((d, 1), lambda l, j: (0, 0)),
            pl.BlockSpec((None, d, tn), lambda l, j: (l, 0, j)),
            pl.BlockSpec((None, 1, tn), lambda l, j: (l, 0, j)),
        ],
        out_specs=pl.BlockSpec((None, 1, tn), lambda l, j: (l, 0, j)),
        out_shape=jax.ShapeDtypeStruct((depth, 1, n), F32),
        compiler_params=_params(("parallel", "parallel")),
        name="ada_mod",
    )(c_col, ada_w, ada_b3)


def _rope_kernel(p_ref, inv_ref, sgn_ref, c_ref, s_ref):
    ang = p_ref[...].astype(F32) * inv_ref[...]
    c_ref[...] = jnp.cos(ang)
    s_ref[...] = jnp.sin(ang) * sgn_ref[...]


def _rope_tables(pos_col, cfg):
    s = pos_col.shape[0]
    hd = cfg.attn_head_dim
    half = hd // 2
    inv = cfg.rope_theta ** (-jnp.arange(half, dtype=F32) / half)
    inv2 = jnp.concatenate([inv, inv]).reshape(1, hd)
    sgn = jnp.concatenate([-jnp.ones((half,), F32), jnp.ones((half,), F32)]).reshape(1, hd)
    ts = _tile(s, MM_TILE)
    spec_t = pl.BlockSpec((ts, hd), lambda i: (i, 0))
    spec_c = pl.BlockSpec((1, hd), lambda i: (0, 0))
    return pl.pallas_call(
        _rope_kernel,
        grid=(s // ts,),
        in_specs=[pl.BlockSpec((ts, 1), lambda i: (i, 0)), spec_c, spec_c],
        out_specs=[spec_t, spec_t],
        out_shape=[jax.ShapeDtypeStruct((s, hd), F32)] * 2,
        compiler_params=_params(("parallel",)),
        name="rope_tables",
    )(pos_col, inv2, sgn)


def _norm_mod_kernel(x_ref, g_ref, sc_ref, sh_ref, o_ref, *rest, eps, dilations):
    x = x_ref[...]
    ms = jnp.mean(x * x, axis=-1, keepdims=True)
    y = x * lax.rsqrt(ms + eps) * g_ref[...]
    hb = (y * (1.0 + sc_ref[...]) + sh_ref[...]).astype(o_ref.dtype)
    o_ref[...] = hb
    ts = hb.shape[0]
    for p_ref, dil in zip(rest, dilations):
        n = ts // dil
        hp = _dot(_perm_matrix(ts, n, hb.dtype), hb).astype(p_ref.dtype)
        for r in range(dil):
            p_ref[r] = hp[r * n:(r + 1) * n, :]


def _norm_mod(x, g3, mod, layer, i_shift, i_scale, cfg, dilations=()):
    s, d = x.shape
    ts = _tile(s, ROW_TILE)
    out_specs = [pl.BlockSpec((ts, d), lambda i: (i, 0))]
    out_shape = [jax.ShapeDtypeStruct((s, d), BF16)]
    for dil in dilations:
        assert ts % (dil * 2 * SUBLANES) == 0
        out_specs.append(pl.BlockSpec((dil, ts // dil, d), lambda i: (0, i, 0)))
        out_shape.append(jax.ShapeDtypeStruct((dil, s // dil, d), BF16))
    return pl.pallas_call(
        functools.partial(_norm_mod_kernel, eps=cfg.norm_eps, dilations=tuple(dilations)),
        grid=(s // ts,),
        in_specs=[
            pl.BlockSpec((ts, d), lambda i: (i, 0)),
            pl.BlockSpec((None, 1, d), lambda i: (layer, 0, 0)),
            pl.BlockSpec((None, 1, d), lambda i: (layer, 0, i_scale)),
            pl.BlockSpec((None, 1, d), lambda i: (layer, 0, i_shift)),
        ],
        out_specs=out_specs,
        out_shape=out_shape,
        compiler_params=_params(("parallel",)),
        name="norm_mod",
    )(x, g3, mod, mod)


def _final_norm_kernel(x_ref, g_ref, o_ref, *, eps):
    x = x_ref[...]
    ms = jnp.mean(x * x, axis=-1, keepdims=True)
    o_ref[...] = x * lax.rsqrt(ms + eps) * g_ref[...]


def _final_norm(x, g2, cfg):
    s, d = x.shape
    ts = _tile(s, ROW_TILE)
    return pl.pallas_call(
        functools.partial(_final_norm_kernel, eps=cfg.norm_eps),
        grid=(s // ts,),
        in_specs=[pl.BlockSpec((ts, d), lambda i: (i, 0)), pl.BlockSpec((1, d), lambda i: (0, 0))],
        out_specs=pl.BlockSpec((ts, d), lambda i: (i, 0)),
        out_shape=jax.ShapeDtypeStruct((s, d), F32),
        compiler_params=_params(("parallel",)),
        name="final_norm",
    )(x, g2)


def _proj_kernel(x_ref, wt_ref, cos_ref, sin_ref, o_ref, *, q_scale, hd):
    acc = _dot_nt(x_ref[...], wt_ref[...])
    sec = pl.program_id(1)
    is_rope = sec < 2
    scale = jnp.where(sec == 0, q_scale, 1.0).astype(F32)
    cos = jnp.where(is_rope, cos_ref[...] * scale, 1.0)
    sin = jnp.where(is_rope, sin_ref[...] * scale, 0.0)
    for c in range(acc.shape[1] // hd):
        a = acc[:, c * hd:(c + 1) * hd]
        o_ref[:, c * hd:(c + 1) * hd] = (a * cos + pltpu.roll(a, hd // 2, 1) * sin).astype(o_ref.dtype)


def _attn_proj(h, wt3, layer, cos_t, sin_t, g, cfg):
    m, k = h.shape
    hd = cfg.attn_head_dim
    gw = cfg.group_width
    tm = _tile(m, MM_TILE)
    tn = _tile(gw, MM_TILE)
    per = gw // tn
    sect = cfg.attn_width // tn
    tbl0 = g * (m // tm)
    return pl.pallas_call(
        functools.partial(_proj_kernel, q_scale=LOG2_E * float(hd) ** -0.5, hd=hd),
        grid=(m // tm, 3, per),
        in_specs=[
            pl.BlockSpec((tm, k), lambda i, s, p: (i, 0)),
            pl.BlockSpec((None, tn, k), lambda i, s, p: (layer, s * sect + g * per + p, 0)),
            pl.BlockSpec((tm, hd), lambda i, s, p: (tbl0 + i, 0)),
            pl.BlockSpec((tm, hd), lambda i, s, p: (tbl0 + i, 0)),
        ],
        out_specs=pl.BlockSpec((tm, tn), lambda i, s, p: (i, s * per + p)),
        out_shape=jax.ShapeDtypeStruct((m, 3 * gw), BF16),
        compiler_params=_params(("parallel", "parallel", "parallel")),
        name=f"attn_proj_{g}",
    )(h, wt3, cos_t, sin_t)


def _mm_plain_nt_kernel(x_ref, wt_ref, o_ref):
    o_ref[...] = _dot_nt(x_ref[...], wt_ref[...]).astype(o_ref.dtype)


def _gla_proj(x, wt3, layer, n):
    m, k = x.shape
    assert wt3.shape[2] == k and n <= wt3.shape[1]
    tm, tn = _tile(m, MM_TILE), _tile(n, MM_TILE)
    return pl.pallas_call(
        _mm_plain_nt_kernel,
        grid=(m // tm, n // tn),
        in_specs=[pl.BlockSpec((tm, k), lambda i, j: (i, 0)),
                  pl.BlockSpec((None, tn, k), lambda i, j: (layer, j, 0))],
        out_specs=pl.BlockSpec((tm, tn), lambda i, j: (i, j)),
        out_shape=jax.ShapeDtypeStruct((m, n), BF16),
        compiler_params=_params(("parallel", "parallel")),
        name="gla_proj",
    )(x, wt3)


def _mm_relu2_kernel(x_ref, w_ref, o_ref):
    a = jnp.maximum(_dot(x_ref[...], w_ref[...]), 0.0)
    o_ref[...] = (a * a).astype(o_ref.dtype)


def _mm_resid_kernel(x_ref, w_ref, r_ref, g_ref, o_ref):
    o_ref[...] = r_ref[...] + g_ref[...] * _dot(x_ref[...], w_ref[...])


def _mm_resid_acc_kernel(x_ref, w_ref, r_ref, g_ref, o_ref, acc_ref):
    kk = pl.program_id(2)

    @pl.when(kk == 0)
    def _():
        acc_ref[...] = jnp.zeros_like(acc_ref)

    acc_ref[...] += _dot(x_ref[...], w_ref[...])

    @pl.when(kk == pl.num_programs(2) - 1)
    def _():
        o_ref[...] = r_ref[...] + g_ref[...] * acc_ref[...]


def _mlp_in(x, w3, layer):
    m, k = x.shape
    n = w3.shape[2]
    assert w3.shape[1] == k
    tm, tn = _tile(m, MM_TILE), _tile(n, MM_TILE)
    return pl.pallas_call(
        _mm_relu2_kernel,
        grid=(m // tm, n // tn),
        in_specs=[pl.BlockSpec((tm, k), lambda i, j: (i, 0)),
                  pl.BlockSpec((None, k, tn), lambda i, j: (layer, 0, j))],
        out_specs=pl.BlockSpec((tm, tn), lambda i, j: (i, j)),
        out_shape=jax.ShapeDtypeStruct((m, n), BF16),
        compiler_params=_params(("parallel", "parallel")),
        name="mlp_in",
    )(x, w3)


def _mm_resid(x, w3, resid, mod, layer, i_gate, *, tk, name):
    m, k = x.shape
    n = w3.shape[2]
    assert w3.shape[1] == k
    tm, tn = _tile(m, MM_TILE), _tile(n, MM_TILE)
    gate_blocks = n // tn
    tk = _tile(k, tk)
    if tk == k:
        return pl.pallas_call(
            _mm_resid_kernel,
            grid=(m // tm, n // tn),
            in_specs=[
                pl.BlockSpec((tm, k), lambda i, j: (i, 0)),
                pl.BlockSpec((None, k, tn), lambda i, j: (layer, 0, j)),
                pl.BlockSpec((tm, tn), lambda i, j: (i, j)),
                pl.BlockSpec((None, 1, tn), lambda i, j: (layer, 0, i_gate * gate_blocks + j)),
            ],
            out_specs=pl.BlockSpec((tm, tn), lambda i, j: (i, j)),
            out_shape=jax.ShapeDtypeStruct((m, n), F32),
            compiler_params=_params(("parallel", "parallel")),
            name=name,
        )(x, w3, resid, mod)
    return pl.pallas_call(
        _mm_resid_acc_kernel,
        grid=(m // tm, n // tn, k // tk),
        in_specs=[
            pl.BlockSpec((tm, tk), lambda i, j, kk: (i, kk)),
            pl.BlockSpec((None, tk, tn), lambda i, j, kk: (layer, kk, j)),
            pl.BlockSpec((tm, tn), lambda i, j, kk: (i, j)),
            pl.BlockSpec((None, 1, tn), lambda i, j, kk: (layer, 0, i_gate * gate_blocks + j)),
        ],
        out_specs=pl.BlockSpec((tm, tn), lambda i, j, kk: (i, j)),
        out_shape=jax.ShapeDtypeStruct((m, n), F32),
        scratch_shapes=[pltpu.VMEM((tm, tn), F32)],
        compiler_params=_params(("parallel", "parallel", "arbitrary")),
        name=name,
    )(x, w3, resid, mod)


def _merge_kernel(h_ref, og_ref, oa_ref, wga_ref, wgb_ref, bga_ref, bgb_ref, wa_ref, wb_ref, o_ref):
    h = h_ref[...]
    ga = _sigmoid(_dot(h, wga_ref[...]) + bga_ref[...])
    gb = _sigmoid(_dot(h, wgb_ref[...]) + bgb_ref[...])
    a = _dot(og_ref[...], wa_ref[...])
    b = _dot(oa_ref[...], wb_ref[...])
    o_ref[...] = (ga * a + gb * b).astype(o_ref.dtype)


def _merge(h, o_gla, o_attn, w_gate3, b_gate3, w_a3, w_b3, layer):
    m, d = h.shape
    tm, tn = _tile(m, MERGE_TILE), _tile(d, MERGE_TILE)
    nb = d // tn
    x_spec = lambda kdim: pl.BlockSpec((tm, kdim), lambda j, i: (i, 0))
    w_spec = lambda kdim, off: pl.BlockSpec((None, kdim, tn), lambda j, i: (layer, 0, j + off))
    b_spec = lambda off: pl.BlockSpec((None, 1, tn), lambda j, i: (layer, 0, j + off))
    return pl.pallas_call(
        _merge_kernel,
        grid=(nb, m // tm),
        in_specs=[
            x_spec(d), x_spec(o_gla.shape[1]), x_spec(o_attn.shape[1]),
            w_spec(d, 0), w_spec(d, nb), b_spec(0), b_spec(nb),
            w_spec(w_a3.shape[1], 0), w_spec(w_b3.shape[1], 0),
        ],
        out_specs=pl.BlockSpec((tm, tn), lambda j, i: (i, j)),
        out_shape=jax.ShapeDtypeStruct((m, d), BF16),
        compiler_params=_params(("parallel", "parallel")),
        name="gated_merge",
    )(h, o_gla, o_attn, w_gate3, w_gate3, b_gate3, b_gate3, w_a3, w_b3)


def _split_bf16(x, pieces):
    out = []
    for _ in range(pieces - 1):
        p = x.astype(BF16)
        out.append(p)
        x = x - p.astype(F32)
    out.append(x.astype(BF16))
    return out


def _gate_kernel(h_ref, wl_ref, wc_ref, ba_ref, b_ref, *, chunk, tau, rank):
    tm = h_ref.shape[0]
    x = _dot(h_ref[...], wl_ref[...])
    hi = x.astype(BF16).astype(F32)
    lane = lax.broadcasted_iota(jnp.int32, x.shape, 1)
    a = jnp.where(lane < 2 * rank, hi, x - hi).astype(BF16)
    logits = _dot(a, wc_ref[...]) + ba_ref[...]
    g = (jnp.minimum(logits, 0.0) - jnp.log(1.0 + jnp.exp(-jnp.abs(logits)))) * (LOG2_E / tau)
    ri = lax.broadcasted_iota(jnp.int32, (tm, tm), 0)
    ci = lax.broadcasted_iota(jnp.int32, (tm, tm), 1)
    tril = jnp.where(jnp.logical_and(ri >= ci, (ri ^ ci) < chunk), 1.0, 0.0).astype(BF16)
    p1, p2, p3 = _split_bf16(g, 3)
    b_ref[...] = _dot(tril, p1) + _dot(tril, p2) + _dot(tril, p3)


def _gla_gate(h, w_glr3, w_alpha_cat, b_alpha2, chunk, cfg):
    s, d = h.shape
    n = w_alpha_cat.shape[1]
    tm = _tile(s, max(ROW_TILE, chunk))
    assert tm % chunk == 0 and chunk & (chunk - 1) == 0 and 3 * cfg.gate_rank <= LANES
    return pl.pallas_call(
        functools.partial(_gate_kernel, chunk=chunk, tau=cfg.gate_tau, rank=cfg.gate_rank),
        grid=(s // tm,),
        in_specs=[
            pl.BlockSpec((tm, d), lambda i: (i, 0)),
            pl.BlockSpec((d, LANES), lambda i: (0, 0)),
            pl.BlockSpec((LANES, n), lambda i: (0, 0)),
            pl.BlockSpec((1, n), lambda i: (0, 0)),
        ],
        out_specs=pl.BlockSpec((tm, n), lambda i: (i, 0)),
        out_shape=jax.ShapeDtypeStruct((s, n), F32),
        compiler_params=_params(("parallel",)),
        name="gla_gate",
    )(h, w_glr3, w_alpha_cat, b_alpha2)


def _gla_block_kernel(q_ref, k_ref, v_ref, gr_ref, b_ref, gn_ref, o_ref, st_ref, *, heads, q_scale, eps):
    rows = q_ref.shape[0]
    dk = q_ref.shape[1] // heads
    dv = v_ref.shape[1] // heads
    sub = SUBLANES
    half = rows // 2

    @pl.when(pl.program_id(1) == 0)
    def _():
        st_ref[...] = jnp.zeros_like(st_ref)

    ri = lax.broadcasted_iota(jnp.int32, (half, half), 0)
    ci = lax.broadcasted_iota(jnp.int32, (half, half), 1)
    sub_i = lax.broadcasted_iota(jnp.int32, (sub, half), 0)
    lane_i = lax.broadcasted_iota(jnp.int32, (sub, half), 1)

    def cross_scores(q, k, b, g_blk):
        n = q.shape[0]
        span = 2 * g_blk
        zero = jnp.zeros((g_blk, dk), F32)
        q_parts, k_parts = [], []
        for s0 in range(0, n, span):
            ref = jnp.broadcast_to(b[s0 + g_blk - 1:s0 + g_blk, :], (g_blk, dk))
            lo, up = slice(s0, s0 + g_blk), slice(s0 + g_blk, s0 + span)
            q_parts += [zero, q[up] * jnp.exp2(b[up] - ref)]
            k_parts += [k[lo] * jnp.exp2(ref - b[lo]), zero]
        qg = jnp.concatenate(q_parts, axis=0).astype(BF16)
        kg = jnp.concatenate(k_parts, axis=0).astype(BF16)
        return _dot_nt(qg, kg)

    for hh in range(heads):
        ksl = slice(hh * dk, (hh + 1) * dk)
        vsl = slice(hh * dv, (hh + 1) * dv)

        q = q_ref[:, ksl].astype(F32) * q_scale
        k = k_ref[:, ksl].astype(F32)
        vb = v_ref[:, vsl]
        b = b_ref[:, ksl]
        b_last = b[rows - 1:rows, :]

        st = st_ref[hh]
        qe = (q * jnp.exp2(b)).astype(BF16)
        o = _dot_nt(qe, st.astype(BF16))

        b_mid = jnp.broadcast_to(b[half - 1:half, :], (half, dk))
        x_lo = _dot_nt((q[half:] * jnp.exp2(b[half:] - b_mid)).astype(BF16),
                       (k[:half] * jnp.exp2(b_mid - b[:half])).astype(BF16))
        a_halves = []
        for hf in range(2):
            rs = slice(hf * half, (hf + 1) * half)
            qh, kh, bh = q[rs], k[rs], b[rs]
            tiles = []
            for lo in range(0, half, sub):
                qi, ki, bi = qh[lo:lo + sub, :], kh[lo:lo + sub, :], bh[lo:lo + sub, :]
                acc = jnp.zeros((sub, half), F32)
                for jj in range(sub):
                    t = qi * ki[jj:jj + 1, :] * jnp.exp2(bi - bi[jj:jj + 1, :])
                    col = jnp.sum(t, axis=1, keepdims=True)
                    keep = jnp.logical_and(lane_i == lo + jj, sub_i >= jj)
                    acc = jnp.where(keep, col, acc)
                tiles.append(acc)
            a = jnp.concatenate(tiles, axis=0) + cross_scores(qh, kh, bh, half // 2)
            g_blk = half // 4
            while g_blk >= sub:
                same = (ri ^ ci) < 2 * g_blk
                a = a + jnp.where(same, cross_scores(qh, kh, bh, g_blk), 0.0)
                g_blk //= 2
            a_halves.append(a)
        top = jnp.concatenate([a_halves[0], jnp.zeros((half, half), F32)], axis=1)
        bot = jnp.concatenate([x_lo, a_halves[1]], axis=1)
        s = jnp.concatenate([top, bot], axis=0).astype(BF16)
        o = o + _dot(s, vb)

        kd = (k * jnp.exp2(b_last - b)).astype(BF16)
        st_ref[hh] = st * jnp.exp2(b_last) + _dot_tn(vb, kd)

        ms = jnp.mean(o * o, axis=-1, keepdims=True)
        y = o * lax.rsqrt(ms + eps) * gn_ref[...]
        gr = gr_ref[:, vsl].astype(F32)
        o_ref[:, vsl] = (y * (gr * _sigmoid(gr))).astype(o_ref.dtype)


def _gla_block(proj, b_cum, gla_norm2, rows, cfg, heads_per_step=4):
    s = proj.shape[0]
    nh, dk, dv = cfg.gla_heads, cfg.dk_head, cfg.dv_head
    hp = heads_per_step
    assert s % rows == 0 and nh % hp == 0 and rows & (rows - 1) == 0 and rows >= 8 * SUBLANES
    assert cfg.gla_dk % (hp * dk) == 0 and (2 * cfg.gla_dk) % (hp * dv) == 0
    k_off = cfg.gla_dk // (hp * dk)
    v_off = (2 * cfg.gla_dk) // (hp * dv)
    r_off = (2 * cfg.gla_dk + cfg.gla_dv) // (hp * dv)
    return pl.pallas_call(
        functools.partial(_gla_block_kernel, heads=hp, q_scale=float(dk) ** -0.5, eps=cfg.norm_eps),
        grid=(nh // hp, s // rows),
        in_specs=[
            pl.BlockSpec((rows, hp * dk), lambda h, t: (t, h)),
            pl.BlockSpec((rows, hp * dk), lambda h, t: (t, k_off + h)),
            pl.BlockSpec((rows, hp * dv), lambda h, t: (t, v_off + h)),
            pl.BlockSpec((rows, hp * dv), lambda h, t: (t, r_off + h)),
            pl.BlockSpec((rows, hp * dk), lambda h, t: (t, h)),
            pl.BlockSpec((1, dv), lambda h, t: (0, 0)),
        ],
        out_specs=pl.BlockSpec((rows, hp * dv), lambda h, t: (t, h)),
        out_shape=jax.ShapeDtypeStruct((s, cfg.gla_dv), BF16),
        scratch_shapes=[pltpu.VMEM((hp, dv, dk), F32)],
        compiler_params=_params(("parallel", "arbitrary")),
        name="gla",
    )(proj, proj, proj, proj, b_cum, gla_norm2)


def _attn_kernel(q_ref, kp_ref, kc_ref, vp_ref, vc_ref, o_ref, lse_ref, s_scr, p_scr, *, heads, hd, blk):
    nq = q_ref.shape[0] // blk
    n = pl.program_id(1)
    qi = lax.broadcasted_iota(jnp.int32, (blk, 2 * blk), 0)
    kj = lax.broadcasted_iota(jnp.int32, (blk, 2 * blk), 1)
    dist = blk + qi - kj
    band = jnp.logical_and(dist >= 0, dist <= blk)
    first = jnp.logical_and(band, jnp.logical_or(kj >= blk, n > 0))
    lane = lax.broadcasted_iota(jnp.int32, (blk, LANES), 1)
    for b in range(nq):
        rows = slice(b * blk, (b + 1) * blk)
        keys = slice((b - 1) * blk, (b + 1) * blk)
        mask = first if b == 0 else band
        slot = b % 2
        for hh in range(heads):
            sl = slice(hh * hd, (hh + 1) * hd)
            q = q_ref[rows, sl]
            if b == 0:
                s = jnp.concatenate([_dot_nt(q, kp_ref[:, sl]), _dot_nt(q, kc_ref[rows, sl])], axis=1)
            else:
                s = _dot_nt(q, kc_ref[keys, sl])
            s_scr[slot, hh] = jnp.where(mask, s, -jnp.inf)
        lse_tile = jnp.zeros((blk, LANES), F32)
        for hh in range(heads):
            s = s_scr[slot, hh]
            m = jnp.max(s, axis=1, keepdims=True)
            p = jnp.exp2(s - m)
            den = jnp.sum(p, axis=1, keepdims=True)
            p_scr[slot, hh] = (p * (1.0 / den)).astype(BF16)
            lse_tile = jnp.where(lane == hh, m * LN_2 + jnp.log(den), lse_tile)
        lse_ref[rows, :] = lse_tile
        for hh in range(heads):
            sl = slice(hh * hd, (hh + 1) * hd)
            p = p_scr[slot, hh]
            if b == 0:
                o = _dot(p[:, :blk], vp_ref[:, sl]) + _dot(p[:, blk:], vc_ref[rows, sl])
            else:
                o = _dot(p, vc_ref[keys, sl])
            o_ref[rows, sl] = o.astype(o_ref.dtype)


def _attn_group(qkv, window, dilation, cfg, name, blocks_per_step=8):
    dil, u, w3 = qkv.shape
    gw = cfg.group_width
    heads = cfg.attn_heads_per_group
    blk = window // dilation
    assert dil == dilation and w3 == 3 * gw and u % blk == 0 and heads <= LANES
    nb = u // blk
    nq = blocks_per_step
    while nb % nq:
        nq //= 2
    rows = nq * blk

    def cur(part):
        return pl.BlockSpec((None, rows, gw), lambda r, n: (r, n, part))

    def prev(part):
        return pl.BlockSpec((None, blk, gw), lambda r, n: (r, jnp.maximum(n * nq - 1, 0), part))

    return pl.pallas_call(
        functools.partial(_attn_kernel, heads=heads, hd=cfg.attn_head_dim, blk=blk),
        grid=(dilation, nb // nq),
        in_specs=[cur(0), prev(1), cur(1), prev(2), cur(2)],
        out_specs=[pl.BlockSpec((None, rows, gw), lambda r, n: (r, n, 0)),
                   pl.BlockSpec((None, rows, LANES), lambda r, n: (r, n, 0))],
        out_shape=[jax.ShapeDtypeStruct((dilation, u, gw), BF16),
                   jax.ShapeDtypeStruct((dilation, u, LANES), F32)],
        scratch_shapes=[pltpu.VMEM((2, heads, blk, 2 * blk), F32), pltpu.VMEM((2, heads, blk, 2 * blk), BF16)],
        compiler_params=_params(("parallel", "arbitrary")),
        name=name,
    )(qkv, qkv, qkv, qkv, qkv)


def _combine_kernel(*refs, dilations, hd):
    ng = len(dilations)
    o_refs, l_refs, out_ref = refs[:ng], refs[ng:2 * ng], refs[2 * ng]
    ts, gw = out_ref.shape
    perms = [_perm_matrix(ts, dil, BF16) if dil > 1 else None for dil in dilations]
    ls = []
    for gi in range(ng):
        l_g = l_refs[gi][...].reshape(ts, LANES)
        if perms[gi] is not None:
            l_g = functools.reduce(lambda a, b: a + b, [_dot(perms[gi], p) for p in _split_bf16(l_g, 3)])
        ls.append(l_g)
    m = functools.reduce(jnp.maximum, ls)
    es = [jnp.exp(l - m) for l in ls]
    inv = 1.0 / functools.reduce(lambda a, b: a + b, es)
    ws = [e * inv for e in es]
    width = 2 * hd if gw % (2 * hd) == 0 else hd
    for c0 in range(0, gw, width):
        o_c = []
        for gi in range(ng):
            o_g = o_refs[gi][:, :, c0:c0 + width].reshape(ts, width)
            o_c.append(_dot(perms[gi], o_g) if perms[gi] is not None else o_g.astype(F32))
        for c in range(c0, c0 + width, hd):
            hh = c // hd
            acc = None
            for gi in range(ng):
                term = ws[gi][:, hh:hh + 1] * o_c[gi][:, c - c0:c - c0 + hd]
                acc = term if acc is None else acc + term
            out_ref[:, c:c + hd] = acc.astype(out_ref.dtype)


def _combine(outs, lses, dilations, cfg):
    gw = cfg.group_width
    s = outs[0].shape[0] * outs[0].shape[1]
    ts = _tile(s, MERGE_TILE)
    in_specs = []
    for dil in dilations:
        assert ts % (dil * 2 * SUBLANES) == 0
        in_specs.append(pl.BlockSpec((dil, ts // dil, gw), lambda i: (0, i, 0)))
    for dil in dilations:
        in_specs.append(pl.BlockSpec((dil, ts // dil, LANES), lambda i: (0, i, 0)))
    return pl.pallas_call(
        functools.partial(_combine_kernel, dilations=tuple(dilations), hd=cfg.attn_head_dim),
        grid=(s // ts,),
        in_specs=in_specs,
        out_specs=pl.BlockSpec((ts, gw), lambda i: (i, 0)),
        out_shape=jax.ShapeDtypeStruct((s, gw), BF16),
        compiler_params=_params(("parallel",)),
        name="attn_combine",
    )(*outs, *lses)


def _forward(cfg, x, c, positions, ada_w, ada_b, norm_mix, norm_mlp, w_in, w_alpha_up, b_alpha,
             gla_norm, w_branch_gla, w_branch_attn, w_gate, b_gate, w_out, w_mlp_in, w_mlp_out,
             norm_final):
    bsz, s, d = x.shape
    assert bsz == 1 and d == cfg.d_model
    depth = cfg.depth
    x2 = x.reshape(s, d)
    mod = _ada_mod(c.reshape(d, 1), ada_w, ada_b.reshape(depth, 1, -1))
    dils = [dil for _, dil in cfg.dilated_groups]
    perm_dils = sorted({dil for dil in dils if dil > 1})
    pos = positions.reshape(s)
    pos_all = jnp.concatenate([pos.reshape(s // dil, dil).T.reshape(s) for dil in dils]).reshape(-1, 1)
    cos_t, sin_t = _rope_tables(pos_all, cfg)
    g_mix = norm_mix.reshape(depth, 1, d)
    g_mlp = norm_mlp.reshape(depth, 1, d)
    gw, rank = cfg.gla_width, cfg.gate_rank
    w_in_t = jnp.swapaxes(w_in, 1, 2).astype(BF16)
    w_attn_t = w_in_t[:, gw + rank:]
    w_gate_b = w_gate.astype(BF16)
    b_gate3 = b_gate.reshape(depth, 1, -1)
    w_bgla_b = w_branch_gla.astype(BF16)
    w_battn_b = w_branch_attn.astype(BF16)
    w_out_b = w_out.astype(BF16)
    w_mlp_in_b = w_mlp_in.astype(BF16)
    w_mlp_out_b = w_mlp_out.astype(BF16)

    for l in range(depth):
        w_glr = w_in[l, :, gw:gw + rank]
        w_glr3 = jnp.pad(jnp.concatenate([w_glr] * 3, axis=1), ((0, 0), (0, LANES - 3 * rank))).astype(BF16)
        wa_hi, wa_lo = _split_bf16(w_alpha_up[l], 2)
        w_alpha_cat = jnp.pad(jnp.concatenate([wa_hi, wa_lo, wa_hi], axis=0), ((0, LANES - 3 * rank), (0, 0)))

        hs = _norm_mod(x2, g_mix, mod, l, 0, 1, cfg, dilations=perm_dils)
        h = hs[0]
        h_by_dil = {1: h, **{dil: hp.reshape(s, d) for dil, hp in zip(perm_dils, hs[1:])}}
        proj = _gla_proj(h, w_in_t, l, gw)
        b_cum = _gla_gate(h, w_glr3, w_alpha_cat, b_alpha[l].reshape(1, -1), GLA_BLOCK, cfg)
        o_gla = _gla_block(proj, b_cum, gla_norm[l].reshape(1, -1), GLA_BLOCK, cfg)
        outs, lses = [], []
        for g, (window, dil) in enumerate(cfg.dilated_groups):
            qkv = _attn_proj(h_by_dil[dil], w_attn_t, l, cos_t, sin_t, g, cfg)
            o_g, lse_g = _attn_group(qkv.reshape(dil, s // dil, -1), window, dil, cfg, f"dilated_attn_{g}")
            outs.append(o_g)
            lses.append(lse_g)
        o_attn = _combine(outs, lses, dils, cfg)
        merged = _merge(h, o_gla, o_attn, w_gate_b, b_gate3, w_bgla_b, w_battn_b, l)
        x2 = _mm_resid(merged, w_out_b, x2, mod, l, 2, tk=MAX_CONTRACTION, name="out_proj")

        h = _norm_mod(x2, g_mlp, mod, l, 3, 4, cfg)[0]
        a = _mlp_in(h, w_mlp_in_b, l)
        x2 = _mm_resid(a, w_mlp_out_b, x2, mod, l, 5, tk=MAX_CONTRACTION // 2, name="mlp_out")

    return _final_norm(x2, norm_final.reshape(1, d), cfg).reshape(bsz, s, d)


def kernel(x, c, positions, ada_w, ada_b, norm_mix, norm_mlp, w_in, w_alpha_up, b_alpha, gla_norm,
           w_branch_gla, w_branch_attn, w_gate, b_gate, w_out, w_mlp_in, w_mlp_out, norm_final):
    return _forward(Config(), x, c, positions, ada_w, ada_b, norm_mix, norm_mlp, w_in, w_alpha_up,
                    b_alpha, gla_norm, w_branch_gla, w_branch_attn, w_gate, b_gate, w_out, w_mlp_in,
                    w_mlp_out, norm_final)
```

```python
import functools
from typing import NamedTuple

import jax
import jax.numpy as jnp
from jax import lax
from jax.experimental import pallas as pl
from jax.experimental.pallas import tpu as pltpu

F32 = jnp.float32
BF16 = jnp.bfloat16
LANES = 128
SUBLANES = 8
VMEM_LIMIT = 56 * 1024 * 1024
MM_TILE = 1024
MAX_CONTRACTION = 4096
MERGE_TILE = 512
ROW_TILE = 256
GLA_BLOCK = 256
LOG2_E = 1.4426950408889634
LN_2 = 0.6931471805599453


class Config(NamedTuple):
    d_model: int = 4096
    depth: int = 2
    gla_heads: int = 4
    gate_rank: int = 16
    gate_tau: float = 16.0
    attn_head_dim: int = 128
    attn_heads_per_group: int = 16
    dilated_groups: tuple = ((128, 1), (512, 4), (2048, 16))
    rope_theta: float = 10000.0
    norm_eps: float = 1e-6

    @property
    def gla_dk(self):
        return self.d_model // 2

    @property
    def gla_dv(self):
        return self.d_model

    @property
    def dk_head(self):
        return self.gla_dk // self.gla_heads

    @property
    def dv_head(self):
        return self.gla_dv // self.gla_heads

    @property
    def group_width(self):
        return self.attn_heads_per_group * self.attn_head_dim

    @property
    def attn_width(self):
        return len(self.dilated_groups) * self.group_width

    @property
    def gla_width(self):
        return 2 * self.gla_dk + 2 * self.gla_dv


def _params(sem):
    return pltpu.CompilerParams(dimension_semantics=sem, vmem_limit_bytes=VMEM_LIMIT)


def _sigmoid(x):
    return 1.0 / (1.0 + jnp.exp(-x))


def _dot(a, b):
    return jnp.dot(a, b, preferred_element_type=F32)


def _dot_nt(a, b):
    return lax.dot_general(a, b, (((1,), (1,)), ((), ())), preferred_element_type=F32)


def _dot_tn(a, b):
    return lax.dot_general(a, b, (((0,), (0,)), ((), ())), preferred_element_type=F32)


def _tile(n, pref):
    t = min(n, pref)
    assert n % t == 0, (n, pref)
    return t


def _perm_matrix(size, inner, dtype):
    outer = size // inner
    assert inner * outer == size and inner & (inner - 1) == 0 and outer & (outer - 1) == 0
    i = lax.broadcasted_iota(jnp.int32, (size, size), 0)
    j = lax.broadcasted_iota(jnp.int32, (size, size), 1)
    src = (i & (inner - 1)) * outer + (i >> (inner.bit_length() - 1))
    return (j == src).astype(dtype)


def _ada_kernel(c_ref, w_ref, b_ref, o_ref, *, rows):
    d, tn = w_ref.shape

    def body(i, acc):
        r = pl.multiple_of(i * rows, rows)
        cc = c_ref[pl.ds(r, rows), :]
        cond = cc * _sigmoid(cc)
        p = w_ref[pl.ds(r, rows), :] * cond
        return acc + jnp.sum(p.reshape(rows // SUBLANES, SUBLANES, tn), axis=0)

    acc = lax.fori_loop(0, d // rows, body, jnp.zeros((SUBLANES, tn), F32))
    o_ref[...] = jnp.sum(acc, axis=0, keepdims=True) + b_ref[...]


def _ada_mod(c_col, ada_w, ada_b3):
    depth, d, n = ada_w.shape
    tn = _tile(n, MM_TILE)
    rows = _tile(d, ROW_TILE)
    return pl.pallas_call(
        functools.partial(_ada_kernel, rows=rows),
        grid=(depth, n // tn),
        in_specs=[
            pl.BlockSpec((d, 1), lambda l, j: (0, 0)),
            pl.BlockSpec((None, d, tn), lambda l, j: (l, 0, j)),
            pl.BlockSpec((None, 1, tn), lambda l, j: (l, 0, j)),
        ],
        out_specs=pl.BlockSpec((None, 1, tn), lambda l, j: (l, 0, j)),
        out_shape=jax.ShapeDtypeStruct((depth, 1, n), F32),
        compiler_params=_params(("parallel", "parallel")),
        name="ada_mod",
    )(c_col, ada_w, ada_b3)


def _rope_kernel(p_ref, inv_ref, sgn_ref, c_ref, s_ref):
    ts, hd = c_ref.shape
    hts, half = ts // 2, hd // 2
    low = lax.broadcasted_iota(jnp.int32, (hts, hd), 1) < half
    pos = jnp.where(low, p_ref[:hts, :].astype(F32), p_ref[hts:, :].astype(F32))
    ang = pos * inv_ref[...]
    c, s = jnp.cos(ang), jnp.sin(ang)
    c_sw, s_sw = pltpu.roll(c, half, 1), pltpu.roll(s, half, 1)
    sgn = sgn_ref[...]
    c_ref[:hts, :] = jnp.where(low, c, c_sw)
    c_ref[hts:, :] = jnp.where(low, c_sw, c)
    s_ref[:hts, :] = jnp.where(low, s, s_sw) * sgn
    s_ref[hts:, :] = jnp.where(low, s_sw, s) * sgn


def _rope_tables(pos_col, cfg):
    s = pos_col.shape[0]
    hd = cfg.attn_head_dim
    half = hd // 2
    inv = cfg.rope_theta ** (-jnp.arange(half, dtype=F32) / half)
    inv2 = jnp.concatenate([inv, inv]).reshape(1, hd)
    sgn = jnp.concatenate([-jnp.ones((half,), F32), jnp.ones((half,), F32)]).reshape(1, hd)
    ts = _tile(s, MM_TILE)
    spec_t = pl.BlockSpec((ts, hd), lambda i: (i, 0))
    spec_c = pl.BlockSpec((1, hd), lambda i: (0, 0))
    return pl.pallas_call(
        _rope_kernel,
        grid=(s // ts,),
        in_specs=[pl.BlockSpec((ts, 1), lambda i: (i, 0)), spec_c, spec_c],
        out_specs=[spec_t, spec_t],
        out_shape=[jax.ShapeDtypeStruct((s, hd), F32)] * 2,
        compiler_params=_params(("parallel",)),
        name="rope_tables",
    )(pos_col, inv2, sgn)


def _norm_mod_kernel(x_ref, g_ref, sc_ref, sh_ref, o_ref, *rest, eps, dilations):
    x = x_ref[...]
    ms = jnp.mean(x * x, axis=-1, keepdims=True)
    y = x * lax.rsqrt(ms + eps) * g_ref[...]
    hb = (y * (1.0 + sc_ref[...]) + sh_ref[...]).astype(o_ref.dtype)
    o_ref[...] = hb
    ts = hb.shape[0]
    for p_ref, dil in zip(rest, dilations):
        n = ts // dil
        hp = _dot(_perm_matrix(ts, n, hb.dtype), hb).astype(p_ref.dtype)
        for r in range(dil):
            p_ref[r] = hp[r * n:(r + 1) * n, :]


def _norm_mod(x, g3, mod, layer, i_shift, i_scale, cfg, dilations=()):
    s, d = x.shape
    ts = _tile(s, ROW_TILE)
    out_specs = [pl.BlockSpec((ts, d), lambda i: (i, 0))]
    out_shape = [jax.ShapeDtypeStruct((s, d), BF16)]
    for dil in dilations:
        assert ts % (dil * 2 * SUBLANES) == 0
        out_specs.append(pl.BlockSpec((dil, ts // dil, d), lambda i: (0, i, 0)))
        out_shape.append(jax.ShapeDtypeStruct((dil, s // dil, d), BF16))
    return pl.pallas_call(
        functools.partial(_norm_mod_kernel, eps=cfg.norm_eps, dilations=tuple(dilations)),
        grid=(s // ts,),
        in_specs=[
            pl.BlockSpec((ts, d), lambda i: (i, 0)),
            pl.BlockSpec((None, 1, d), lambda i: (layer, 0, 0)),
            pl.BlockSpec((None, 1, d), lambda i: (layer, 0, i_scale)),
            pl.BlockSpec((None, 1, d), lambda i: (layer, 0, i_shift)),
        ],
        out_specs=out_specs,
        out_shape=out_shape,
        compiler_params=_params(("parallel",)),
        name="norm_mod",
    )(x, g3, mod, mod)


def _final_norm_kernel(x_ref, g_ref, o_ref, *, eps):
    x = x_ref[...]
    ms = jnp.mean(x * x, axis=-1, keepdims=True)
    o_ref[...] = x * lax.rsqrt(ms + eps) * g_ref[...]


def _final_norm(x, g2, cfg):
    s, d = x.shape
    ts = _tile(s, ROW_TILE)
    return pl.pallas_call(
        functools.partial(_final_norm_kernel, eps=cfg.norm_eps),
        grid=(s // ts,),
        in_specs=[pl.BlockSpec((ts, d), lambda i: (i, 0)), pl.BlockSpec((1, d), lambda i: (0, 0))],
        out_specs=pl.BlockSpec((ts, d), lambda i: (i, 0)),
        out_shape=jax.ShapeDtypeStruct((s, d), F32),
        compiler_params=_params(("parallel",)),
        name="final_norm",
    )(x, g2)


def _proj_kernel(x_ref, wt_ref, cos_ref, sin_ref, o_ref, *, q_scale, hd):
    acc = _dot_nt(x_ref[...], wt_ref[...])
    sec = pl.program_id(1)
    is_rope = sec < 2
    scale = jnp.where(sec == 0, q_scale, 1.0).astype(F32)
    cos = jnp.where(is_rope, cos_ref[...] * scale, 1.0)
    sin = jnp.where(is_rope, sin_ref[...] * scale, 0.0)
    for c in range(acc.shape[1] // hd):
        a = acc[:, c * hd:(c + 1) * hd]
        o_ref[:, c * hd:(c + 1) * hd] = (a * cos + pltpu.roll(a, hd // 2, 1) * sin).astype(o_ref.dtype)


def _attn_proj(h, wt3, layer, cos_t, sin_t, g, cfg):
    m, k = h.shape
    hd = cfg.attn_head_dim
    gw = cfg.group_width
    tm = _tile(m, MM_TILE)
    tn = _tile(gw, MM_TILE)
    per = gw // tn
    sect = cfg.attn_width // tn
    tbl0 = g * (m // tm)
    return pl.pallas_call(
        functools.partial(_proj_kernel, q_scale=LOG2_E * float(hd) ** -0.5, hd=hd),
        grid=(m // tm, 3, per),
        in_specs=[
            pl.BlockSpec((tm, k), lambda i, s, p: (i, 0)),
            pl.BlockSpec((None, tn, k), lambda i, s, p: (layer, s * sect + g * per + p, 0)),
            pl.BlockSpec((tm, hd), lambda i, s, p: (tbl0 + i, 0)),
            pl.BlockSpec((tm, hd), lambda i, s, p: (tbl0 + i, 0)),
        ],
        out_specs=pl.BlockSpec((tm, tn), lambda i, s, p: (i, s * per + p)),
        out_shape=jax.ShapeDtypeStruct((m, 3 * gw), BF16),
        compiler_params=_params(("parallel", "parallel", "parallel")),
        name=f"attn_proj_{g}",
    )(h, wt3, cos_t, sin_t)


def _mm_plain_nt_kernel(x_ref, wt_ref, o_ref):
    o_ref[...] = _dot_nt(x_ref[...], wt_ref[...]).astype(o_ref.dtype)


def _gla_proj(x, wt3, layer, n):
    m, k = x.shape
    assert wt3.shape[2] == k and n <= wt3.shape[1]
    tm, tn = _tile(m, MM_TILE), _tile(n, MM_TILE)
    return pl.pallas_call(
        _mm_plain_nt_kernel,
        grid=(m // tm, n // tn),
        in_specs=[pl.BlockSpec((tm, k), lambda i, j: (i, 0)),
                  pl.BlockSpec((None, tn, k), lambda i, j: (layer, j, 0))],
        out_specs=pl.BlockSpec((tm, tn), lambda i, j: (i, j)),
        out_shape=jax.ShapeDtypeStruct((m, n), BF16),
        compiler_params=_params(("parallel", "parallel")),
        name="gla_proj",
    )(x, wt3)


def _mm_relu2_kernel(x_ref, w_ref, o_ref):
    a = jnp.maximum(_dot(x_ref[...], w_ref[...]), 0.0)
    o_ref[...] = (a * a).astype(o_ref.dtype)


def _mm_resid_kernel(x_ref, w_ref, r_ref, g_ref, o_ref):
    o_ref[...] = r_ref[...] + g_ref[...] * _dot(x_ref[...], w_ref[...])


def _mm_resid_acc_kernel(x_ref, w_ref, r_ref, g_ref, o_ref, acc_ref):
    kk = pl.program_id(2)

    @pl.when(kk == 0)
    def _():
        acc_ref[...] = jnp.zeros_like(acc_ref)

    acc_ref[...] += _dot(x_ref[...], w_ref[...])

    @pl.when(kk == pl.num_programs(2) - 1)
    def _():
        o_ref[...] = r_ref[...] + g_ref[...] * acc_ref[...]


def _mlp_in(x, w3, layer):
    m, k = x.shape
    n = w3.shape[2]
    assert w3.shape[1] == k
    tm, tn = _tile(m, MM_TILE), _tile(n, MM_TILE)
    return pl.pallas_call(
        _mm_relu2_kernel,
        grid=(m // tm, n // tn),
        in_specs=[pl.BlockSpec((tm, k), lambda i, j: (i, 0)),
                  pl.BlockSpec((None, k, tn), lambda i, j: (layer, 0, j))],
        out_specs=pl.BlockSpec((tm, tn), lambda i, j: (i, j)),
        out_shape=jax.ShapeDtypeStruct((m, n), BF16),
        compiler_params=_params(("parallel", "parallel")),
        name="mlp_in",
    )(x, w3)


def _mm_resid(x, w3, resid, mod, layer, i_gate, *, tk, name):
    m, k = x.shape
    n = w3.shape[2]
    assert w3.shape[1] == k
    tm, tn = _tile(m, MM_TILE), _tile(n, MM_TILE)
    gate_blocks = n // tn
    tk = _tile(k, tk)
    if tk == k:
        return pl.pallas_call(
            _mm_resid_kernel,
            grid=(m // tm, n // tn),
            in_specs=[
                pl.BlockSpec((tm, k), lambda i, j: (i, 0)),
                pl.BlockSpec((None, k, tn), lambda i, j: (layer, 0, j)),
                pl.BlockSpec((tm, tn), lambda i, j: (i, j)),
                pl.BlockSpec((None, 1, tn), lambda i, j: (layer, 0, i_gate * gate_blocks + j)),
            ],
            out_specs=pl.BlockSpec((tm, tn), lambda i, j: (i, j)),
            out_shape=jax.ShapeDtypeStruct((m, n), F32),
            compiler_params=_params(("parallel", "parallel")),
            name=name,
        )(x, w3, resid, mod)
    return pl.pallas_call(
        _mm_resid_acc_kernel,
        grid=(m // tm, n // tn, k // tk),
        in_specs=[
            pl.BlockSpec((tm, tk), lambda i, j, kk: (i, kk)),
            pl.BlockSpec((None, tk, tn), lambda i, j, kk: (layer, kk, j)),
            pl.BlockSpec((tm, tn), lambda i, j, kk: (i, j)),
            pl.BlockSpec((None, 1, tn), lambda i, j, kk: (layer, 0, i_gate * gate_blocks + j)),
        ],
        out_specs=pl.BlockSpec((tm, tn), lambda i, j, kk: (i, j)),
        out_shape=jax.ShapeDtypeStruct((m, n), F32),
        scratch_shapes=[pltpu.VMEM((tm, tn), F32)],
        compiler_params=_params(("parallel", "parallel", "arbitrary")),
        name=name,
    )(x, w3, resid, mod)


def _merge_kernel(h_ref, og_ref, oa_ref, wga_ref, wgb_ref, bga_ref, bgb_ref, wa_ref, wb_ref, o_ref):
    h = h_ref[...]
    ga = _sigmoid(_dot(h, wga_ref[...]) + bga_ref[...])
    gb = _sigmoid(_dot(h, wgb_ref[...]) + bgb_ref[...])
    a = _dot(og_ref[...], wa_ref[...])
    b = _dot(oa_ref[...], wb_ref[...])
    o_ref[...] = (ga * a + gb * b).astype(o_ref.dtype)


def _merge(h, o_gla, o_attn, w_gate3, b_gate3, w_a3, w_b3, layer):
    m, d = h.shape
    tm, tn = _tile(m, MERGE_TILE), _tile(d, MERGE_TILE)
    nb = d // tn
    x_spec = lambda kdim: pl.BlockSpec((tm, kdim), lambda j, i: (i, 0))
    w_spec = lambda kdim, off: pl.BlockSpec((None, kdim, tn), lambda j, i: (layer, 0, j + off))
    b_spec = lambda off: pl.BlockSpec((None, 1, tn), lambda j, i: (layer, 0, j + off))
    return pl.pallas_call(
        _merge_kernel,
        grid=(nb, m // tm),
        in_specs=[
            x_spec(d), x_spec(o_gla.shape[1]), x_spec(o_attn.shape[1]),
            w_spec(d, 0), w_spec(d, nb), b_spec(0), b_spec(nb),
            w_spec(w_a3.shape[1], 0), w_spec(w_b3.shape[1], 0),
        ],
        out_specs=pl.BlockSpec((tm, tn), lambda j, i: (i, j)),
        out_shape=jax.ShapeDtypeStruct((m, d), BF16),
        compiler_params=_params(("parallel", "parallel")),
        name="gated_merge",
    )(h, o_gla, o_attn, w_gate3, w_gate3, b_gate3, b_gate3, w_a3, w_b3)


def _split_bf16(x, pieces):
    out = []
    for _ in range(pieces - 1):
        p = x.astype(BF16)
        out.append(p)
        x = x - p.astype(F32)
    out.append(x.astype(BF16))
    return out


def _gate_kernel(h_ref, wl_ref, wc_ref, ba_ref, b_ref, *, chunk, tau, rank):
    tm = h_ref.shape[0]
    x = _dot(h_ref[...], wl_ref[...])
    hi = x.astype(BF16).astype(F32)
    lane = lax.broadcasted_iota(jnp.int32, x.shape, 1)
    a = jnp.where(lane < 2 * rank, hi, x - hi).astype(BF16)
    logits = _dot(a, wc_ref[...]) + ba_ref[...]
    g = (jnp.minimum(logits, 0.0) - jnp.log(1.0 + jnp.exp(-jnp.abs(logits)))) * (LOG2_E / tau)
    ri = lax.broadcasted_iota(jnp.int32, (tm, tm), 0)
    ci = lax.broadcasted_iota(jnp.int32, (tm, tm), 1)
    tril = jnp.where(jnp.logical_and(ri >= ci, (ri ^ ci) < chunk), 1.0, 0.0).astype(BF16)
    p1, p2, p3 = _split_bf16(g, 3)
    b_ref[...] = _dot(tril, p1) + _dot(tril, p2) + _dot(tril, p3)


def _gla_gate(h, w_glr3, w_alpha_cat, b_alpha2, chunk, cfg):
    s, d = h.shape
    n = w_alpha_cat.shape[1]
    tm = _tile(s, max(ROW_TILE, chunk))
    assert tm % chunk == 0 and chunk & (chunk - 1) == 0 and 3 * cfg.gate_rank <= LANES
    return pl.pallas_call(
        functools.partial(_gate_kernel, chunk=chunk, tau=cfg.gate_tau, rank=cfg.gate_rank),
        grid=(s // tm,),
        in_specs=[
            pl.BlockSpec((tm, d), lambda i: (i, 0)),
            pl.BlockSpec((d, LANES), lambda i: (0, 0)),
            pl.BlockSpec((LANES, n), lambda i: (0, 0)),
            pl.BlockSpec((1, n), lambda i: (0, 0)),
        ],
        out_specs=pl.BlockSpec((tm, n), lambda i: (i, 0)),
        out_shape=jax.ShapeDtypeStruct((s, n), F32),
        compiler_params=_params(("parallel",)),
        name="gla_gate",
    )(h, w_glr3, w_alpha_cat, b_alpha2)


def _gla_block_kernel(q_ref, k_ref, v_ref, gr_ref, b_ref, gn_ref, o_ref, st_ref, *, heads, q_scale, eps):
    rows = q_ref.shape[0]
    dk = q_ref.shape[1] // heads
    dv = v_ref.shape[1] // heads
    sub = SUBLANES
    half = rows // 2

    @pl.when(pl.program_id(1) == 0)
    def _():
        st_ref[...] = jnp.zeros_like(st_ref)

    ri = lax.broadcasted_iota(jnp.int32, (half, half), 0)
    ci = lax.broadcasted_iota(jnp.int32, (half, half), 1)
    sub_i = lax.broadcasted_iota(jnp.int32, (sub, half), 0)
    lane_i = lax.broadcasted_iota(jnp.int32, (sub, half), 1)

    def cross_scores(q, k, b, g_blk):
        n = q.shape[0]
        span = 2 * g_blk
        zero = jnp.zeros((g_blk, dk), F32)
        q_parts, k_parts = [], []
        for s0 in range(0, n, span):
            ref = jnp.broadcast_to(b[s0 + g_blk - 1:s0 + g_blk, :], (g_blk, dk))
            lo, up = slice(s0, s0 + g_blk), slice(s0 + g_blk, s0 + span)
            q_parts += [zero, q[up] * jnp.exp2(b[up] - ref)]
            k_parts += [k[lo] * jnp.exp2(ref - b[lo]), zero]
        qg = jnp.concatenate(q_parts, axis=0).astype(BF16)
        kg = jnp.concatenate(k_parts, axis=0).astype(BF16)
        return _dot_nt(qg, kg)

    for hh in range(heads):
        ksl = slice(hh * dk, (hh + 1) * dk)
        vsl = slice(hh * dv, (hh + 1) * dv)

        q = q_ref[:, ksl].astype(F32) * q_scale
        k = k_ref[:, ksl].astype(F32)
        vb = v_ref[:, vsl]
        b = b_ref[:, ksl]
        b_last = b[rows - 1:rows, :]

        st = st_ref[hh]
        qe = (q * jnp.exp2(b)).astype(BF16)
        o = _dot_nt(qe, st.astype(BF16))

        b_mid = jnp.broadcast_to(b[half - 1:half, :], (half, dk))
        x_lo = _dot_nt((q[half:] * jnp.exp2(b[half:] - b_mid)).astype(BF16),
                       (k[:half] * jnp.exp2(b_mid - b[:half])).astype(BF16))
        a_halves = []
        for hf in range(2):
            rs = slice(hf * half, (hf + 1) * half)
            qh, kh, bh = q[rs], k[rs], b[rs]
            tiles = []
            for lo in range(0, half, sub):
                qi, ki, bi = qh[lo:lo + sub, :], kh[lo:lo + sub, :], bh[lo:lo + sub, :]
                acc = jnp.zeros((sub, half), F32)
                for jj in range(sub):
                    t = qi * ki[jj:jj + 1, :] * jnp.exp2(bi - bi[jj:jj + 1, :])
                    col = jnp.sum(t, axis=1, keepdims=True)
                    keep = jnp.logical_and(lane_i == lo + jj, sub_i >= jj)
                    acc = jnp.where(keep, col, acc)
                tiles.append(acc)
            a = jnp.concatenate(tiles, axis=0) + cross_scores(qh, kh, bh, half // 2)
            g_blk = half // 4
            while g_blk >= sub:
                same = (ri ^ ci) < 2 * g_blk
                a = a + jnp.where(same, cross_scores(qh, kh, bh, g_blk), 0.0)
                g_blk //= 2
            a_halves.append(a)
        top = jnp.concatenate([a_halves[0], jnp.zeros((half, half), F32)], axis=1)
        bot = jnp.concatenate([x_lo, a_halves[1]], axis=1)
        s = jnp.concatenate([top, bot], axis=0).astype(BF16)
        o = o + _dot(s, vb)

        kd = (k * jnp.exp2(b_last - b)).astype(BF16)
        st_ref[hh] = st * jnp.exp2(b_last) + _dot_tn(vb, kd)

        ms = jnp.mean(o * o, axis=-1, keepdims=True)
        y = o * lax.rsqrt(ms + eps) * gn_ref[...]
        gr = gr_ref[:, vsl].astype(F32)
        o_ref[:, vsl] = (y * (gr * _sigmoid(gr))).astype(o_ref.dtype)


def _gla_block(proj, b_cum, gla_norm2, rows, cfg, heads_per_step=4):
    s = proj.shape[0]
    nh, dk, dv = cfg.gla_heads, cfg.dk_head, cfg.dv_head
    hp = heads_per_step
    assert s % rows == 0 and nh % hp == 0 and rows & (rows - 1) == 0 and rows >= 8 * SUBLANES
    assert cfg.gla_dk % (hp * dk) == 0 and (2 * cfg.gla_dk) % (hp * dv) == 0
    k_off = cfg.gla_dk // (hp * dk)
    v_off = (2 * cfg.gla_dk) // (hp * dv)
    r_off = (2 * cfg.gla_dk + cfg.gla_dv) // (hp * dv)
    return pl.pallas_call(
        functools.partial(_gla_block_kernel, heads=hp, q_scale=float(dk) ** -0.5, eps=cfg.norm_eps),
        grid=(nh // hp, s // rows),
        in_specs=[
            pl.BlockSpec((rows, hp * dk), lambda h, t: (t, h)),
            pl.BlockSpec((rows, hp * dk), lambda h, t: (t, k_off + h)),
            pl.BlockSpec((rows, hp * dv), lambda h, t: (t, v_off + h)),
            pl.BlockSpec((rows, hp * dv), lambda h, t: (t, r_off + h)),
            pl.BlockSpec((rows, hp * dk), lambda h, t: (t, h)),
            pl.BlockSpec((1, dv), lambda h, t: (0, 0)),
        ],
        out_specs=pl.BlockSpec((rows, hp * dv), lambda h, t: (t, h)),
        out_shape=jax.ShapeDtypeStruct((s, cfg.gla_dv), BF16),
        scratch_shapes=[pltpu.VMEM((hp, dv, dk), F32)],
        compiler_params=_params(("parallel", "arbitrary")),
        name="gla",
    )(proj, proj, proj, proj, b_cum, gla_norm2)


def _attn_kernel(q_ref, kp_ref, kc_ref, vp_ref, vc_ref, o_ref, lse_ref, s_scr, p_scr, *, heads, hd, blk):
    nq = q_ref.shape[0] // blk
    n = pl.program_id(1)
    qi = lax.broadcasted_iota(jnp.int32, (blk, 2 * blk), 0)
    kj = lax.broadcasted_iota(jnp.int32, (blk, 2 * blk), 1)
    dist = blk + qi - kj
    band = jnp.logical_and(dist >= 0, dist <= blk)
    first = jnp.logical_and(band, jnp.logical_or(kj >= blk, n > 0))
    lane = lax.broadcasted_iota(jnp.int32, (blk, LANES), 1)
    for b in range(nq):
        rows = slice(b * blk, (b + 1) * blk)
        keys = slice((b - 1) * blk, (b + 1) * blk)
        mask = first if b == 0 else band
        slot = b % 2
        for hh in range(heads):
            sl = slice(hh * hd, (hh + 1) * hd)
            q = q_ref[rows, sl]
            if b == 0:
                s = jnp.concatenate([_dot_nt(q, kp_ref[:, sl]), _dot_nt(q, kc_ref[rows, sl])], axis=1)
            else:
                s = _dot_nt(q, kc_ref[keys, sl])
            s_scr[slot, hh] = jnp.where(mask, s, -jnp.inf)
        lse_tile = jnp.zeros((blk, LANES), F32)
        for hh in range(heads):
            s = s_scr[slot, hh]
            m = jnp.max(s, axis=1, keepdims=True)
            p = jnp.exp2(s - m)
            den = jnp.sum(p, axis=1, keepdims=True)
            p_scr[slot, hh] = (p * (1.0 / den)).astype(BF16)
            lse_tile = jnp.where(lane == hh, m * LN_2 + jnp.log(den), lse_tile)
        lse_ref[rows, :] = lse_tile
        for hh in range(heads):
            sl = slice(hh * hd, (hh + 1) * hd)
            p = p_scr[slot, hh]
            if b == 0:
                o = _dot(p[:, :blk], vp_ref[:, sl]) + _dot(p[:, blk:], vc_ref[rows, sl])
            else:
                o = _dot(p, vc_ref[keys, sl])
            o_ref[rows, sl] = o.astype(o_ref.dtype)


def _attn_group(qkv, window, dilation, cfg, name, blocks_per_step=8):
    dil, u, w3 = qkv.shape
    gw = cfg.group_width
    heads = cfg.attn_heads_per_group
    blk = window // dilation
    assert dil == dilation and w3 == 3 * gw and u % blk == 0 and heads <= LANES
    nb = u // blk
    nq = blocks_per_step
    while nb % nq:
        nq //= 2
    rows = nq * blk

    def cur(part):
        return pl.BlockSpec((None, rows, gw), lambda r, n: (r, n, part))

    def prev(part):
        return pl.BlockSpec((None, blk, gw), lambda r, n: (r, jnp.maximum(n * nq - 1, 0), part))

    return pl.pallas_call(
        functools.partial(_attn_kernel, heads=heads, hd=cfg.attn_head_dim, blk=blk),
        grid=(dilation, nb // nq),
        in_specs=[cur(0), prev(1), cur(1), prev(2), cur(2)],
        out_specs=[pl.BlockSpec((None, rows, gw), lambda r, n: (r, n, 0)),
                   pl.BlockSpec((None, rows, LANES), lambda r, n: (r, n, 0))],
        out_shape=[jax.ShapeDtypeStruct((dilation, u, gw), BF16),
                   jax.ShapeDtypeStruct((dilation, u, LANES), F32)],
        scratch_shapes=[pltpu.VMEM((2, heads, blk, 2 * blk), F32), pltpu.VMEM((2, heads, blk, 2 * blk), BF16)],
        compiler_params=_params(("parallel", "arbitrary")),
        name=name,
    )(qkv, qkv, qkv, qkv, qkv)


def _combine_kernel(*refs, dilations, hd):
    ng = len(dilations)
    o_refs, l_refs, out_ref = refs[:ng], refs[ng:2 * ng], refs[2 * ng]
    ts, gw = out_ref.shape
    perms = [_perm_matrix(ts, dil, BF16) if dil > 1 else None for dil in dilations]
    ls = []
    for gi in range(ng):
        l_g = l_refs[gi][...].reshape(ts, LANES)
        if perms[gi] is not None:
            l_g = functools.reduce(lambda a, b: a + b, [_dot(perms[gi], p) for p in _split_bf16(l_g, 3)])
        ls.append(l_g)
    m = functools.reduce(jnp.maximum, ls)
    es = [jnp.exp(l - m) for l in ls]
    inv = 1.0 / functools.reduce(lambda a, b: a + b, es)
    ws = [e * inv for e in es]
    width = 2 * hd if gw % (2 * hd) == 0 else hd
    for c0 in range(0, gw, width):
        o_c = []
        for gi in range(ng):
            o_g = o_refs[gi][:, :, c0:c0 + width].reshape(ts, width)
            o_c.append(_dot(perms[gi], o_g) if perms[gi] is not None else o_g.astype(F32))
        for c in range(c0, c0 + width, hd):
            hh = c // hd
            acc = None
            for gi in range(ng):
                term = ws[gi][:, hh:hh + 1] * o_c[gi][:, c - c0:c - c0 + hd]
                acc = term if acc is None else acc + term
            out_ref[:, c:c + hd] = acc.astype(out_ref.dtype)


def _combine(outs, lses, dilations, cfg):
    gw = cfg.group_width
    s = outs[0].shape[0] * outs[0].shape[1]
    ts = _tile(s, MERGE_TILE)
    in_specs = []
    for dil in dilations:
        assert ts % (dil * 2 * SUBLANES) == 0
        in_specs.append(pl.BlockSpec((dil, ts // dil, gw), lambda i: (0, i, 0)))
    for dil in dilations:
        in_specs.append(pl.BlockSpec((dil, ts // dil, LANES), lambda i: (0, i, 0)))
    return pl.pallas_call(
        functools.partial(_combine_kernel, dilations=tuple(dilations), hd=cfg.attn_head_dim),
        grid=(s // ts,),
        in_specs=in_specs,
        out_specs=pl.BlockSpec((ts, gw), lambda i: (i, 0)),
        out_shape=jax.ShapeDtypeStruct((s, gw), BF16),
        compiler_params=_params(("parallel",)),
        name="attn_combine",
    )(*outs, *lses)


def _forward(cfg, x, c, positions, ada_w, ada_b, norm_mix, norm_mlp, w_in, w_alpha_up, b_alpha,
             gla_norm, w_branch_gla, w_branch_attn, w_gate, b_gate, w_out, w_mlp_in, w_mlp_out,
             norm_final):
    bsz, s, d = x.shape
    assert bsz == 1 and d == cfg.d_model
    depth = cfg.depth
    x2 = x.reshape(s, d)
    mod = _ada_mod(c.reshape(d, 1), ada_w, ada_b.reshape(depth, 1, -1))
    dils = [dil for _, dil in cfg.dilated_groups]
    perm_dils = sorted({dil for dil in dils if dil > 1})
    pos = positions.reshape(s)
    pos_all = jnp.concatenate([pos.reshape(s // dil, dil).T.reshape(s) for dil in dils]).reshape(-1, 1)
    cos_t, sin_t = _rope_tables(pos_all, cfg)
    g_mix = norm_mix.reshape(depth, 1, d)
    g_mlp = norm_mlp.reshape(depth, 1, d)
    gw, rank = cfg.gla_width, cfg.gate_rank
    w_in_t = jnp.swapaxes(w_in, 1, 2).astype(BF16)
    w_attn_t = w_in_t[:, gw + rank:]
    w_gate_b = w_gate.astype(BF16)
    b_gate3 = b_gate.reshape(depth, 1, -1)
    w_bgla_b = w_branch_gla.astype(BF16)
    w_battn_b = w_branch_attn.astype(BF16)
    w_out_b = w_out.astype(BF16)
    w_mlp_in_b = w_mlp_in.astype(BF16)
    w_mlp_out_b = w_mlp_out.astype(BF16)

    for l in range(depth):
        w_glr = w_in[l, :, gw:gw + rank]
        w_glr3 = jnp.pad(jnp.concatenate([w_glr] * 3, axis=1), ((0, 0), (0, LANES - 3 * rank))).astype(BF16)
        wa_hi, wa_lo = _split_bf16(w_alpha_up[l], 2)
        w_alpha_cat = jnp.pad(jnp.concatenate([wa_hi, wa_lo, wa_hi], axis=0), ((0, LANES - 3 * rank), (0, 0)))

        hs = _norm_mod(x2, g_mix, mod, l, 0, 1, cfg, dilations=perm_dils)
        h = hs[0]
        h_by_dil = {1: h, **{dil: hp.reshape(s, d) for dil, hp in zip(perm_dils, hs[1:])}}
        proj = _gla_proj(h, w_in_t, l, gw)
        b_cum = _gla_gate(h, w_glr3, w_alpha_cat, b_alpha[l].reshape(1, -1), GLA_BLOCK, cfg)
        o_gla = _gla_block(proj, b_cum, gla_norm[l].reshape(1, -1), GLA_BLOCK, cfg)
        outs, lses = [], []
        for g, (window, dil) in enumerate(cfg.dilated_groups):
            qkv = _attn_proj(h_by_dil[dil], w_attn_t, l, cos_t, sin_t, g, cfg)
            o_g, lse_g = _attn_group(qkv.reshape(dil, s // dil, -1), window, dil, cfg, f"dilated_attn_{g}")
            outs.append(o_g)
            lses.append(lse_g)
        o_attn = _combine(outs, lses, dils, cfg)
        merged = _merge(h, o_gla, o_attn, w_gate_b, b_gate3, w_bgla_b, w_battn_b, l)
        x2 = _mm_resid(merged, w_out_b, x2, mod, l, 2, tk=MAX_CONTRACTION, name="out_proj")

        h = _norm_mod(x2, g_mlp, mod, l, 3, 4, cfg)[0]
        a = _mlp_in(h, w_mlp_in_b, l)
        x2 = _mm_resid(a, w_mlp_out_b, x2, mod, l, 5, tk=MAX_CONTRACTION // 2, name="mlp_out")

    return _final_norm(x2, norm_final.reshape(1, d), cfg).reshape(bsz, s, d)


def kernel(x, c, positions, ada_w, ada_b, norm_mix, norm_mlp, w_in, w_alpha_up, b_alpha, gla_norm,
           w_branch_gla, w_branch_attn, w_gate, b_gate, w_out, w_mlp_in, w_mlp_out, norm_final):
    return _forward(Config(), x, c, positions, ada_w, ada_b, norm_mix, norm_mlp, w_in, w_alpha_up,
                    b_alpha, gla_norm, w_branch_gla, w_branch_attn, w_gate, b_gate, w_out, w_mlp_in,
                    w_mlp_out, norm_final)
```
